```python
import math
import jax
import jax.numpy as jnp
from jax import lax
import numpy as np

D_MODEL = 2048
BATCH = 4
SEQ = 2048
DEPTH = 2

NORM_EPS = 1e-6

A_HEADS = 16
A_HEAD = 64
A_WIDTH = A_HEADS * A_HEAD
A_DECAY_LORA = 64
A_ICLR_LORA = 64
A_GATE_LORA = 160
A_VRES_LORA = 32
A_GN_EPS = 64e-5
A_IN = 3 * A_WIDTH + A_DECAY_LORA + A_ICLR_LORA + A_GATE_LORA

B_PAIRS = ((128, 1), (512, 4), (2048, 16))
B_GROUPS = len(B_PAIRS)
B_HEADS_PER_GROUP = 4
B_HEAD = 128
B_WIDTH = B_GROUPS * B_HEADS_PER_GROUP * B_HEAD
B_OUT = B_HEADS_PER_GROUP * B_HEAD
B_IN = 3 * B_WIDTH
B_QBLOCK = 64
ROPE_THETA = 500000.0
ROPE_DIM = B_HEAD // 4

C_HEADS = 8
C_HEAD_K = 128
C_HEAD_V = 128
C_KW = C_HEADS * C_HEAD_K
C_VW = C_HEADS * C_HEAD_V
C_CONV = 4
C_CHUNK = 64
C_IN = 2 * C_KW + C_VW + 2 * C_HEADS + C_VW

G_IN = 3 * D_MODEL
N_IN = A_IN + B_IN + C_IN + G_IN

D_FF = 5632
FFN_CONV = 3

kernel_name = 'hybrid_rwkv7_dilated_gdn_convffn'

F32 = jnp.float32


def rmsnorm(x, g, eps=NORM_EPS):
    xf = x.astype(F32)
    y = xf * lax.rsqrt(jnp.mean(xf * xf, axis=-1, keepdims=True) + eps)
    return (y * g.astype(F32)).astype(x.dtype)


def l2norm(t):
    t = t.astype(F32)
    return t / jnp.maximum(jnp.sqrt(jnp.sum(t * t, axis=-1, keepdims=True)), 1e-12)


def token_shift(x):
    return jnp.pad(x, ((0, 0), (1, 0), (0, 0)))[:, :-1]


def causal_dwconv(x, w):
    K, C = w.shape
    return lax.conv_general_dilated(x, w[:, None, :].astype(x.dtype), window_strides=(1,),
                                    padding=((K - 1, 0),), dimension_numbers=('NWC', 'WIO', 'NWC'),
                                    feature_group_count=C)


def partial_rope(x, pos):
    half = ROPE_DIM // 2
    inv = ROPE_THETA ** (-jnp.arange(half, dtype=F32) / half)
    ang = pos.astype(F32)[:, None] * inv[None, :]
    cos = jnp.cos(ang)[None, :, None, :]
    sin = jnp.sin(ang)[None, :, None, :]
    xr = x[..., :ROPE_DIM].astype(F32)
    x1, x2 = xr[..., :half], xr[..., half:]
    rot = jnp.concatenate([x1 * cos - x2 * sin, x2 * cos + x1 * sin], axis=-1).astype(x.dtype)
    return jnp.concatenate([rot, x[..., ROPE_DIM:]], axis=-1)


def rwkv7_mix(za, mu, w0, w2, a0, a2, g2, k_k, k_a, r_k, ln_w, ln_b, v_first, v_mix):
    Bt, S, _ = za.shape
    z = za.astype(F32)
    z = z + (token_shift(z) - z) * mu
    r, k, v, wd, ad, gd = jnp.split(z, [A_WIDTH, 2 * A_WIDTH, 3 * A_WIDTH, 3 * A_WIDTH + A_DECAY_LORA,
                                        3 * A_WIDTH + A_DECAY_LORA + A_ICLR_LORA], axis=-1)
    logw = -jax.nn.softplus(-(w0 + jnp.tanh(wd) @ w2)) - 0.5
    decay = jnp.exp(-jnp.exp(logw))
    a = jax.nn.sigmoid(a0 + ad @ a2)
    g = jax.nn.sigmoid(gd) @ g2
    heads = lambda t: t.reshape(Bt, S, A_HEADS, A_HEAD)
    kk = l2norm(heads(k * k_k))
    k = k * (1.0 + (a - 1.0) * k_a)
    if v_mix is not None:
        v = v + (v_first.astype(F32) - v) * v_mix
    rh, kh, vh, wh = heads(r), heads(k), heads(v), heads(decay)
    a_vec = -kk
    b_vec = kk * heads(a)
    xs = tuple(jnp.moveaxis(t, 1, 0) for t in (rh, wh, kh, vh, a_vec, b_vec))

    def step(state, inp):
        r_t, w_t, k_t, v_t, a_t, b_t = inp
        sa = jnp.einsum('bhvk,bhk->bhv', state, a_t)
        state = (state * w_t[:, :, None, :] + sa[..., None] * b_t[:, :, None, :]
                 + v_t[..., None] * k_t[:, :, None, :])
        return state, jnp.einsum('bhvk,bhk->bhv', state, r_t)

    s0 = jnp.zeros((Bt, A_HEADS, A_HEAD, A_HEAD), F32)
    _, y = lax.scan(step, s0, xs)
    y = jnp.moveaxis(y, 0, 1)
    mean = jnp.mean(y, axis=-1, keepdims=True)
    var = jnp.mean(jnp.square(y - mean), axis=-1, keepdims=True)
    y = ((y - mean) * lax.rsqrt(var + A_GN_EPS)).reshape(Bt, S, A_WIDTH) * ln_w + ln_b
    bonus = (jnp.sum(rh * kh * r_k, axis=-1, keepdims=True) * vh).reshape(Bt, S, A_WIDTH)
    return ((y + bonus) * g).astype(za.dtype), v.astype(za.dtype)


def dilated_attention(q, k, v):
    Bt, S = q.shape[:2]
    nb = S // B_QBLOCK
    scale = B_HEAD ** -0.5

    def block(bi):
        t0 = bi * B_QBLOCK
        qb = lax.dynamic_slice_in_dim(q, t0, B_QBLOCK, axis=1)
        t = t0 + jnp.arange(B_QBLOCK)
        scores, vals = [], []
        for gi, (win, dil) in enumerate(B_PAIRS):
            dist = dil * jnp.arange(win // dil + 1)
            idx = t[:, None] - dist[None, :]
            valid = idx >= 0
            idx = jnp.maximum(idx, 0)
            kg = jnp.take(k[:, :, gi], idx, axis=1)
            vg = jnp.take(v[:, :, gi], idx, axis=1)
            s = jnp.einsum('bqhd,bqjhd->bhqj', qb[:, :, gi], kg).astype(F32) * scale
            scores.append(jnp.where(valid[None, None], s, -jnp.inf))
            vals.append(vg)
        m = scores[0].max(-1)
        for s in scores[1:]:
            m = jnp.maximum(m, s.max(-1))
        num = jnp.zeros(qb.shape[:2] + qb.shape[3:], F32)
        den = jnp.zeros(m.shape, F32)
        for s, vg in zip(scores, vals):
            p = jnp.exp(s - m[..., None])
            den = den + p.sum(-1)
            num = num + jnp.einsum('bhqj,bqjhd->bqhd', p, vg.astype(F32))
        return (num / jnp.transpose(den, (0, 2, 1))[..., None]).astype(q.dtype)

    out = lax.map(block, jnp.arange(nb))
    return jnp.moveaxis(out, 0, 1).reshape(Bt, S, B_OUT)


def chunk_gated_delta_rule(q, k, v, g, beta):
    Bt, S, H, Dk = q.shape
    Dv = v.shape[-1]
    n = S // C_CHUNK
    ch = lambda t: jnp.transpose(t.reshape(Bt, n, C_CHUNK, H, t.shape[-1]), (0, 1, 3, 2, 4))
    ch2 = lambda t: jnp.transpose(t.reshape(Bt, n, C_CHUNK, H), (0, 1, 3, 2))
    q, k, v = ch(q), ch(k), ch(v)
    g, beta = ch2(g), ch2(beta)
    gam = jnp.cumsum(g, axis=-1)
    causal = jnp.tril(jnp.ones((C_CHUNK, C_CHUNK), bool))
    strict = jnp.tril(jnp.ones((C_CHUNK, C_CHUNK), bool), k=-1)
    decay = jnp.exp(jnp.where(causal, gam[..., :, None] - gam[..., None, :], -jnp.inf))
    Lmat = jnp.where(strict, beta[..., None] * jnp.einsum('bnhid,bnhjd->bnhij', k, k) * decay, 0.0)
    eye = jnp.eye(C_CHUNK, dtype=F32)
    rhs = jnp.concatenate([v * beta[..., None], k * (beta * jnp.exp(gam))[..., None]], axis=-1)
    sol = lax.linalg.triangular_solve(Lmat + eye, rhs, left_side=True, lower=True, unit_diagonal=True)
    u, w = sol[..., :Dv], sol[..., Dv:]
    attn = jnp.where(causal, jnp.einsum('bnhid,bnhjd->bnhij', q, k) * decay, 0.0)
    xs = tuple(jnp.moveaxis(t, 1, 0) for t in (q, k, u, w, attn, gam))

    def step(state, inp):
        qc, kc, uc, wc, ac, gc = inp
        v_new = uc - jnp.einsum('bhck,bhkv->bhcv', wc, state)
        o = (jnp.einsum('bhck,bhkv->bhcv', qc * jnp.exp(gc)[..., None], state)
             + jnp.einsum('bhij,bhjv->bhiv', ac, v_new))
        g_last = gc[..., -1]
        state = (state * jnp.exp(g_last)[..., None, None]
                 + jnp.einsum('bhck,bhcv->bhkv', kc * jnp.exp(g_last[..., None] - gc)[..., None], v_new))
        return state, o

    s0 = jnp.zeros((Bt, H, Dk, Dv), F32)
    _, o = lax.scan(step, s0, xs)
    return jnp.transpose(o, (1, 0, 3, 2, 4)).reshape(Bt, S, H, Dv)


def gated_deltanet(zc, conv_w, A_log, dt_bias, norm_w):
    Bt, S, _ = zc.shape
    q, k, v, beta_in, alpha_in, gate = jnp.split(
        zc, [C_KW, 2 * C_KW, 2 * C_KW + C_VW, 2 * C_KW + C_VW + C_HEADS, 2 * C_KW + C_VW + 2 * C_HEADS], axis=-1)
    qkv = jax.nn.silu(causal_dwconv(jnp.concatenate([q, k, v], axis=-1), conv_w))
    q, k, v = jnp.split(qkv, [C_KW, 2 * C_KW], axis=-1)
    q = l2norm(q.reshape(Bt, S, C_HEADS, C_HEAD_K)) * (C_HEAD_K ** -0.5)
    k = l2norm(k.reshape(Bt, S, C_HEADS, C_HEAD_K))
    v = v.reshape(Bt, S, C_HEADS, C_HEAD_V).astype(F32)
    beta = jax.nn.sigmoid(beta_in.astype(F32))
    g = -jnp.exp(A_log.astype(F32)) * jax.nn.softplus(alpha_in.astype(F32) + dt_bias)
    o = chunk_gated_delta_rule(q, k, v, g, beta)
    o = rmsnorm(o, norm_w) * jax.nn.silu(gate.reshape(Bt, S, C_HEADS, C_HEAD_V).astype(F32))
    return o.reshape(Bt, S, C_VW).astype(zc.dtype)


def setup_inputs(seed: int = 0) -> dict:
    key = jax.random.key(seed)
    ks = iter(jax.random.split(key, 40))
    nrm = lambda shape, scale: jax.random.normal(next(ks), shape, F32) * scale
    uni = lambda shape, lo, hi: jax.random.uniform(next(ks), shape, F32, lo, hi)
    L, D = DEPTH, D_MODEL
    LV = max(DEPTH - 1, 0)
    dt = jnp.exp(uni((L, C_HEADS), math.log(1e-3), math.log(0.1)))
    return {
        'x': nrm((BATCH, SEQ, D), 1.0),
        'attn_norm': 1.0 + nrm((L, D), 0.02),
        'w_in': nrm((L, D, N_IN), D ** -0.5),
        'rwkv_mu': uni((L, A_IN), 0.0, 1.0),
        'rwkv_w0': uni((L, A_WIDTH), -7.0, -2.0),
        'rwkv_w2': nrm((L, A_DECAY_LORA, A_WIDTH), 0.1),
        'rwkv_a0': nrm((L, A_WIDTH), 0.1),
        'rwkv_a2': nrm((L, A_ICLR_LORA, A_WIDTH), 0.5 * A_ICLR_LORA ** -0.5),
        'rwkv_g2': nrm((L, A_GATE_LORA, A_WIDTH), A_GATE_LORA ** -0.5),
        'rwkv_k_k': uni((L, A_WIDTH), 0.7, 1.0),
        'rwkv_k_a': uni((L, A_WIDTH), 0.8, 1.2),
        'rwkv_r_k': nrm((L, A_HEADS, A_HEAD), 0.1),
        'rwkv_ln_w': 1.0 + nrm((L, A_WIDTH), 0.02),
        'rwkv_ln_b': nrm((L, A_WIDTH), 0.02),
        'rwkv_v0': uni((LV, A_WIDTH), 0.0, 1.0),
        'rwkv_v1': nrm((LV, D, A_VRES_LORA), D ** -0.5),
        'rwkv_v2': nrm((LV, A_VRES_LORA, A_WIDTH), A_VRES_LORA ** -0.5),
        'gdn_conv': nrm((L, C_CONV, 2 * C_KW + C_VW), 0.5),
        'gdn_A_log': jnp.log(uni((L, C_HEADS), 1.0, 16.0)),
        'gdn_dt_bias': dt + jnp.log(-jnp.expm1(-dt)),
        'gdn_norm': 1.0 + nrm((L, C_HEAD_V), 0.02),
        'proj_a': nrm((L, A_WIDTH, D), A_WIDTH ** -0.5),
        'proj_b': nrm((L, B_OUT, D), B_OUT ** -0.5),
        'proj_c': nrm((L, C_VW, D), C_VW ** -0.5),
        'w_out': nrm((L, D, D), D ** -0.5),
        'ffn_norm': 1.0 + nrm((L, D), 0.02),
        'ffn_up': nrm((L, D, 2 * D_FF), D ** -0.5),
        'ffn_conv': nrm((L, FFN_CONV, 2 * D_FF), 0.3).at[:, -1].add(1.0),
        'ffn_down': nrm((L, D_FF, D), D_FF ** -0.5),
        'final_norm': 1.0 + nrm((D,), 0.02),
    }


def reference(x, attn_norm, w_in, rwkv_mu, rwkv_w0, rwkv_w2, rwkv_a0, rwkv_a2, rwkv_g2, rwkv_k_k, rwkv_k_a,
              rwkv_r_k, rwkv_ln_w, rwkv_ln_b, rwkv_v0, rwkv_v1, rwkv_v2, gdn_conv, gdn_A_log, gdn_dt_bias,
              gdn_norm, proj_a, proj_b, proj_c, w_out, ffn_norm, ffn_up, ffn_conv, ffn_down, final_norm):
    Bt, S, _ = x.shape
    pos = jnp.arange(S)
    v_first = None
    for l in range(DEPTH):
        h = rmsnorm(x, attn_norm[l])
        z = h @ w_in[l]
        za, zb, zc, zg = jnp.split(z, [A_IN, A_IN + B_IN, A_IN + B_IN + C_IN], axis=-1)
        if l == 0:
            v_mix = None
        else:
            v_mix = jax.nn.sigmoid(rwkv_v0[l - 1] + (h @ rwkv_v1[l - 1]) @ rwkv_v2[l - 1]).astype(F32)
        ya, v_l = rwkv7_mix(za, rwkv_mu[l], rwkv_w0[l], rwkv_w2[l], rwkv_a0[l], rwkv_a2[l], rwkv_g2[l],
                            rwkv_k_k[l], rwkv_k_a[l], rwkv_r_k[l], rwkv_ln_w[l], rwkv_ln_b[l], v_first, v_mix)
        if l == 0:
            v_first = v_l
        qb, kb, vb = jnp.split(zb, 3, axis=-1)
        shp = (Bt, S, B_GROUPS * B_HEADS_PER_GROUP, B_HEAD)
        qb = partial_rope(qb.reshape(shp), pos)
        kb = partial_rope(kb.reshape(shp), pos)
        gshp = (Bt, S, B_GROUPS, B_HEADS_PER_GROUP, B_HEAD)
        yb = dilated_attention(qb.reshape(gshp), kb.reshape(gshp), vb.reshape(gshp))
        yc = gated_deltanet(zc, gdn_conv[l], gdn_A_log[l], gdn_dt_bias[l], gdn_norm[l])
        ga, gb, gc = jnp.split(jax.nn.sigmoid(zg), 3, axis=-1)
        merged = ga * (ya @ proj_a[l]) + gb * (yb @ proj_b[l]) + gc * (yc @ proj_c[l])
        x = x + merged @ w_out[l]
        h = rmsnorm(x, ffn_norm[l])
        u = causal_dwconv(h @ ffn_up[l], ffn_conv[l])
        u_gate, u_val = jnp.split(u, 2, axis=-1)
        x = x + (jax.nn.silu(u_gate) * u_val) @ ffn_down[l]
    return rmsnorm(x, final_norm)
```

```python
import functools

import jax
import jax.numpy as jnp
from jax import lax
from jax.experimental import pallas as pl
from jax.experimental.pallas import tpu as pltpu

F32 = jnp.float32
BF16 = jnp.bfloat16

D_MODEL = 2048
DEPTH = 2
NORM_EPS = 1e-6

A_HEADS, A_HEAD = 16, 64
A_WIDTH = A_HEADS * A_HEAD
A_DECAY_LORA, A_ICLR_LORA, A_GATE_LORA, A_VRES_LORA = 64, 64, 160, 32
A_GN_EPS = 64e-5
A_IN = 3 * A_WIDTH + A_DECAY_LORA + A_ICLR_LORA + A_GATE_LORA

B_PAIRS = ((128, 1), (512, 4), (2048, 16))
B_GROUPS = 3
B_HEADS_PER_GROUP, B_HEAD = 4, 128
B_WIDTH = B_GROUPS * B_HEADS_PER_GROUP * B_HEAD
B_OUT = B_HEADS_PER_GROUP * B_HEAD
B_IN = 3 * B_WIDTH
ROPE_THETA = 500000.0
ROPE_DIM = B_HEAD // 4

C_HEADS, C_HEAD_K, C_HEAD_V = 8, 128, 128
C_KW = C_HEADS * C_HEAD_K
C_VW = C_HEADS * C_HEAD_V
C_CONV = 4
C_IN = 2 * C_KW + C_VW + 2 * C_HEADS + C_VW

D_FF = 5632
FFN_CONV = 3

LANES = 128
SUBLANES = 8
CHUNK = 64
VMEM_LIMIT = 56 * 1024 * 1024

ZA_GD = 3 * A_WIDTH
ZA_WA = ZA_GD + 256
ZA_LV = ZA_WA + 128
ZA_W = ZA_LV + 128
ZC_BA = 3 * C_KW + C_VW
ZC_W = ZC_BA + 128


def _cparams(sem):
    return pltpu.CompilerParams(dimension_semantics=sem, vmem_limit_bytes=VMEM_LIMIT)


def _dot(a, b):
    return jnp.dot(a.astype(BF16), b.astype(BF16), preferred_element_type=F32)


def _dot_nt(a, b):
    return lax.dot_general(a.astype(BF16), b.astype(BF16), (((1,), (1,)), ((), ())), preferred_element_type=F32)


def _dot_tn(a, b):
    return lax.dot_general(a.astype(BF16), b.astype(BF16), (((0,), (0,)), ((), ())), preferred_element_type=F32)


def _split(a):
    hi = a.astype(BF16)
    lo = (a - hi.astype(F32)).astype(BF16)
    return hi, lo


def _dot_sel_r(a, sel):
    hi, lo = _split(a)
    return (jnp.dot(hi, sel, preferred_element_type=F32) + jnp.dot(lo, sel, preferred_element_type=F32))


def _dot_sel_l(sel, a):
    hi, lo = _split(a)
    return (jnp.dot(sel, hi, preferred_element_type=F32) + jnp.dot(sel, lo, preferred_element_type=F32))


def _sigmoid(x):
    return 1.0 / (1.0 + jnp.exp(-x))


def _softplus(x):
    return jnp.maximum(x, 0.0) + jnp.log(1.0 + jnp.exp(-jnp.abs(x)))


def _iota2(shape, axis):
    return lax.broadcasted_iota(jnp.int32, shape, axis)


def _chunk_of(idx):
    return jnp.right_shift(idx, CHUNK.bit_length() - 1)


def _chunk_tri(n):
    ri, ci = _iota2((n, n), 0), _iota2((n, n), 1)
    return jnp.where((_chunk_of(ri) == _chunk_of(ci)) & (ri >= ci), 1.0, 0.0).astype(BF16)


def _neumann_inverse(n_mat, eye):
    p = eye + n_mat
    q = n_mat
    for _ in range(5):
        q = _dot(q, q)
        p = p + _dot(p, q)
    return p


def _rmsnorm_kernel(x_ref, g_ref, o_ref):
    x = x_ref[...]
    ms = jnp.mean(x * x, axis=-1, keepdims=True)
    o_ref[...] = ((x * lax.rsqrt(ms + NORM_EPS)) * g_ref[...]).astype(o_ref.dtype)


def rmsnorm(x, g, out_dtype, tm=512):
    m, d = x.shape
    return pl.pallas_call(
        _rmsnorm_kernel,
        grid=(m // tm,),
        in_specs=[pl.BlockSpec((tm, d), lambda i: (i, 0)), pl.BlockSpec((1, d), lambda i: (0, 0))],
        out_specs=pl.BlockSpec((tm, d), lambda i: (i, 0)),
        out_shape=jax.ShapeDtypeStruct((m, d), out_dtype),
        compiler_params=_cparams(("parallel",)),
        name="rmsnorm",
    )(x, g.reshape(1, d))


def _matmul_kernel(x_ref, w_ref, o_ref):
    o_ref[...] = jnp.dot(x_ref[...], w_ref[...], preferred_element_type=F32).astype(o_ref.dtype)


def _matmul_res_kernel(x_ref, w_ref, r_ref, o_ref):
    o_ref[...] = r_ref[...] + jnp.dot(x_ref[...], w_ref[...], preferred_element_type=F32)


def matmul(x, w, tn, tm=1024, residual=None, name="matmul"):
    m, k = x.shape
    n = w.shape[1]
    tm = min(tm, m)
    in_specs = [pl.BlockSpec((tm, k), lambda i, j: (i, 0)), pl.BlockSpec((k, tn), lambda i, j: (0, j))]
    args = [x, w]
    kern = _matmul_kernel
    if residual is not None:
        in_specs.append(pl.BlockSpec((tm, tn), lambda i, j: (i, j)))
        args.append(residual)
        kern = _matmul_res_kernel
    return pl.pallas_call(
        kern,
        grid=(m // tm, n // tn),
        in_specs=in_specs,
        out_specs=pl.BlockSpec((tm, tn), lambda i, j: (i, j)),
        out_shape=jax.ShapeDtypeStruct((m, n), F32),
        compiler_params=_cparams(("parallel", "parallel")),
        name=name,
    )(*args)


def _rwkv_kernel(r_ref, k_ref, v_ref, gd_ref, wa_ref, lv_ref, vf_ref, p_ref, mu2_ref, w2_ref, a2_ref, g2_ref,
                 v2_ref, y_ref, vout_ref, s_ref, br_ref, bk_ref, bv_ref, bgd_ref, bwa_ref, *, tb, has_vmix):
    i = pl.program_id(2)
    h = SUBLANES

    @pl.when(i == 0)
    def _():
        s_ref[...] = jnp.zeros_like(s_ref)
        for b in (br_ref, bk_ref, bv_ref, bgd_ref, bwa_ref):
            b[0:h, :] = jnp.zeros((h, b.shape[1]), F32)

    prm = p_ref[...]
    w0, a0, k_k, k_a, r_k, ln_w, ln_b, v0 = [prm[j:j + 1] for j in range(8)]
    mu_r, mu_k, mu_v = prm[8:9], prm[9:10], prm[10:11]
    mu2 = mu2_ref[...]
    mu_gd, mu_wa = mu2[:, 0:256], mu2[:, 256:384]

    def shifted_mix(x_ref, buf_ref, mu):
        x = x_ref[...]
        buf_ref[h:h + tb, :] = x
        xs = buf_ref[h - 1:h - 1 + tb, :]
        buf_ref[0:h, :] = x[tb - h:tb, :]
        return x + (xs - x) * mu

    r = shifted_mix(r_ref, br_ref, mu_r)
    k = shifted_mix(k_ref, bk_ref, mu_k)
    v = shifted_mix(v_ref, bv_ref, mu_v)
    gd = shifted_mix(gd_ref, bgd_ref, mu_gd)
    wa = shifted_mix(wa_ref, bwa_ref, mu_wa)

    lane = _iota2((1, LANES), 1)
    m0 = jnp.where(lane < A_HEAD, 1.0, 0.0)
    m1 = 1.0 - m0
    ri = _iota2((LANES, LANES), 0)
    ci = _iota2((LANES, LANES), 1)
    same = _chunk_of(ri) == _chunk_of(ci)
    strict = same & (ri > ci)
    incl = same & (ri >= ci)
    eye = jnp.where(ri == ci, 1.0, 0.0)
    ones_bd = jnp.where(same, 1.0, 0.0).astype(BF16)
    tri = _chunk_tri(tb)

    lw = -jnp.exp(-0.5) * _sigmoid(w0 + _dot(jnp.tanh(wa), w2_ref[...]))
    a = _sigmoid(a0 + _dot(wa, a2_ref[...]))
    g = _dot(_sigmoid(gd), g2_ref[...])
    kk = k * k_k
    kk = kk / jnp.maximum(jnp.sqrt(_dot_sel_r(kk * kk, ones_bd)), 1e-12)
    k = k * (1.0 + (a - 1.0) * k_a)
    if has_vmix:
        v_mix = _sigmoid(v0 + _dot(lv_ref[...], v2_ref[...]))
        v = v + (vf_ref[...] - v) * v_mix
    vout_ref[...] = v

    cum = _dot_sel_l(tri, lw)
    e_pos = jnp.exp(cum)
    e_neg = jnp.exp(-cum)
    r_t = r * e_pos
    a_t = -kk * jnp.exp(cum - lw)
    b_t = (kk * a) * e_neg
    k_t = k * e_neg

    def stack_masked(x):
        return jnp.concatenate([x * m0, x * m1], axis=0)

    def stack_dup(x):
        return jnp.concatenate([x, x], axis=0)

    for c in range(tb // CHUNK):
        sl = slice(c * CHUNK, (c + 1) * CHUNK)
        cum_last = cum[(c + 1) * CHUNK - 1:(c + 1) * CHUNK, :]
        e_end = jnp.exp(cum_last - cum[sl])
        left = jnp.concatenate([stack_masked(a_t[sl]), stack_masked(r_t[sl])], axis=0)
        right = jnp.concatenate([stack_dup(b_t[sl]), stack_dup(k_t[sl])], axis=0)
        gm = _dot_nt(left, right)
        n_ab = jnp.where(strict, gm[0:LANES, 0:LANES], 0.0)
        g_ak = jnp.where(strict, gm[0:LANES, LANES:], 0.0)
        g_rb = jnp.where(incl, gm[LANES:, 0:LANES], 0.0)
        g_rk = jnp.where(incl, gm[LANES:, LANES:], 0.0)
        t_inv = _neumann_inverse(n_ab, eye)
        v_s = stack_masked(v[sl])
        s = s_ref[...]
        x0 = _dot_nt(left, s)
        u_s = _dot(t_inv, x0[0:LANES] + _dot(g_ak, v_s))
        uv = jnp.concatenate([u_s, v_s], axis=0)
        y_s = x0[LANES:] + _dot(jnp.concatenate([g_rb, g_rk], axis=1), uv)
        bk_end = jnp.concatenate([stack_dup((kk * a)[sl] * e_end), stack_dup(k[sl] * e_end)], axis=0)
        s_new = s * jnp.exp(cum_last) + _dot_tn(uv, bk_end)
        s_ref[...] = jnp.where(same, s_new, 0.0)
        y_ref[sl, :] = y_s[0:CHUNK] + y_s[CHUNK:]

    y = y_ref[...]
    inv_n = 1.0 / A_HEAD
    mean = _dot_sel_r(y, ones_bd) * inv_n
    d = y - mean
    var = _dot_sel_r(d * d, ones_bd) * inv_n
    yn = d * lax.rsqrt(var + A_GN_EPS) * ln_w + ln_b
    bonus = _dot_sel_r(r * k * r_k, ones_bd) * v
    y_ref[...] = (yn + bonus) * g


def rwkv_mix(za, v_first, prm, mu2, w2p, a2p, g2p, v2p, *, batch, seq, tb=256):
    has_vmix = v_first is not None
    nt = seq // tb
    npair = A_WIDTH // LANES
    row = lambda b, p, i: b * nt + i
    if v_first is None:
        v_first = za
    in_specs = [
        pl.BlockSpec((tb, LANES), lambda b, p, i: (row(b, p, i), p)),
        pl.BlockSpec((tb, LANES), lambda b, p, i: (row(b, p, i), npair + p)),
        pl.BlockSpec((tb, LANES), lambda b, p, i: (row(b, p, i), 2 * npair + p)),
        pl.BlockSpec((tb, 256), lambda b, p, i: (row(b, p, i), ZA_GD // 256)),
        pl.BlockSpec((tb, LANES), lambda b, p, i: (row(b, p, i), ZA_WA // LANES)),
        pl.BlockSpec((tb, LANES), lambda b, p, i: (row(b, p, i), ZA_LV // LANES)),
        pl.BlockSpec((tb, LANES), lambda b, p, i: (row(b, p, i), p)),
        pl.BlockSpec((16, LANES), lambda b, p, i: (0, p)),
        pl.BlockSpec((1, 384), lambda b, p, i: (0, 0)),
        pl.BlockSpec((LANES, LANES), lambda b, p, i: (0, p)),
        pl.BlockSpec((LANES, LANES), lambda b, p, i: (0, p)),
        pl.BlockSpec((256, LANES), lambda b, p, i: (0, p)),
        pl.BlockSpec((LANES, LANES), lambda b, p, i: (0, p)),
    ]
    out_spec = pl.BlockSpec((tb, LANES), lambda b, p, i: (row(b, p, i), p))
    out_sds = jax.ShapeDtypeStruct((batch * seq, A_WIDTH), F32)
    return pl.pallas_call(
        functools.partial(_rwkv_kernel, tb=tb, has_vmix=has_vmix),
        grid=(batch, npair, nt),
        in_specs=in_specs,
        out_specs=[out_spec, out_spec],
        out_shape=[out_sds, out_sds],
        scratch_shapes=[
            pltpu.VMEM((LANES, LANES), F32),
            pltpu.VMEM((tb + SUBLANES, LANES), F32),
            pltpu.VMEM((tb + SUBLANES, LANES), F32),
            pltpu.VMEM((tb + SUBLANES, LANES), F32),
            pltpu.VMEM((tb + SUBLANES, 256), F32),
            pltpu.VMEM((tb + SUBLANES, LANES), F32),
        ],
        compiler_params=_cparams(("parallel", "parallel", "arbitrary")),
        name="rwkv7",
    )(za, za, za, za, za, za, v_first, prm, mu2, w2p, a2p, g2p, v2p)


def _attn_kernel(q_ref, kc_ref, kp_ref, vc_ref, vp_ref, tq_ref, tp_ref, o_ref, lse_ref, *, tq):
    i = pl.program_id(2)
    scale = B_HEAD ** -0.5

    def rope(x, tab):
        half = ROPE_DIM // 2
        return (x * tab[:, 0:LANES] + pltpu.roll(x, half, axis=1) * tab[:, LANES:2 * LANES]
                + pltpu.roll(x, LANES - half, axis=1) * tab[:, 2 * LANES:3 * LANES])

    tab_q = tq_ref[...]
    tab_p = tp_ref[...]
    ri = _iota2((tq, 2 * tq), 0)
    ci = _iota2((tq, 2 * tq), 1)
    no_prev = jnp.where(i > 0, 0, 2 * tq)
    valid = ((ci < tq) & (ci >= ri + no_prev)) | ((ci >= tq) & ((ci - tq) <= ri))
    for hh in range(B_HEADS_PER_GROUP):
        ls = slice(hh * B_HEAD, (hh + 1) * B_HEAD)
        q = rope(q_ref[:, ls], tab_q) * scale
        kc = rope(kc_ref[:, ls], tab_q)
        kp = rope(kp_ref[:, ls], tab_p)
        s = _dot_nt(q, jnp.concatenate([kp, kc], axis=0))
        s = jnp.where(valid, s, -1e30)
        m = jnp.max(s, axis=-1, keepdims=True)
        p = jnp.exp(s - m)
        den = jnp.sum(p, axis=-1, keepdims=True)
        num = _dot(p, jnp.concatenate([vp_ref[:, ls], vc_ref[:, ls]], axis=0))
        o_ref[:, ls] = num / den
        lse_ref[:, ls] = jnp.broadcast_to(m + jnp.log(den), (tq, B_HEAD))


def dilated_attention_group(zb, rope_tab, gi, *, batch, seq, tq=128):
    win, dil = B_PAIRS[gi]
    assert win // dil == tq
    sub = seq // dil
    nq = sub // tq
    w512 = B_OUT
    ncol = B_IN // w512
    zv = zb.reshape(batch * sub, dil * B_IN)
    tabv = rope_tab.reshape(sub, dil * 3 * LANES)
    cur = lambda b, r, i: b * nq + i
    prev = lambda b, r, i: b * nq + jnp.maximum(i - 1, 0)
    in_specs = [
        pl.BlockSpec((tq, w512), lambda b, r, i: (cur(b, r, i), r * ncol + gi)),
        pl.BlockSpec((tq, w512), lambda b, r, i: (cur(b, r, i), r * ncol + B_GROUPS + gi)),
        pl.BlockSpec((tq, w512), lambda b, r, i: (prev(b, r, i), r * ncol + B_GROUPS + gi)),
        pl.BlockSpec((tq, w512), lambda b, r, i: (cur(b, r, i), r * ncol + 2 * B_GROUPS + gi)),
        pl.BlockSpec((tq, w512), lambda b, r, i: (prev(b, r, i), r * ncol + 2 * B_GROUPS + gi)),
        pl.BlockSpec((tq, 3 * LANES), lambda b, r, i: (i, r)),
        pl.BlockSpec((tq, 3 * LANES), lambda b, r, i: (jnp.maximum(i - 1, 0), r)),
    ]
    out_spec = pl.BlockSpec((tq, w512), lambda b, r, i: (cur(b, r, i), r))
    out_sds = jax.ShapeDtypeStruct((batch * sub, dil * w512), F32)
    o, lse = pl.pallas_call(
        functools.partial(_attn_kernel, tq=tq),
        grid=(batch, dil, nq),
        in_specs=in_specs,
        out_specs=[out_spec, out_spec],
        out_shape=[out_sds, out_sds],
        compiler_params=_cparams(("parallel", "parallel", "arbitrary")),
        name=f"dilated_attn_g{gi}",
    )(zv, zv, zv, zv, zv, tabv, tabv)
    return o.reshape(batch * seq, w512), lse.reshape(batch * seq, w512)


def rope_table(seq):
    half = ROPE_DIM // 2
    inv = ROPE_THETA ** (-jnp.arange(half, dtype=F32) / half)
    ang = jnp.arange(seq, dtype=F32)[:, None] * inv[None, :]
    cos, sin = jnp.cos(ang), jnp.sin(ang)
    z = jnp.zeros((seq, LANES - ROPE_DIM), F32)
    zh = jnp.zeros((seq, half), F32)
    c_tab = jnp.concatenate([cos, cos, jnp.ones_like(z)], axis=1)
    s_pos = jnp.concatenate([zh, sin, z], axis=1)
    s_neg = jnp.concatenate([-sin, zh, z], axis=1)
    return jnp.concatenate([c_tab, s_pos, s_neg], axis=1)


def _gdn_kernel(q_ref, k_ref, v_ref, gate_ref, ba_ref, cw_ref, p_ref, o_ref, s_ref, bq_ref, bk_ref, bv_ref,
                *, tb):
    pidx = pl.program_id(1)
    i = pl.program_id(2)
    h = SUBLANES
    hd = C_HEAD_K

    @pl.when(i == 0)
    def _():
        s_ref[...] = jnp.zeros_like(s_ref)
        for b in (bq_ref, bk_ref, bv_ref):
            b[0:h, :] = jnp.zeros((h, b.shape[1]), F32)

    cw = cw_ref[...]
    prm = p_ref[...]

    def conv_silu(x_ref, buf_ref, w):
        x = x_ref[...]
        buf_ref[h:h + tb, :] = x
        acc = x * w[C_CONV - 1:C_CONV]
        for j in range(C_CONV - 1):
            off = h - (C_CONV - 1) + j
            acc = acc + buf_ref[off:off + tb, :] * w[j:j + 1]
        buf_ref[0:h, :] = x[tb - h:tb, :]
        return acc * _sigmoid(acc)

    q = conv_silu(q_ref, bq_ref, cw[0])
    k = conv_silu(k_ref, bk_ref, cw[1])
    v = conv_silu(v_ref, bv_ref, cw[2])

    ri = _iota2((LANES, LANES), 0)
    ci = _iota2((LANES, LANES), 1)
    same = _chunk_of(ri) == _chunk_of(ci)
    strict = same & (ri > ci)
    incl = same & (ri >= ci)
    eye = jnp.where(ri == ci, 1.0, 0.0)
    ones = jnp.ones((hd, hd), BF16)
    tri = _chunk_tri(tb)

    ba = ba_ref[...]
    qh, kh, vh, beta, gam = [], [], [], [], []
    for hh in range(2):
        ls = slice(hh * hd, (hh + 1) * hd)
        qq, kx = q[:, ls], k[:, ls]
        qh.append(qq / jnp.maximum(jnp.sqrt(_dot_sel_r(qq * qq, ones)), 1e-12) * (hd ** -0.5))
        kh.append(kx / jnp.maximum(jnp.sqrt(_dot_sel_r(kx * kx, ones)), 1e-12))
        vh.append(v[:, ls])
        head = 2 * pidx + hh
        sel_b = jnp.where(ri == head, 1.0, 0.0).astype(BF16)
        sel_a = jnp.where(ri == head + C_HEADS, 1.0, 0.0).astype(BF16)
        beta.append(_sigmoid(_dot_sel_r(ba, sel_b)))
        glog = -jnp.exp(prm[0:1, ls]) * _softplus(_dot_sel_r(ba, sel_a) + prm[1:2, ls])
        gam.append(_dot_sel_l(tri, glog))

    for c in range(tb // CHUNK):
        sl = slice(c * CHUNK, (c + 1) * CHUNK)
        stack = lambda xs: jnp.concatenate([xs[0][sl], xs[1][sl]], axis=0)
        k_s, q_s, v_s, beta_s, gam_s = stack(kh), stack(qh), stack(vh), stack(beta), stack(gam)
        kq = _dot_nt(jnp.concatenate([k_s, q_s], axis=0), k_s)
        dm = jnp.exp(jnp.where(incl, gam_s - gam_s.T, -1e30))
        n_mat = jnp.where(strict, -(beta_s * kq[0:LANES] * dm), 0.0)
        attn = kq[LANES:] * dm
        t_inv = _neumann_inverse(n_mat, eye)
        e_gam = jnp.exp(gam_s)
        rhs = jnp.concatenate([v_s * beta_s, k_s * (beta_s * e_gam)], axis=1)
        sol = _dot(t_inv, rhs)
        u_s, w_s = sol[:, 0:hd], sol[:, hd:]
        qg_s = q_s * e_gam
        v_new, o_inter = [], []
        for hh in range(2):
            hs = slice(hh * CHUNK, (hh + 1) * CHUNK)
            st = s_ref[hh]
            ws = _dot(jnp.concatenate([w_s[hs], qg_s[hs]], axis=0), st)
            v_new.append(u_s[hs] - ws[0:CHUNK])
            o_inter.append(ws[CHUNK:])
        v_new_s = jnp.concatenate(v_new, axis=0)
        o_s = jnp.concatenate(o_inter, axis=0) + _dot(attn, v_new_s)
        for hh in range(2):
            hs = slice(hh * CHUNK, (hh + 1) * CHUNK)
            g_h = gam[hh][sl]
            g_last = g_h[CHUNK - 1:CHUNK, :]
            s_ref[hh] = s_ref[hh] * jnp.exp(g_last) + _dot_tn(kh[hh][sl] * jnp.exp(g_last - g_h), v_new[hh])
            o_ref[sl, hh * hd:(hh + 1) * hd] = o_s[hs]

    gate = gate_ref[...]
    for hh in range(2):
        ls = slice(hh * hd, (hh + 1) * hd)
        o = o_ref[:, ls]
        ms = _dot_sel_r(o * o, ones) * (1.0 / hd)
        gt = gate[:, ls]
        o_ref[:, ls] = (o * lax.rsqrt(ms + NORM_EPS) * prm[2:3, ls]) * (gt * _sigmoid(gt))


def gated_deltanet(zc, conv_w, prm, *, batch, seq, tb=256):
    nt = seq // tb
    npair = C_HEADS // 2
    wblk = 2 * C_HEAD_K
    row = lambda b, p, i: b * nt + i
    in_specs = [
        pl.BlockSpec((tb, wblk), lambda b, p, i: (row(b, p, i), p)),
        pl.BlockSpec((tb, wblk), lambda b, p, i: (row(b, p, i), npair + p)),
        pl.BlockSpec((tb, wblk), lambda b, p, i: (row(b, p, i), 2 * npair + p)),
        pl.BlockSpec((tb, wblk), lambda b, p, i: (row(b, p, i), 3 * npair + p)),
        pl.BlockSpec((tb, LANES), lambda b, p, i: (row(b, p, i), ZC_BA // LANES)),
        pl.BlockSpec((3, C_CONV, wblk), lambda b, p, i: (0, 0, p)),
        pl.BlockSpec((SUBLANES, wblk), lambda b, p, i: (0, p)),
    ]
    return pl.pallas_call(
        functools.partial(_gdn_kernel, tb=tb),
        grid=(batch, npair, nt),
        in_specs=in_specs,
        out_specs=pl.BlockSpec((tb, wblk), lambda b, p, i: (row(b, p, i), p)),
        out_shape=jax.ShapeDtypeStruct((batch * seq, C_VW), F32),
        scratch_shapes=[
            pltpu.VMEM((2, C_HEAD_K, C_HEAD_V), F32),
            pltpu.VMEM((tb + SUBLANES, wblk), F32),
            pltpu.VMEM((tb + SUBLANES, wblk), F32),
            pltpu.VMEM((tb + SUBLANES, wblk), F32),
        ],
        compiler_params=_cparams(("parallel", "parallel", "arbitrary")),
        name="gated_deltanet",
    )(zc, zc, zc, zc, zc, conv_w, prm)


def _merge_kernel(ya_ref, o0_ref, o1_ref, o2_ref, l0_ref, l1_ref, l2_ref, yc_ref, ga_ref, gb_ref, gc_ref,
                  pa_ref, pb_ref, pc_ref, out_ref):
    l0, l1, l2 = l0_ref[...], l1_ref[...], l2_ref[...]
    m = jnp.maximum(jnp.maximum(l0, l1), l2)
    w0, w1, w2 = jnp.exp(l0 - m), jnp.exp(l1 - m), jnp.exp(l2 - m)
    yb = (w0 * o0_ref[...] + w1 * o1_ref[...] + w2 * o2_ref[...]) / (w0 + w1 + w2)
    merged = (_sigmoid(ga_ref[...]) * _dot(ya_ref[...], pa_ref[...])
              + _sigmoid(gb_ref[...]) * _dot(yb, pb_ref[...])
              + _sigmoid(gc_ref[...]) * _dot(yc_ref[...], pc_ref[...]))
    out_ref[...] = merged.astype(out_ref.dtype)


def merge_mixers(ya, attn, yc, zg, pa, pb, pc, tm=256):
    m = ya.shape[0]
    d = D_MODEL
    rows = lambda w: pl.BlockSpec((tm, w), lambda i: (i, 0))
    const = lambda a: pl.BlockSpec(a.shape, lambda i: (0, 0), pipeline_mode=pl.Buffered(1))
    (o0, l0), (o1, l1), (o2, l2) = attn
    in_specs = ([rows(A_WIDTH)] + [rows(B_OUT)] * 6 + [rows(C_VW)]
                + [pl.BlockSpec((tm, d), lambda i, j=j: (i, j)) for j in range(3)]
                + [const(pa), const(pb), const(pc)])
    return pl.pallas_call(
        _merge_kernel,
        grid=(m // tm,),
        in_specs=in_specs,
        out_specs=pl.BlockSpec((tm, d), lambda i: (i, 0)),
        out_shape=jax.ShapeDtypeStruct((m, d), BF16),
        compiler_params=_cparams(("parallel",)),
        name="merge_mixers",
    )(ya, o0, o1, o2, l0, l1, l2, yc, zg, zg, zg, pa, pb, pc)


def _ffn_tail_kernel(ug_ref, uv_ref, hg_ref, hv_ref, cw_ref, wd_ref, x_ref, fn_ref, o_ref, acc_ref, bg_ref, bv_ref,
                     *, tm, rows_per_seq, final_norm):
    i = pl.program_id(0)
    j = pl.program_id(1)
    h = SUBLANES
    first = (i % rows_per_seq) == 0

    @pl.when(j == 0)
    def _():
        acc_ref[...] = jnp.zeros_like(acc_ref)

    cw = cw_ref[...]

    def conv(u_ref, halo_ref, buf_ref, w):
        u = u_ref[...]
        buf_ref[0:h, :] = jnp.where(first, 0.0, halo_ref[...])
        buf_ref[h:h + tm, :] = u
        acc = u * w[FFN_CONV - 1:FFN_CONV]
        for t in range(FFN_CONV - 1):
            off = h - (FFN_CONV - 1) + t
            acc = acc + buf_ref[off:off + tm, :] * w[t:t + 1]
        return acc

    cg = conv(ug_ref, hg_ref, bg_ref, cw[0])
    cv = conv(uv_ref, hv_ref, bv_ref, cw[1])
    act = (cg * _sigmoid(cg)) * cv
    acc_ref[...] += _dot(act, wd_ref[...])

    @pl.when(j == pl.num_programs(1) - 1)
    def _():
        y = x_ref[...] + acc_ref[...]
        if final_norm:
            ms = jnp.mean(y * y, axis=-1, keepdims=True)
            y = (y * lax.rsqrt(ms + NORM_EPS)) * fn_ref[...]
        o_ref[...] = y


def ffn_tail(u, conv_w, w_down, x, final_g, *, seq, final_norm, tm=512, tf=512):
    m = u.shape[0]
    d = D_MODEL
    nf = D_FF // tf
    hb = tm // SUBLANES
    halo = lambda i: jnp.maximum(i * hb - 1, 0)
    in_specs = [
        pl.BlockSpec((tm, tf), lambda i, j: (i, j)),
        pl.BlockSpec((tm, tf), lambda i, j: (i, nf + j)),
        pl.BlockSpec((SUBLANES, tf), lambda i, j: (halo(i), j)),
        pl.BlockSpec((SUBLANES, tf), lambda i, j: (halo(i), nf + j)),
        pl.BlockSpec((2, FFN_CONV, tf), lambda i, j: (0, 0, j)),
        pl.BlockSpec((tf, d), lambda i, j: (j, 0)),
        pl.BlockSpec((tm, d), lambda i, j: (i, 0)),
        pl.BlockSpec((1, d), lambda i, j: (0, 0)),
    ]
    return pl.pallas_call(
        functools.partial(_ffn_tail_kernel, tm=tm, rows_per_seq=seq // tm, final_norm=final_norm),
        grid=(m // tm, nf),
        in_specs=in_specs,
        out_specs=pl.BlockSpec((tm, d), lambda i, j: (i, 0)),
        out_shape=jax.ShapeDtypeStruct((m, d), F32),
        scratch_shapes=[
            pltpu.VMEM((tm, d), F32),
            pltpu.VMEM((tm + SUBLANES, tf), F32),
            pltpu.VMEM((tm + SUBLANES, tf), F32),
        ],
        compiler_params=_cparams(("parallel", "arbitrary")),
        name="ffn_tail",
    )(u, u, u, u, conv_w, w_down, x, final_g.reshape(1, d))


def _pad_cols(w, width):
    return jnp.pad(w, ((0, 0), (0, width - w.shape[1])))


def _pad_rows(w, rows, at=0):
    return jnp.pad(w, ((at, rows - at - w.shape[0]), (0, 0)))


def _layer_params(l, p):
    w_in = p["w_in"][l]
    a_end, b_end, c_end = A_IN, A_IN + B_IN, A_IN + B_IN + C_IN
    aw = 3 * A_WIDTH
    wa_cols = w_in[:, :a_end]
    lora_v = (p["rwkv_v1"][l - 1] if l > 0 else jnp.zeros((D_MODEL, A_VRES_LORA), F32))
    w_za = jnp.concatenate([
        wa_cols[:, :aw],
        _pad_cols(wa_cols[:, aw + A_DECAY_LORA + A_ICLR_LORA:], 256),
        wa_cols[:, aw:aw + A_DECAY_LORA + A_ICLR_LORA],
        _pad_cols(lora_v, 128)], axis=1)
    wc_cols = w_in[:, b_end:c_end]
    qkv_w = 2 * C_KW + C_VW
    w_zc = jnp.concatenate([
        wc_cols[:, :qkv_w],
        wc_cols[:, qkv_w + 2 * C_HEADS:],
        _pad_cols(wc_cols[:, qkv_w:qkv_w + 2 * C_HEADS], 128)], axis=1)
    mu = p["rwkv_mu"][l]
    rows = [p["rwkv_w0"][l], p["rwkv_a0"][l], p["rwkv_k_k"][l], p["rwkv_k_a"][l], p["rwkv_r_k"][l].reshape(-1),
            p["rwkv_ln_w"][l], p["rwkv_ln_b"][l],
            (p["rwkv_v0"][l - 1] if l > 0 else jnp.zeros((A_WIDTH,), F32)),
            mu[:A_WIDTH], mu[A_WIDTH:2 * A_WIDTH], mu[2 * A_WIDTH:aw]]
    rwkv_prm = jnp.pad(jnp.stack(rows), ((0, 16 - len(rows)), (0, 0)))
    mu2 = jnp.concatenate([_pad_cols(mu[None, aw + 128:], 256), mu[None, aw:aw + 128]], axis=1)
    v2 = (p["rwkv_v2"][l - 1] if l > 0 else jnp.zeros((A_VRES_LORA, A_WIDTH), F32))
    bcast = lambda t: jnp.repeat(t, C_HEAD_K)
    gdn_prm = jnp.pad(jnp.stack([bcast(p["gdn_A_log"][l]), bcast(p["gdn_dt_bias"][l]),
                                 jnp.tile(p["gdn_norm"][l], C_HEADS)]), ((0, SUBLANES - 3), (0, 0)))
    fc = p["ffn_conv"][l]
    return dict(
        w_za=w_za.astype(BF16), w_zb=w_in[:, a_end:b_end].astype(BF16), w_zc=w_zc.astype(BF16),
        w_zg=w_in[:, c_end:].astype(BF16),
        rwkv_prm=rwkv_prm, mu2=mu2,
        w2p=_pad_rows(p["rwkv_w2"][l], 128, 0).astype(BF16),
        a2p=_pad_rows(p["rwkv_a2"][l], 128, A_DECAY_LORA).astype(BF16),
        g2p=_pad_rows(p["rwkv_g2"][l], 256, 0).astype(BF16),
        v2p=_pad_rows(v2, 128, 0).astype(BF16),
        gdn_conv=p["gdn_conv"][l].reshape(C_CONV, 3, C_KW).transpose(1, 0, 2),
        gdn_prm=gdn_prm,
        pa=p["proj_a"][l].astype(BF16), pb=p["proj_b"][l].astype(BF16), pc=p["proj_c"][l].astype(BF16),
        w_out=p["w_out"][l].astype(BF16),
        ffn_up=p["ffn_up"][l].astype(BF16),
        ffn_conv=fc.reshape(FFN_CONV, 2, D_FF).transpose(1, 0, 2),
        ffn_down=p["ffn_down"][l].astype(BF16),
    )


def _forward(x, p):
    batch, seq, d = x.shape
    m = batch * seq
    xf = x.reshape(m, d)
    tab = rope_table(seq)
    v_first = None
    for l in range(DEPTH):
        lp = _layer_params(l, p)
        h = rmsnorm(xf, p["attn_norm"][l], BF16)
        za = matmul(h, lp["w_za"], tn=512, name="in_proj_a")
        zb = matmul(h, lp["w_zb"], tn=512, name="in_proj_b")
        zc = matmul(h, lp["w_zc"], tn=1408, name="in_proj_c")
        zg = matmul(h, lp["w_zg"], tn=512, name="in_proj_g")
        ya, v_l = rwkv_mix(za, v_first, lp["rwkv_prm"], lp["mu2"], lp["w2p"], lp["a2p"], lp["g2p"], lp["v2p"],
                           batch=batch, seq=seq)
        if l == 0:
            v_first = v_l
        attn = [dilated_attention_group(zb, tab, gi, batch=batch, seq=seq) for gi in range(B_GROUPS)]
        yc = gated_deltanet(zc, lp["gdn_conv"], lp["gdn_prm"], batch=batch, seq=seq)
        merged = merge_mixers(ya, attn, yc, zg, lp["pa"], lp["pb"], lp["pc"])
        xf = matmul(merged, lp["w_out"], tn=512, residual=xf, name="out_proj")
        h2 = rmsnorm(xf, p["ffn_norm"][l], BF16)
        u = matmul(h2, lp["ffn_up"], tn=512, name="ffn_up")
        xf = ffn_tail(u, lp["ffn_conv"], lp["ffn_down"], xf, p["final_norm"], seq=seq,
                      final_norm=(l == DEPTH - 1))
    return xf.reshape(batch, seq, d)


def kernel(x, attn_norm, w_in, rwkv_mu, rwkv_w0, rwkv_w2, rwkv_a0, rwkv_a2, rwkv_g2, rwkv_k_k, rwkv_k_a, rwkv_r_k, rwkv_ln_w, rwkv_ln_b, rwkv_v0, rwkv_v1, rwkv_v2, gdn_conv, gdn_A_log, gdn_dt_bias, gdn_norm, proj_a, proj_b, proj_c, w_out, ffn_norm, ffn_up, ffn_conv, ffn_down, final_norm):
    params = dict(
        attn_norm=attn_norm, w_in=w_in, rwkv_mu=rwkv_mu, rwkv_w0=rwkv_w0, rwkv_w2=rwkv_w2, rwkv_a0=rwkv_a0,
        rwkv_a2=rwkv_a2, rwkv_g2=rwkv_g2, rwkv_k_k=rwkv_k_k, rwkv_k_a=rwkv_k_a, rwkv_r_k=rwkv_r_k,
        rwkv_ln_w=rwkv_ln_w, rwkv_ln_b=rwkv_ln_b, rwkv_v0=rwkv_v0, rwkv_v1=rwkv_v1, rwkv_v2=rwkv_v2,
        gdn_conv=gdn_conv, gdn_A_log=gdn_A_log, gdn_dt_bias=gdn_dt_bias, gdn_norm=gdn_norm, proj_a=proj_a,
        proj_b=proj_b, proj_c=proj_c, w_out=w_out, ffn_norm=ffn_norm, ffn_up=ffn_up, ffn_conv=ffn_conv,
        ffn_down=ffn_down, final_norm=final_norm)
    return _forward(x, params)
```

```python
import functools

import jax
import jax.numpy as jnp
from jax import lax
from jax.experimental import pallas as pl
from jax.experimental.pallas import tpu as pltpu

F32 = jnp.float32
BF16 = jnp.bfloat16

D_MODEL = 2048
DEPTH = 2
NORM_EPS = 1e-6

A_HEADS, A_HEAD = 16, 64
A_WIDTH = A_HEADS * A_HEAD
A_DECAY_LORA, A_ICLR_LORA, A_GATE_LORA, A_VRES_LORA = 64, 64, 160, 32
A_GN_EPS = 64e-5
A_IN = 3 * A_WIDTH + A_DECAY_LORA + A_ICLR_LORA + A_GATE_LORA

B_PAIRS = ((128, 1), (512, 4), (2048, 16))
B_GROUPS = 3
B_HEADS_PER_GROUP, B_HEAD = 4, 128
B_WIDTH = B_GROUPS * B_HEADS_PER_GROUP * B_HEAD
B_OUT = B_HEADS_PER_GROUP * B_HEAD
B_IN = 3 * B_WIDTH
ROPE_THETA = 500000.0
ROPE_DIM = B_HEAD // 4

C_HEADS, C_HEAD_K, C_HEAD_V = 8, 128, 128
C_KW = C_HEADS * C_HEAD_K
C_VW = C_HEADS * C_HEAD_V
C_CONV = 4
C_IN = 2 * C_KW + C_VW + 2 * C_HEADS + C_VW

D_FF = 5632
FFN_CONV = 3

LANES = 128
SUBLANES = 8
CHUNK = 64
VMEM_LIMIT = 56 * 1024 * 1024

ZA_GD = 3 * A_WIDTH
ZA_WA = ZA_GD + 256
ZA_LV = ZA_WA + 128
ZA_W = ZA_LV + 128
ZC_BA = 3 * C_KW + C_VW
ZC_W = ZC_BA + 128


def _cparams(sem):
    return pltpu.CompilerParams(dimension_semantics=sem, vmem_limit_bytes=VMEM_LIMIT)


def _dot(a, b):
    return jnp.dot(a.astype(BF16), b.astype(BF16), preferred_element_type=F32)


def _dot_nt(a, b):
    return lax.dot_general(a.astype(BF16), b.astype(BF16), (((1,), (1,)), ((), ())), preferred_element_type=F32)


def _dot_tn(a, b):
    return lax.dot_general(a.astype(BF16), b.astype(BF16), (((0,), (0,)), ((), ())), preferred_element_type=F32)


def _split(a):
    hi = a.astype(BF16)
    lo = (a - hi.astype(F32)).astype(BF16)
    return hi, lo


def _dot_sel_r(a, sel):
    hi, lo = _split(a)
    return (jnp.dot(hi, sel, preferred_element_type=F32) + jnp.dot(lo, sel, preferred_element_type=F32))


def _dot_sel_l(sel, a):
    hi, lo = _split(a)
    return (jnp.dot(sel, hi, preferred_element_type=F32) + jnp.dot(sel, lo, preferred_element_type=F32))


def _sigmoid(x):
    return 1.0 / (1.0 + jnp.exp(-x))


def _softplus(x):
    return jnp.maximum(x, 0.0) + jnp.log(1.0 + jnp.exp(-jnp.abs(x)))


def _iota2(shape, axis):
    return lax.broadcasted_iota(jnp.int32, shape, axis)


def _chunk_of(idx):
    return jnp.right_shift(idx, CHUNK.bit_length() - 1)


def _chunk_tri(n):
    ri, ci = _iota2((n, n), 0), _iota2((n, n), 1)
    return jnp.where((_chunk_of(ri) == _chunk_of(ci)) & (ri >= ci), 1.0, 0.0).astype(BF16)


def _neumann_inverse_many(n_mats, eye):
    ps = [eye + n for n in n_mats]
    qs = list(n_mats)
    for _ in range(CHUNK.bit_length() - 2):
        qs = [_dot(q, q) for q in qs]
        ps = [p + _dot(p, q) for p, q in zip(ps, qs)]
    return ps


def _neumann_inverse(n_mat, eye):
    return _neumann_inverse_many([n_mat], eye)[0]


def _rmsnorm_kernel(x_ref, g_ref, o_ref):
    x = x_ref[...]
    ms = jnp.mean(x * x, axis=-1, keepdims=True)
    o_ref[...] = ((x * lax.rsqrt(ms + NORM_EPS)) * g_ref[...]).astype(o_ref.dtype)


def rmsnorm(x, g, out_dtype, tm=512):
    m, d = x.shape
    return pl.pallas_call(
        _rmsnorm_kernel,
        grid=(m // tm,),
        in_specs=[pl.BlockSpec((tm, d), lambda i: (i, 0)), pl.BlockSpec((1, d), lambda i: (0, 0))],
        out_specs=pl.BlockSpec((tm, d), lambda i: (i, 0)),
        out_shape=jax.ShapeDtypeStruct((m, d), out_dtype),
        compiler_params=_cparams(("parallel",)),
        name="rmsnorm",
    )(x, g.reshape(1, d))


def _matmul_kernel(x_ref, w_ref, o_ref):
    o_ref[...] = jnp.dot(x_ref[...], w_ref[...], preferred_element_type=F32).astype(o_ref.dtype)


def _matmul_res_kernel(x_ref, w_ref, r_ref, o_ref):
    o_ref[...] = r_ref[...] + jnp.dot(x_ref[...], w_ref[...], preferred_element_type=F32)


def matmul(x, w, tn, tm=1024, residual=None, name="matmul"):
    m, k = x.shape
    n = w.shape[1]
    tm = min(tm, m)
    in_specs = [pl.BlockSpec((tm, k), lambda i, j: (i, 0)), pl.BlockSpec((k, tn), lambda i, j: (0, j))]
    args = [x, w]
    kern = _matmul_kernel
    if residual is not None:
        in_specs.append(pl.BlockSpec((tm, tn), lambda i, j: (i, j)))
        args.append(residual)
        kern = _matmul_res_kernel
    return pl.pallas_call(
        kern,
        grid=(m // tm, n // tn),
        in_specs=in_specs,
        out_specs=pl.BlockSpec((tm, tn), lambda i, j: (i, j)),
        out_shape=jax.ShapeDtypeStruct((m, n), F32),
        compiler_params=_cparams(("parallel", "parallel")),
        name=name,
    )(*args)


def _rwkv_kernel(r_ref, k_ref, v_ref, gd_ref, wa_ref, lv_ref, vf_ref, p_ref, mu2_ref, w2_ref, a2_ref, g2_ref,
                 v2_ref, y_ref, vout_ref, s_ref, br_ref, bk_ref, bv_ref, bgd_ref, bwa_ref, *, tb, pp, has_vmix):
    i = pl.program_id(2)
    h = SUBLANES

    @pl.when(i == 0)
    def _():
        s_ref[...] = jnp.zeros_like(s_ref)
        for b in (br_ref, bk_ref, bv_ref, bgd_ref, bwa_ref):
            b[0:h, :] = jnp.zeros((h, b.shape[1]), F32)

    prm = p_ref[...]
    w0, a0, k_k, k_a, r_k, ln_w, ln_b, v0 = [prm[j:j + 1] for j in range(8)]
    mu_r, mu_k, mu_v = prm[8:9], prm[9:10], prm[10:11]
    mu2 = mu2_ref[...]
    mu_gd, mu_wa = mu2[:, 0:256], mu2[:, 256:384]

    def shifted_mix(x_ref, buf_ref, mu):
        x = x_ref[...]
        buf_ref[h:h + tb, :] = x
        xs = buf_ref[h - 1:h - 1 + tb, :]
        buf_ref[0:h, :] = x[tb - h:tb, :]
        return x + (xs - x) * mu

    r = shifted_mix(r_ref, br_ref, mu_r)
    k = shifted_mix(k_ref, bk_ref, mu_k)
    v = shifted_mix(v_ref, bv_ref, mu_v)
    gd = shifted_mix(gd_ref, bgd_ref, mu_gd)
    wa = shifted_mix(wa_ref, bwa_ref, mu_wa)

    lane = _iota2((1, LANES), 1)
    m0 = jnp.where(lane < A_HEAD, 1.0, 0.0)
    m1 = 1.0 - m0
    ri = _iota2((LANES, LANES), 0)
    ci = _iota2((LANES, LANES), 1)
    same = _chunk_of(ri) == _chunk_of(ci)
    strict = same & (ri > ci)
    incl = same & (ri >= ci)
    eye = jnp.where(ri == ci, 1.0, 0.0)
    ones_bd = jnp.where(same, 1.0, 0.0).astype(BF16)
    tri = _chunk_tri(tb)
    pair_lanes = [slice(q * LANES, (q + 1) * LANES) for q in range(pp)]

    def head_sum(x):
        return jnp.concatenate([_dot_sel_r(x[:, ls], ones_bd) for ls in pair_lanes], axis=1)

    lw = -jnp.exp(-0.5) * _sigmoid(w0 + _dot(jnp.tanh(wa), w2_ref[...]))
    a = _sigmoid(a0 + _dot(wa, a2_ref[...]))
    g = _dot(_sigmoid(gd), g2_ref[...])
    kk = k * k_k
    kk = kk / jnp.maximum(jnp.sqrt(head_sum(kk * kk)), 1e-12)
    k = k * (1.0 + (a - 1.0) * k_a)
    if has_vmix:
        v_mix = _sigmoid(v0 + _dot(lv_ref[...], v2_ref[...]))
        v = v + (vf_ref[...] - v) * v_mix
    vout_ref[...] = v

    cum = _dot_sel_l(tri, lw)
    e_pos = jnp.exp(cum)
    e_neg = jnp.exp(-cum)
    r_t = r * e_pos
    a_t = -kk * jnp.exp(cum - lw)
    b_t = (kk * a) * e_neg
    k_t = k * e_neg

    def stack_masked(x):
        return jnp.concatenate([x * m0, x * m1], axis=0)

    def stack_dup(x):
        return jnp.concatenate([x, x], axis=0)

    kb = kk * a
    nchunk = tb // CHUNK
    inst = [(c, q) for c in range(nchunk) for q in range(pp)]
    rows_of = lambda c: slice(c * CHUNK, (c + 1) * CHUNK)
    last_of = lambda c: slice((c + 1) * CHUNK - 1, (c + 1) * CHUNK)
    a_s = {cq: stack_masked(a_t[rows_of(cq[0]), pair_lanes[cq[1]]]) for cq in inst}
    r_s = {cq: stack_masked(r_t[rows_of(cq[0]), pair_lanes[cq[1]]]) for cq in inst}
    v_s = {cq: stack_masked(v[rows_of(cq[0]), pair_lanes[cq[1]]]) for cq in inst}
    gm = {}
    for c, q in inst:
        sl, ls = rows_of(c), pair_lanes[q]
        right = jnp.concatenate([stack_dup(b_t[sl, ls]), stack_dup(k_t[sl, ls])], axis=0)
        gm[c, q] = _dot_nt(jnp.concatenate([a_s[c, q], r_s[c, q]], axis=0), right)
    n_ab = [jnp.where(strict, gm[cq][0:LANES, 0:LANES], 0.0) for cq in inst]
    t_inv = dict(zip(inst, _neumann_inverse_many(n_ab, eye)))
    gv = {cq: _dot(jnp.where(strict, gm[cq][0:LANES, LANES:], 0.0), v_s[cq]) for cq in inst}
    tu = {cq: _dot(t_inv[cq], jnp.concatenate([a_s[cq], gv[cq]], axis=1)) for cq in inst}
    g_r = {cq: jnp.where(jnp.concatenate([incl, incl], axis=1), gm[cq][LANES:], 0.0) for cq in inst}

    states = [s_ref[q] for q in range(pp)]
    for c in range(nchunk):
        sl = rows_of(c)
        x0 = [_dot_nt(jnp.concatenate([tu[c, q][:, 0:LANES], r_s[c, q]], axis=0), states[q]) for q in range(pp)]
        uv = [jnp.concatenate([x0[q][0:LANES] + tu[c, q][:, LANES:], v_s[c, q]], axis=0) for q in range(pp)]
        for q, ls in enumerate(pair_lanes):
            cum_last = cum[last_of(c), ls]
            e_end = jnp.exp(cum_last - cum[sl, ls])
            bk_end = jnp.concatenate([stack_dup(kb[sl, ls] * e_end), stack_dup(k[sl, ls] * e_end)], axis=0)
            states[q] = jnp.where(same, states[q] * jnp.exp(cum_last) + _dot_tn(uv[q], bk_end), 0.0)
        for q, ls in enumerate(pair_lanes):
            y_s = x0[q][LANES:] + _dot(g_r[c, q], uv[q])
            y_ref[sl, ls] = y_s[0:CHUNK] + y_s[CHUNK:]
    for q in range(pp):
        s_ref[q] = states[q]

    y = y_ref[...]
    inv_n = 1.0 / A_HEAD
    mean = head_sum(y) * inv_n
    d = y - mean
    var = head_sum(d * d) * inv_n
    yn = d * lax.rsqrt(var + A_GN_EPS) * ln_w + ln_b
    bonus = head_sum(r * k * r_k) * v
    y_ref[...] = (yn + bonus) * g


def rwkv_mix(za, v_first, prm, mu2, w2p, a2p, g2p, v2p, *, batch, seq, tb=256, pp=4):
    has_vmix = v_first is not None
    nt = seq // tb
    w = pp * LANES
    npair = A_WIDTH // w
    row = lambda b, p, i: b * nt + i
    if v_first is None:
        v_first = za
    in_specs = [
        pl.BlockSpec((tb, w), lambda b, p, i: (row(b, p, i), p)),
        pl.BlockSpec((tb, w), lambda b, p, i: (row(b, p, i), npair + p)),
        pl.BlockSpec((tb, w), lambda b, p, i: (row(b, p, i), 2 * npair + p)),
        pl.BlockSpec((tb, 256), lambda b, p, i: (row(b, p, i), ZA_GD // 256)),
        pl.BlockSpec((tb, LANES), lambda b, p, i: (row(b, p, i), ZA_WA // LANES)),
        pl.BlockSpec((tb, LANES), lambda b, p, i: (row(b, p, i), ZA_LV // LANES)),
        pl.BlockSpec((tb, w), lambda b, p, i: (row(b, p, i), p)),
        pl.BlockSpec((16, w), lambda b, p, i: (0, p)),
        pl.BlockSpec((1, 384), lambda b, p, i: (0, 0)),
        pl.BlockSpec((LANES, w), lambda b, p, i: (0, p)),
        pl.BlockSpec((LANES, w), lambda b, p, i: (0, p)),
        pl.BlockSpec((256, w), lambda b, p, i: (0, p)),
        pl.BlockSpec((LANES, w), lambda b, p, i: (0, p)),
    ]
    out_spec = pl.BlockSpec((tb, w), lambda b, p, i: (row(b, p, i), p))
    out_sds = jax.ShapeDtypeStruct((batch * seq, A_WIDTH), F32)
    return pl.pallas_call(
        functools.partial(_rwkv_kernel, tb=tb, pp=pp, has_vmix=has_vmix),
        grid=(batch, npair, nt),
        in_specs=in_specs,
        out_specs=[out_spec, out_spec],
        out_shape=[out_sds, out_sds],
        scratch_shapes=[
            pltpu.VMEM((pp, LANES, LANES), F32),
            pltpu.VMEM((tb + SUBLANES, w), F32),
            pltpu.VMEM((tb + SUBLANES, w), F32),
            pltpu.VMEM((tb + SUBLANES, w), F32),
            pltpu.VMEM((tb + SUBLANES, 256), F32),
            pltpu.VMEM((tb + SUBLANES, LANES), F32),
        ],
        compiler_params=_cparams(("parallel", "parallel", "arbitrary")),
        name="rwkv7",
    )(za, za, za, za, za, za, v_first, prm, mu2, w2p, a2p, g2p, v2p)


def _attn_kernel(*refs, tq, dil, nb, has_prev):
    if has_prev:
        (q_ref, kc_ref, kp_ref, vc_ref, vp_ref, cc_ref, sc_ref, nc_ref, cp_ref, sp_ref, np_ref,
         o_ref, lse_ref) = refs
    else:
        q_ref, kc_ref, vc_ref, cc_ref, sc_ref, nc_ref, o_ref, lse_ref = refs
    i = pl.program_id(1)
    scale = B_HEAD ** -0.5

    def rows(j, r):
        start = j * tq * dil + r
        return pl.ds(start, tq, stride=dil) if dil > 1 else pl.ds(start, tq)

    def rope(x, tabs):
        half = ROPE_DIM // 2
        return (x * tabs[0] + pltpu.roll(x, half, axis=1) * tabs[1]
                + pltpu.roll(x, LANES - half, axis=1) * tabs[2])

    if has_prev:
        ri, ci = _iota2((tq, 2 * tq), 0), _iota2((tq, 2 * tq), 1)
        cur_ok = (ci >= tq) & ((ci - tq) <= ri)
        no_prev = jnp.where(i > 0, 0, 2 * tq)
        valid_first = ((ci < tq) & (ci >= ri + no_prev)) | cur_ok
        valid_inner = ((ci < tq) & (ci >= ri)) | cur_ok
    else:
        causal = _iota2((tq, tq), 1) <= _iota2((tq, tq), 0)

    inst = [(r, j) for r in range(dil) for j in range(nb)]
    qs, ks, vs = {}, {}, {}
    for r in range(dil):
        if has_prev:
            rp = rows(0, r)
            ks[r, -1] = rope(kp_ref[rp, :], (cp_ref[rp, :], sp_ref[rp, :], np_ref[rp, :]))
            vs[r, -1] = vp_ref[rp, :]
        for j in range(nb):
            rw = rows(j, r)
            tabs = (cc_ref[rw, :], sc_ref[rw, :], nc_ref[rw, :])
            qs[r, j] = rope(q_ref[rw, :], tabs) * scale
            ks[r, j] = rope(kc_ref[rw, :], tabs)
            vs[r, j] = vc_ref[rw, :]
    if has_prev:
        s = {(r, j): jnp.where(valid_first if j == 0 else valid_inner,
                               _dot_nt(qs[r, j], jnp.concatenate([ks[r, j - 1], ks[r, j]], axis=0)), -1e30)
             for r, j in inst}
    else:
        s = {rj: jnp.where(causal, _dot_nt(qs[rj], ks[rj]), -1e30) for rj in inst}
    m = {rj: jnp.max(s[rj], axis=-1, keepdims=True) for rj in inst}
    p = {rj: jnp.exp(s[rj] - m[rj]) for rj in inst}
    den = {rj: jnp.sum(p[rj], axis=-1, keepdims=True) for rj in inst}
    if has_prev:
        num = {(r, j): _dot(p[r, j], jnp.concatenate([vs[r, j - 1], vs[r, j]], axis=0)) for r, j in inst}
    else:
        num = {rj: _dot(p[rj], vs[rj]) for rj in inst}
    for r, j in inst:
        o_ref[rows(j, r), :] = num[r, j] / den[r, j]
        lse_ref[rows(j, r), :] = jnp.broadcast_to(m[r, j] + jnp.log(den[r, j]), (tq, B_HEAD))


def dilated_attention_group(zb, rope_tabs, gi, *, batch, seq, tq=128, blocks_per_step=8):
    win, dil = B_PAIRS[gi]
    assert win // dil == tq
    nq = seq // (dil * tq)
    has_prev = nq > 1
    nb = min(nq, max(1, blocks_per_step // dil))
    rows_step = nb * tq * dil
    nsteps = seq // rows_step
    prev_rows = tq * dil
    nprev = seq // prev_rows
    nh = B_HEADS_PER_GROUP
    cur = lambda col: pl.BlockSpec((rows_step, B_HEAD), lambda b, i, hh: (b * nsteps + i, col * nh + hh))
    prv = lambda col: pl.BlockSpec(
        (prev_rows, B_HEAD), lambda b, i, hh: (b * nprev + jnp.maximum(i * nb - 1, 0), col * nh + hh))
    tab_cur = pl.BlockSpec((rows_step, LANES), lambda b, i, hh: (i, 0))
    tab_prv = pl.BlockSpec((prev_rows, LANES), lambda b, i, hh: (jnp.maximum(i * nb - 1, 0), 0))
    qc, kc, vc = gi, B_GROUPS + gi, 2 * B_GROUPS + gi
    if has_prev:
        in_specs = [cur(qc), cur(kc), prv(kc), cur(vc), prv(vc)] + [tab_cur] * 3 + [tab_prv] * 3
        args = (zb, zb, zb, zb, zb) + tuple(rope_tabs) * 2
    else:
        in_specs = [cur(qc), cur(kc), cur(vc)] + [tab_cur] * 3
        args = (zb, zb, zb) + tuple(rope_tabs)
    out_spec = pl.BlockSpec((rows_step, B_HEAD), lambda b, i, hh: (b * nsteps + i, hh))
    out_sds = jax.ShapeDtypeStruct((batch * seq, B_OUT), F32)
    return pl.pallas_call(
        functools.partial(_attn_kernel, tq=tq, dil=dil, nb=nb, has_prev=has_prev),
        grid=(batch, nsteps, nh),
        in_specs=in_specs,
        out_specs=[out_spec, out_spec],
        out_shape=[out_sds, out_sds],
        compiler_params=_cparams(("parallel", "parallel", "arbitrary")),
        name=f"dilated_attn_g{gi}",
    )(*args)


def rope_tables(seq):
    half = ROPE_DIM // 2
    inv = ROPE_THETA ** (-jnp.arange(half, dtype=F32) / half)
    ang = jnp.arange(seq, dtype=F32)[:, None] * inv[None, :]
    cos, sin = jnp.cos(ang), jnp.sin(ang)
    z = jnp.zeros((seq, LANES - ROPE_DIM), F32)
    zh = jnp.zeros((seq, half), F32)
    c_tab = jnp.concatenate([cos, cos, jnp.ones_like(z)], axis=1)
    s_pos = jnp.concatenate([zh, sin, z], axis=1)
    s_neg = jnp.concatenate([-sin, zh, z], axis=1)
    return c_tab, s_pos, s_neg


def _gdn_kernel(q_ref, k_ref, v_ref, gate_ref, ba_ref, cw_ref, p_ref, o_ref, s_ref, bq_ref, bk_ref, bv_ref,
                *, tb, pp):
    pidx = pl.program_id(1)
    i = pl.program_id(2)
    h = SUBLANES
    hd = C_HEAD_K

    @pl.when(i == 0)
    def _():
        s_ref[...] = jnp.zeros_like(s_ref)
        for b in (bq_ref, bk_ref, bv_ref):
            b[0:h, :] = jnp.zeros((h, b.shape[1]), F32)

    cw = cw_ref[...]
    prm = p_ref[...]

    def conv_silu(x_ref, buf_ref, w):
        x = x_ref[...]
        buf_ref[h:h + tb, :] = x
        acc = x * w[C_CONV - 1:C_CONV]
        for j in range(C_CONV - 1):
            off = h - (C_CONV - 1) + j
            acc = acc + buf_ref[off:off + tb, :] * w[j:j + 1]
        buf_ref[0:h, :] = x[tb - h:tb, :]
        return acc * _sigmoid(acc)

    q = conv_silu(q_ref, bq_ref, cw[0])
    k = conv_silu(k_ref, bk_ref, cw[1])
    v = conv_silu(v_ref, bv_ref, cw[2])

    ri = _iota2((LANES, LANES), 0)
    ci = _iota2((LANES, LANES), 1)
    same = _chunk_of(ri) == _chunk_of(ci)
    strict = same & (ri > ci)
    incl = same & (ri >= ci)
    eye = jnp.where(ri == ci, 1.0, 0.0)
    ones = jnp.ones((hd, hd), BF16)
    tri = _chunk_tri(tb)

    ba = ba_ref[...]
    nh = 2 * pp
    head_lanes = [slice(hh * hd, (hh + 1) * hd) for hh in range(nh)]
    vh = [v[:, ls] for ls in head_lanes]
    q_ss = [_dot_sel_r(q[:, ls] * q[:, ls], ones) for ls in head_lanes]
    k_ss = [_dot_sel_r(k[:, ls] * k[:, ls], ones) for ls in head_lanes]
    b_raw = [_dot_sel_r(ba, jnp.where(ri == nh * pidx + hh, 1.0, 0.0).astype(BF16)) for hh in range(nh)]
    a_raw = [_dot_sel_r(ba, jnp.where(ri == nh * pidx + hh + C_HEADS, 1.0, 0.0).astype(BF16)) for hh in range(nh)]
    qh = [q[:, ls] / jnp.maximum(jnp.sqrt(ss), 1e-12) * (hd ** -0.5) for ls, ss in zip(head_lanes, q_ss)]
    kh = [k[:, ls] / jnp.maximum(jnp.sqrt(ss), 1e-12) for ls, ss in zip(head_lanes, k_ss)]
    beta = [_sigmoid(b) for b in b_raw]
    glog = [-jnp.exp(prm[0:1, ls]) * _softplus(al + prm[1:2, ls]) for ls, al in zip(head_lanes, a_raw)]
    gam = [_dot_sel_l(tri, gl) for gl in glog]

    nchunk = tb // CHUNK
    inst = [(c, pr) for c in range(nchunk) for pr in range(pp)]
    rows_of = lambda c: slice(c * CHUNK, (c + 1) * CHUNK)

    def stack(xs, cp):
        c, pr = cp
        return jnp.concatenate([xs[2 * pr][rows_of(c)], xs[2 * pr + 1][rows_of(c)]], axis=0)

    k_s = {cp: stack(kh, cp) for cp in inst}
    q_s = {cp: stack(qh, cp) for cp in inst}
    beta_s = {cp: stack(beta, cp) for cp in inst}
    gam_s = {cp: stack(gam, cp) for cp in inst}
    kq = {cp: _dot_nt(jnp.concatenate([k_s[cp], q_s[cp]], axis=0), k_s[cp]) for cp in inst}
    dm = {cp: jnp.exp(jnp.where(incl, gam_s[cp] - gam_s[cp].T, -1e30)) for cp in inst}
    n_mats = [jnp.where(strict, -(beta_s[cp] * kq[cp][0:LANES] * dm[cp]), 0.0) for cp in inst]
    t_inv = dict(zip(inst, _neumann_inverse_many(n_mats, eye)))
    e_gam = {cp: jnp.exp(gam_s[cp]) for cp in inst}
    sol = {cp: _dot(t_inv[cp], jnp.concatenate([stack(vh, cp) * beta_s[cp],
                                                k_s[cp] * (beta_s[cp] * e_gam[cp])], axis=1)) for cp in inst}

    states = [s_ref[hh] for hh in range(nh)]
    for c in range(nchunk):
        sl = rows_of(c)
        ws = []
        for hh in range(nh):
            cp, hs = (c, hh // 2), slice((hh % 2) * CHUNK, (hh % 2 + 1) * CHUNK)
            qg = (q_s[cp] * e_gam[cp])[hs]
            ws.append(_dot(jnp.concatenate([sol[cp][hs, hd:], qg], axis=0), states[hh]))
        v_new = [sol[c, hh // 2][(hh % 2) * CHUNK:(hh % 2 + 1) * CHUNK, 0:hd] - ws[hh][0:CHUNK]
                 for hh in range(nh)]
        for hh in range(nh):
            g_h = gam[hh][sl]
            g_last = g_h[CHUNK - 1:CHUNK, :]
            states[hh] = states[hh] * jnp.exp(g_last) + _dot_tn(kh[hh][sl] * jnp.exp(g_last - g_h), v_new[hh])
        for pr in range(pp):
            h0 = 2 * pr
            attn = kq[c, pr][LANES:] * dm[c, pr]
            o_s = (jnp.concatenate([ws[h0][CHUNK:], ws[h0 + 1][CHUNK:]], axis=0)
                   + _dot(attn, jnp.concatenate([v_new[h0], v_new[h0 + 1]], axis=0)))
            o_ref[sl, h0 * hd:(h0 + 1) * hd] = o_s[0:CHUNK]
            o_ref[sl, (h0 + 1) * hd:(h0 + 2) * hd] = o_s[CHUNK:]
    for hh in range(nh):
        s_ref[hh] = states[hh]

    gate = gate_ref[...]
    o_h = [o_ref[:, ls] for ls in head_lanes]
    o_ms = [_dot_sel_r(o * o, ones) * (1.0 / hd) for o in o_h]
    for o, ms, ls in zip(o_h, o_ms, head_lanes):
        gt = gate[:, ls]
        o_ref[:, ls] = (o * lax.rsqrt(ms + NORM_EPS) * prm[2:3, ls]) * (gt * _sigmoid(gt))


def gated_deltanet(zc, conv_w, prm, *, batch, seq, tb=256, pp=2):
    nt = seq // tb
    npair = C_HEADS // (2 * pp)
    wblk = 2 * pp * C_HEAD_K
    row = lambda b, p, i: b * nt + i
    in_specs = [
        pl.BlockSpec((tb, wblk), lambda b, p, i: (row(b, p, i), p)),
        pl.BlockSpec((tb, wblk), lambda b, p, i: (row(b, p, i), npair + p)),
        pl.BlockSpec((tb, wblk), lambda b, p, i: (row(b, p, i), 2 * npair + p)),
        pl.BlockSpec((tb, wblk), lambda b, p, i: (row(b, p, i), 3 * npair + p)),
        pl.BlockSpec((tb, LANES), lambda b, p, i: (row(b, p, i), ZC_BA // LANES)),
        pl.BlockSpec((3, C_CONV, wblk), lambda b, p, i: (0, 0, p)),
        pl.BlockSpec((SUBLANES, wblk), lambda b, p, i: (0, p)),
    ]
    return pl.pallas_call(
        functools.partial(_gdn_kernel, tb=tb, pp=pp),
        grid=(batch, npair, nt),
        in_specs=in_specs,
        out_specs=pl.BlockSpec((tb, wblk), lambda b, p, i: (row(b, p, i), p)),
        out_shape=jax.ShapeDtypeStruct((batch * seq, C_VW), F32),
        scratch_shapes=[
            pltpu.VMEM((2 * pp, C_HEAD_K, C_HEAD_V), F32),
            pltpu.VMEM((tb + SUBLANES, wblk), F32),
            pltpu.VMEM((tb + SUBLANES, wblk), F32),
            pltpu.VMEM((tb + SUBLANES, wblk), F32),
        ],
        compiler_params=_cparams(("parallel", "parallel", "arbitrary")),
        name="gated_deltanet",
    )(zc, zc, zc, zc, zc, conv_w, prm)


def _merge_kernel(ya_ref, o0_ref, o1_ref, o2_ref, l0_ref, l1_ref, l2_ref, yc_ref, ga_ref, gb_ref, gc_ref,
                  pa_ref, pb_ref, pc_ref, out_ref):
    l0, l1, l2 = l0_ref[...], l1_ref[...], l2_ref[...]
    m = jnp.maximum(jnp.maximum(l0, l1), l2)
    w0, w1, w2 = jnp.exp(l0 - m), jnp.exp(l1 - m), jnp.exp(l2 - m)
    yb = (w0 * o0_ref[...] + w1 * o1_ref[...] + w2 * o2_ref[...]) / (w0 + w1 + w2)
    merged = (_sigmoid(ga_ref[...]) * _dot(ya_ref[...], pa_ref[...])
              + _sigmoid(gb_ref[...]) * _dot(yb, pb_ref[...])
              + _sigmoid(gc_ref[...]) * _dot(yc_ref[...], pc_ref[...]))
    out_ref[...] = merged.astype(out_ref.dtype)


def merge_mixers(ya, attn, yc, zg, pa, pb, pc, tm=256):
    m = ya.shape[0]
    d = D_MODEL
    rows = lambda w: pl.BlockSpec((tm, w), lambda i: (i, 0))
    const = lambda a: pl.BlockSpec(a.shape, lambda i: (0, 0), pipeline_mode=pl.Buffered(1))
    (o0, l0), (o1, l1), (o2, l2) = attn
    in_specs = ([rows(A_WIDTH)] + [rows(B_OUT)] * 6 + [rows(C_VW)]
                + [pl.BlockSpec((tm, d), lambda i, j=j: (i, j)) for j in range(3)]
                + [const(pa), const(pb), const(pc)])
    return pl.pallas_call(
        _merge_kernel,
        grid=(m // tm,),
        in_specs=in_specs,
        out_specs=pl.BlockSpec((tm, d), lambda i: (i, 0)),
        out_shape=jax.ShapeDtypeStruct((m, d), BF16),
        compiler_params=_cparams(("parallel",)),
        name="merge_mixers",
    )(ya, o0, o1, o2, l0, l1, l2, yc, zg, zg, zg, pa, pb, pc)


def _ffn_tail_kernel(ug_ref, uv_ref, hg_ref, hv_ref, cw_ref, wd_ref, x_ref, fn_ref, o_ref, acc_ref, bg_ref, bv_ref,
                     *, tm, rows_per_seq, final_norm):
    i = pl.program_id(0)
    j = pl.program_id(1)
    h = SUBLANES
    first = (i % rows_per_seq) == 0

    @pl.when(j == 0)
    def _():
        acc_ref[...] = jnp.zeros_like(acc_ref)

    cw = cw_ref[...]

    def conv(u_ref, halo_ref, buf_ref, w):
        u = u_ref[...]
        buf_ref[0:h, :] = jnp.where(first, 0.0, halo_ref[...])
        buf_ref[h:h + tm, :] = u
        acc = u * w[FFN_CONV - 1:FFN_CONV]
        for t in range(FFN_CONV - 1):
            off = h - (FFN_CONV - 1) + t
            acc = acc + buf_ref[off:off + tm, :] * w[t:t + 1]
        return acc

    cg = conv(ug_ref, hg_ref, bg_ref, cw[0])
    cv = conv(uv_ref, hv_ref, bv_ref, cw[1])
    act = (cg * _sigmoid(cg)) * cv
    acc_ref[...] += _dot(act, wd_ref[...])

    @pl.when(j == pl.num_programs(1) - 1)
    def _():
        y = x_ref[...] + acc_ref[...]
        if final_norm:
            ms = jnp.mean(y * y, axis=-1, keepdims=True)
            y = (y * lax.rsqrt(ms + NORM_EPS)) * fn_ref[...]
        o_ref[...] = y


def ffn_tail(u, conv_w, w_down, x, final_g, *, seq, final_norm, tm=512, tf=512):
    m = u.shape[0]
    d = D_MODEL
    nf = D_FF // tf
    hb = tm // SUBLANES
    halo = lambda i: jnp.maximum(i * hb - 1, 0)
    in_specs = [
        pl.BlockSpec((tm, tf), lambda i, j: (i, j)),
        pl.BlockSpec((tm, tf), lambda i, j: (i, nf + j)),
        pl.BlockSpec((SUBLANES, tf), lambda i, j: (halo(i), j)),
        pl.BlockSpec((SUBLANES, tf), lambda i, j: (halo(i), nf + j)),
        pl.BlockSpec((2, FFN_CONV, tf), lambda i, j: (0, 0, j)),
        pl.BlockSpec((tf, d), lambda i, j: (j, 0)),
        pl.BlockSpec((tm, d), lambda i, j: (i, 0)),
        pl.BlockSpec((1, d), lambda i, j: (0, 0)),
    ]
    return pl.pallas_call(
        functools.partial(_ffn_tail_kernel, tm=tm, rows_per_seq=seq // tm, final_norm=final_norm),
        grid=(m // tm, nf),
        in_specs=in_specs,
        out_specs=pl.BlockSpec((tm, d), lambda i, j: (i, 0)),
        out_shape=jax.ShapeDtypeStruct((m, d), F32),
        scratch_shapes=[
            pltpu.VMEM((tm, d), F32),
            pltpu.VMEM((tm + SUBLANES, tf), F32),
            pltpu.VMEM((tm + SUBLANES, tf), F32),
        ],
        compiler_params=_cparams(("parallel", "arbitrary")),
        name="ffn_tail",
    )(u, u, u, u, conv_w, w_down, x, final_g.reshape(1, d))


def _pad_cols(w, width):
    return jnp.pad(w, ((0, 0), (0, width - w.shape[1])))


def _pad_rows(w, rows, at=0):
    return jnp.pad(w, ((at, rows - at - w.shape[0]), (0, 0)))


def _layer_params(l, p):
    w_in = p["w_in"][l]
    a_end, b_end, c_end = A_IN, A_IN + B_IN, A_IN + B_IN + C_IN
    aw = 3 * A_WIDTH
    wa_cols = w_in[:, :a_end]
    lora_v = (p["rwkv_v1"][l - 1] if l > 0 else jnp.zeros((D_MODEL, A_VRES_LORA), F32))
    w_za = jnp.concatenate([
        wa_cols[:, :aw],
        _pad_cols(wa_cols[:, aw + A_DECAY_LORA + A_ICLR_LORA:], 256),
        wa_cols[:, aw:aw + A_DECAY_LORA + A_ICLR_LORA],
        _pad_cols(lora_v, 128)], axis=1)
    wc_cols = w_in[:, b_end:c_end]
    qkv_w = 2 * C_KW + C_VW
    w_zc = jnp.concatenate([
        wc_cols[:, :qkv_w],
        wc_cols[:, qkv_w + 2 * C_HEADS:],
        _pad_cols(wc_cols[:, qkv_w:qkv_w + 2 * C_HEADS], 128)], axis=1)
    mu = p["rwkv_mu"][l]
    rows = [p["rwkv_w0"][l], p["rwkv_a0"][l], p["rwkv_k_k"][l], p["rwkv_k_a"][l], p["rwkv_r_k"][l].reshape(-1),
            p["rwkv_ln_w"][l], p["rwkv_ln_b"][l],
            (p["rwkv_v0"][l - 1] if l > 0 else jnp.zeros((A_WIDTH,), F32)),
            mu[:A_WIDTH], mu[A_WIDTH:2 * A_WIDTH], mu[2 * A_WIDTH:aw]]
    rwkv_prm = jnp.pad(jnp.stack(rows), ((0, 16 - len(rows)), (0, 0)))
    mu2 = jnp.concatenate([_pad_cols(mu[None, aw + 128:], 256), mu[None, aw:aw + 128]], axis=1)
    v2 = (p["rwkv_v2"][l - 1] if l > 0 else jnp.zeros((A_VRES_LORA, A_WIDTH), F32))
    bcast = lambda t: jnp.repeat(t, C_HEAD_K)
    gdn_prm = jnp.pad(jnp.stack([bcast(p["gdn_A_log"][l]), bcast(p["gdn_dt_bias"][l]),
                                 jnp.tile(p["gdn_norm"][l], C_HEADS)]), ((0, SUBLANES - 3), (0, 0)))
    fc = p["ffn_conv"][l]
    return dict(
        w_za=w_za.astype(BF16), w_zb=w_in[:, a_end:b_end].astype(BF16), w_zc=w_zc.astype(BF16),
        w_zg=w_in[:, c_end:].astype(BF16),
        rwkv_prm=rwkv_prm, mu2=mu2,
        w2p=_pad_rows(p["rwkv_w2"][l], 128, 0).astype(BF16),
        a2p=_pad_rows(p["rwkv_a2"][l], 128, A_DECAY_LORA).astype(BF16),
        g2p=_pad_rows(p["rwkv_g2"][l], 256, 0).astype(BF16),
        v2p=_pad_rows(v2, 128, 0).astype(BF16),
        gdn_conv=p["gdn_conv"][l].reshape(C_CONV, 3, C_KW).transpose(1, 0, 2),
        gdn_prm=gdn_prm,
        pa=p["proj_a"][l].astype(BF16), pb=p["proj_b"][l].astype(BF16), pc=p["proj_c"][l].astype(BF16),
        w_out=p["w_out"][l].astype(BF16),
        ffn_up=p["ffn_up"][l].astype(BF16),
        ffn_conv=fc.reshape(FFN_CONV, 2, D_FF).transpose(1, 0, 2),
        ffn_down=p["ffn_down"][l].astype(BF16),
    )


def _forward(x, p):
    batch, seq, d = x.shape
    m = batch * seq
    xf = x.reshape(m, d)
    tab = rope_tables(seq)
    v_first = None
    for l in range(DEPTH):
        lp = _layer_params(l, p)
        h = rmsnorm(xf, p["attn_norm"][l], BF16)
        za = matmul(h, lp["w_za"], tn=512, name="in_proj_a")
        zb = matmul(h, lp["w_zb"], tn=512, name="in_proj_b")
        zc = matmul(h, lp["w_zc"], tn=1408, name="in_proj_c")
        zg = matmul(h, lp["w_zg"], tn=512, name="in_proj_g")
        ya, v_l = rwkv_mix(za, v_first, lp["rwkv_prm"], lp["mu2"], lp["w2p"], lp["a2p"], lp["g2p"], lp["v2p"],
                           batch=batch, seq=seq)
        if l == 0:
            v_first = v_l
        attn = [dilated_attention_group(zb, tab, gi, batch=batch, seq=seq) for gi in range(B_GROUPS)]
        yc = gated_deltanet(zc, lp["gdn_conv"], lp["gdn_prm"], batch=batch, seq=seq)
        merged = merge_mixers(ya, attn, yc, zg, lp["pa"], lp["pb"], lp["pc"])
        xf = matmul(merged, lp["w_out"], tn=512, residual=xf, name="out_proj")
        h2 = rmsnorm(xf, p["ffn_norm"][l], BF16)
        u = matmul(h2, lp["ffn_up"], tn=512, name="ffn_up")
        xf = ffn_tail(u, lp["ffn_conv"], lp["ffn_down"], xf, p["final_norm"], seq=seq,
                      final_norm=(l == DEPTH - 1))
    return xf.reshape(batch, seq, d)


def kernel(x, attn_norm, w_in, rwkv_mu, rwkv_w0, rwkv_w2, rwkv_a0, rwkv_a2, rwkv_g2, rwkv_k_k, rwkv_k_a, rwkv_r_k, rwkv_ln_w, rwkv_ln_b, rwkv_v0, rwkv_v1, rwkv_v2, gdn_conv, gdn_A_log, gdn_dt_bias, gdn_norm, proj_a, proj_b, proj_c, w_out, ffn_norm, ffn_up, ffn_conv, ffn_down, final_norm):
    params = dict(
        attn_norm=attn_norm, w_in=w_in, rwkv_mu=rwkv_mu, rwkv_w0=rwkv_w0, rwkv_w2=rwkv_w2, rwkv_a0=rwkv_a0,
        rwkv_a2=rwkv_a2, rwkv_g2=rwkv_g2, rwkv_k_k=rwkv_k_k, rwkv_k_a=rwkv_k_a, rwkv_r_k=rwkv_r_k,
        rwkv_ln_w=rwkv_ln_w, rwkv_ln_b=rwkv_ln_b, rwkv_v0=rwkv_v0, rwkv_v1=rwkv_v1, rwkv_v2=rwkv_v2,
        gdn_conv=gdn_conv, gdn_A_log=gdn_A_log, gdn_dt_bias=gdn_dt_bias, gdn_norm=gdn_norm, proj_a=proj_a,
        proj_b=proj_b, proj_c=proj_c, w_out=w_out, ffn_norm=ffn_norm, ffn_up=ffn_up, ffn_conv=ffn_conv,
        ffn_down=ffn_down, final_norm=final_norm)
    return _forward(x, params)
```

```python
import functools

import jax
import jax.numpy as jnp
from jax import lax
from jax.experimental import pallas as pl
from jax.experimental.pallas import tpu as pltpu

F32 = jnp.float32
BF16 = jnp.bfloat16

D_MODEL = 2048
DEPTH = 2
NORM_EPS = 1e-6

A_HEADS, A_HEAD = 16, 64
A_WIDTH = A_HEADS * A_HEAD
A_DECAY_LORA, A_ICLR_LORA, A_GATE_LORA, A_VRES_LORA = 64, 64, 160, 32
A_GN_EPS = 64e-5
A_IN = 3 * A_WIDTH + A_DECAY_LORA + A_ICLR_LORA + A_GATE_LORA

B_PAIRS = ((128, 1), (512, 4), (2048, 16))
B_GROUPS = 3
B_HEADS_PER_GROUP, B_HEAD = 4, 128
B_WIDTH = B_GROUPS * B_HEADS_PER_GROUP * B_HEAD
B_OUT = B_HEADS_PER_GROUP * B_HEAD
B_IN = 3 * B_WIDTH
ROPE_THETA = 500000.0
ROPE_DIM = B_HEAD // 4

C_HEADS, C_HEAD_K, C_HEAD_V = 8, 128, 128
C_KW = C_HEADS * C_HEAD_K
C_VW = C_HEADS * C_HEAD_V
C_CONV = 4
C_IN = 2 * C_KW + C_VW + 2 * C_HEADS + C_VW

D_FF = 5632
FFN_CONV = 3

LANES = 128
SUBLANES = 8
CHUNK = 64
FFN_TF = 512
VMEM_LIMIT = 56 * 1024 * 1024

ZA_GD = 3 * A_WIDTH
ZA_WA = ZA_GD + 256
ZA_LV = ZA_WA + 128
ZA_W = ZA_LV + 128
ZC_BA = 3 * C_KW + C_VW
ZC_W = ZC_BA + 128


def _cparams(sem):
    return pltpu.CompilerParams(dimension_semantics=sem, vmem_limit_bytes=VMEM_LIMIT)


def _dot(a, b):
    return jnp.dot(a.astype(BF16), b.astype(BF16), preferred_element_type=F32)


def _dot_nt(a, b):
    return lax.dot_general(a.astype(BF16), b.astype(BF16), (((1,), (1,)), ((), ())), preferred_element_type=F32)


def _dot_tn(a, b):
    return lax.dot_general(a.astype(BF16), b.astype(BF16), (((0,), (0,)), ((), ())), preferred_element_type=F32)


def _split(a):
    hi = a.astype(BF16)
    lo = (a - hi.astype(F32)).astype(BF16)
    return hi, lo


def _dot_sel_r(a, sel):
    hi, lo = _split(a)
    return (jnp.dot(hi, sel, preferred_element_type=F32) + jnp.dot(lo, sel, preferred_element_type=F32))


def _dot_sel_l(sel, a):
    hi, lo = _split(a)
    return (jnp.dot(sel, hi, preferred_element_type=F32) + jnp.dot(sel, lo, preferred_element_type=F32))


def _sigmoid(x):
    return 1.0 / (1.0 + jnp.exp(-x))


def _softplus(x):
    return jnp.maximum(x, 0.0) + jnp.log(1.0 + jnp.exp(-jnp.abs(x)))


def _iota2(shape, axis):
    return lax.broadcasted_iota(jnp.int32, shape, axis)


def _chunk_of(idx):
    return jnp.right_shift(idx, CHUNK.bit_length() - 1)


def _chunk_tri(n):
    ri, ci = _iota2((n, n), 0), _iota2((n, n), 1)
    return jnp.where((_chunk_of(ri) == _chunk_of(ci)) & (ri >= ci), 1.0, 0.0).astype(BF16)


def _neumann_inverse_many(n_mats, eye):
    ps = [eye + n for n in n_mats]
    qs = [_dot(n, n) for n in n_mats]
    levels = CHUNK.bit_length() - 2
    for lvl in range(levels):
        if lvl == levels - 1:
            ps = [p + _dot(p, q) for p, q in zip(ps, qs)]
        else:
            prods = [_dot(jnp.concatenate([q, p], axis=0), q) for p, q in zip(ps, qs)]
            ps = [p + pr[LANES:] for p, pr in zip(ps, prods)]
            qs = [pr[0:LANES] for pr in prods]
    return ps


def _neumann_inverse(n_mat, eye):
    return _neumann_inverse_many([n_mat], eye)[0]


def _rmsnorm_kernel(x_ref, g_ref, o_ref):
    x = x_ref[...]
    ms = jnp.mean(x * x, axis=-1, keepdims=True)
    o_ref[...] = ((x * lax.rsqrt(ms + NORM_EPS)) * g_ref[...]).astype(o_ref.dtype)


def rmsnorm(x, g, out_dtype, tm=512):
    m, d = x.shape
    return pl.pallas_call(
        _rmsnorm_kernel,
        grid=(m // tm,),
        in_specs=[pl.BlockSpec((tm, d), lambda i: (i, 0)), pl.BlockSpec((1, d), lambda i: (0, 0))],
        out_specs=pl.BlockSpec((tm, d), lambda i: (i, 0)),
        out_shape=jax.ShapeDtypeStruct((m, d), out_dtype),
        compiler_params=_cparams(("parallel",)),
        name="rmsnorm",
    )(x, g.reshape(1, d))


def _matmul_kernel(x_ref, w_ref, o_ref):
    o_ref[...] = jnp.dot(x_ref[...], w_ref[...], preferred_element_type=F32).astype(o_ref.dtype)


def _matmul_res_kernel(x_ref, w_ref, r_ref, o_ref):
    o_ref[...] = r_ref[...] + jnp.dot(x_ref[...], w_ref[...], preferred_element_type=F32)


def matmul(x, w, tn, tm=1024, residual=None, name="matmul"):
    m, k = x.shape
    n = w.shape[1]
    tm = min(tm, m)
    in_specs = [pl.BlockSpec((tm, k), lambda i, j: (i, 0)), pl.BlockSpec((k, tn), lambda i, j: (0, j))]
    args = [x, w]
    kern = _matmul_kernel
    if residual is not None:
        in_specs.append(pl.BlockSpec((tm, tn), lambda i, j: (i, j)))
        args.append(residual)
        kern = _matmul_res_kernel
    return pl.pallas_call(
        kern,
        grid=(m // tm, n // tn),
        in_specs=in_specs,
        out_specs=pl.BlockSpec((tm, tn), lambda i, j: (i, j)),
        out_shape=jax.ShapeDtypeStruct((m, n), F32),
        compiler_params=_cparams(("parallel", "parallel")),
        name=name,
    )(*args)


def _rwkv_kernel(r_ref, k_ref, v_ref, gd_ref, wa_ref, lv_ref, vf_ref, p_ref, mu2_ref, w2_ref, a2_ref, g2_ref,
                 v2_ref, y_ref, vout_ref, s_ref, br_ref, bk_ref, bv_ref, bgd_ref, bwa_ref, *, tb, pp, has_vmix):
    i = pl.program_id(2)
    h = SUBLANES

    @pl.when(i == 0)
    def _():
        s_ref[...] = jnp.zeros_like(s_ref)
        for b in (br_ref, bk_ref, bv_ref, bgd_ref, bwa_ref):
            b[0:h, :] = jnp.zeros((h, b.shape[1]), F32)

    prm = p_ref[...]
    w0, a0, k_k, k_a, r_k, ln_w, ln_b, v0 = [prm[j:j + 1] for j in range(8)]
    mu_r, mu_k, mu_v = prm[8:9], prm[9:10], prm[10:11]
    mu2 = mu2_ref[...]
    mu_gd, mu_wa = mu2[:, 0:256], mu2[:, 256:384]

    def shifted_mix(x_ref, buf_ref, mu):
        x = x_ref[...]
        buf_ref[h:h + tb, :] = x
        xs = buf_ref[h - 1:h - 1 + tb, :]
        buf_ref[0:h, :] = x[tb - h:tb, :]
        return x + (xs - x) * mu

    r = shifted_mix(r_ref, br_ref, mu_r)
    k = shifted_mix(k_ref, bk_ref, mu_k)
    v = shifted_mix(v_ref, bv_ref, mu_v)
    gd = shifted_mix(gd_ref, bgd_ref, mu_gd)
    wa = shifted_mix(wa_ref, bwa_ref, mu_wa)

    lane = _iota2((1, LANES), 1)
    m0 = jnp.where(lane < A_HEAD, 1.0, 0.0)
    m1 = 1.0 - m0
    ri = _iota2((LANES, LANES), 0)
    ci = _iota2((LANES, LANES), 1)
    same = _chunk_of(ri) == _chunk_of(ci)
    strict = same & (ri > ci)
    incl = same & (ri >= ci)
    eye = jnp.where(ri == ci, 1.0, 0.0)
    ones_bd = jnp.where(same, 1.0, 0.0).astype(BF16)
    tri = _chunk_tri(tb)
    pair_lanes = [slice(q * LANES, (q + 1) * LANES) for q in range(pp)]

    def head_sum(x):
        return jnp.concatenate([_dot_sel_r(x[:, ls], ones_bd) for ls in pair_lanes], axis=1)

    lw = -jnp.exp(-0.5) * _sigmoid(w0 + _dot(jnp.tanh(wa), w2_ref[...]))
    a = _sigmoid(a0 + _dot(wa, a2_ref[...]))
    g = _dot(_sigmoid(gd), g2_ref[...])
    kk = k * k_k
    kk = kk / jnp.maximum(jnp.sqrt(head_sum(kk * kk)), 1e-12)
    k = k * (1.0 + (a - 1.0) * k_a)
    if has_vmix:
        v_mix = _sigmoid(v0 + _dot(lv_ref[...], v2_ref[...]))
        v = v + (vf_ref[...] - v) * v_mix
    vout_ref[...] = v

    cum = _dot_sel_l(tri, lw)
    e_pos = jnp.exp(cum)
    e_neg = jnp.exp(-cum)
    r_t = r * e_pos
    a_t = -kk * jnp.exp(cum - lw)
    b_t = (kk * a) * e_neg
    k_t = k * e_neg

    def stack_masked(x):
        return jnp.concatenate([x * m0, x * m1], axis=0)

    def stack_dup(x):
        return jnp.concatenate([x, x], axis=0)

    kb = kk * a
    nchunk = tb // CHUNK
    inst = [(c, q) for c in range(nchunk) for q in range(pp)]
    rows_of = lambda c: slice(c * CHUNK, (c + 1) * CHUNK)
    last_of = lambda c: slice((c + 1) * CHUNK - 1, (c + 1) * CHUNK)
    a_s = {cq: stack_masked(a_t[rows_of(cq[0]), pair_lanes[cq[1]]]) for cq in inst}
    r_s = {cq: stack_masked(r_t[rows_of(cq[0]), pair_lanes[cq[1]]]) for cq in inst}
    v_s = {cq: stack_masked(v[rows_of(cq[0]), pair_lanes[cq[1]]]) for cq in inst}
    gm = {}
    for c, q in inst:
        sl, ls = rows_of(c), pair_lanes[q]
        right = jnp.concatenate([stack_dup(b_t[sl, ls]), stack_dup(k_t[sl, ls])], axis=0)
        gm[c, q] = _dot_nt(jnp.concatenate([a_s[c, q], r_s[c, q]], axis=0), right)
    n_ab = [jnp.where(strict, gm[cq][0:LANES, 0:LANES], 0.0) for cq in inst]
    t_inv = dict(zip(inst, _neumann_inverse_many(n_ab, eye)))
    gv = {cq: _dot(jnp.where(strict, gm[cq][0:LANES, LANES:], 0.0), v_s[cq]) for cq in inst}
    tu = {cq: _dot(t_inv[cq], jnp.concatenate([a_s[cq], gv[cq]], axis=1)) for cq in inst}
    g_r = {cq: jnp.where(jnp.concatenate([incl, incl], axis=1), gm[cq][LANES:], 0.0) for cq in inst}

    states = [s_ref[q] for q in range(pp)]
    for c in range(nchunk):
        sl = rows_of(c)
        x0 = [_dot_nt(jnp.concatenate([tu[c, q][:, 0:LANES], r_s[c, q]], axis=0), states[q]) for q in range(pp)]
        uv = [jnp.concatenate([x0[q][0:LANES] + tu[c, q][:, LANES:], v_s[c, q]], axis=0) for q in range(pp)]
        for q, ls in enumerate(pair_lanes):
            cum_last = cum[last_of(c), ls]
            e_end = jnp.exp(cum_last - cum[sl, ls])
            bk_end = jnp.concatenate([stack_dup(kb[sl, ls] * e_end), stack_dup(k[sl, ls] * e_end)], axis=0)
            states[q] = jnp.where(same, states[q] * jnp.exp(cum_last) + _dot_tn(uv[q], bk_end), 0.0)
        for q, ls in enumerate(pair_lanes):
            y_s = x0[q][LANES:] + _dot(g_r[c, q], uv[q])
            y_ref[sl, ls] = y_s[0:CHUNK] + y_s[CHUNK:]
    for q in range(pp):
        s_ref[q] = states[q]

    y = y_ref[...]
    inv_n = 1.0 / A_HEAD
    mean = head_sum(y) * inv_n
    d = y - mean
    var = head_sum(d * d) * inv_n
    yn = d * lax.rsqrt(var + A_GN_EPS) * ln_w + ln_b
    bonus = head_sum(r * k * r_k) * v
    y_ref[...] = (yn + bonus) * g


def rwkv_mix(za, v_first, prm, mu2, w2p, a2p, g2p, v2p, *, batch, seq, tb=256, pp=4):
    has_vmix = v_first is not None
    nt = seq // tb
    w = pp * LANES
    npair = A_WIDTH // w
    row = lambda b, p, i: b * nt + i
    if v_first is None:
        v_first = za
    in_specs = [
        pl.BlockSpec((tb, w), lambda b, p, i: (row(b, p, i), p)),
        pl.BlockSpec((tb, w), lambda b, p, i: (row(b, p, i), npair + p)),
        pl.BlockSpec((tb, w), lambda b, p, i: (row(b, p, i), 2 * npair + p)),
        pl.BlockSpec((tb, 256), lambda b, p, i: (row(b, p, i), ZA_GD // 256)),
        pl.BlockSpec((tb, LANES), lambda b, p, i: (row(b, p, i), ZA_WA // LANES)),
        pl.BlockSpec((tb, LANES), lambda b, p, i: (row(b, p, i), ZA_LV // LANES)),
        pl.BlockSpec((tb, w), lambda b, p, i: (row(b, p, i), p)),
        pl.BlockSpec((16, w), lambda b, p, i: (0, p)),
        pl.BlockSpec((1, 384), lambda b, p, i: (0, 0)),
        pl.BlockSpec((LANES, w), lambda b, p, i: (0, p)),
        pl.BlockSpec((LANES, w), lambda b, p, i: (0, p)),
        pl.BlockSpec((256, w), lambda b, p, i: (0, p)),
        pl.BlockSpec((LANES, w), lambda b, p, i: (0, p)),
    ]
    out_spec = pl.BlockSpec((tb, w), lambda b, p, i: (row(b, p, i), p))
    out_sds = jax.ShapeDtypeStruct((batch * seq, A_WIDTH), F32)
    return pl.pallas_call(
        functools.partial(_rwkv_kernel, tb=tb, pp=pp, has_vmix=has_vmix),
        grid=(batch, npair, nt),
        in_specs=in_specs,
        out_specs=[out_spec, out_spec],
        out_shape=[out_sds, out_sds],
        scratch_shapes=[
            pltpu.VMEM((pp, LANES, LANES), F32),
            pltpu.VMEM((tb + SUBLANES, w), F32),
            pltpu.VMEM((tb + SUBLANES, w), F32),
            pltpu.VMEM((tb + SUBLANES, w), F32),
            pltpu.VMEM((tb + SUBLANES, 256), F32),
            pltpu.VMEM((tb + SUBLANES, LANES), F32),
        ],
        compiler_params=_cparams(("parallel", "parallel", "arbitrary")),
        name="rwkv7",
    )(za, za, za, za, za, za, v_first, prm, mu2, w2p, a2p, g2p, v2p)


def _attn_kernel(*refs, tq, dil, nb, has_prev):
    if has_prev:
        (q_ref, kc_ref, kp_ref, vc_ref, vp_ref, cc_ref, sc_ref, nc_ref, cp_ref, sp_ref, np_ref,
         o_ref, lse_ref) = refs
    else:
        q_ref, kc_ref, vc_ref, cc_ref, sc_ref, nc_ref, o_ref, lse_ref = refs
    i = pl.program_id(1)
    scale = B_HEAD ** -0.5

    def rows(j, r):
        start = j * tq * dil + r
        return pl.ds(start, tq, stride=dil) if dil > 1 else pl.ds(start, tq)

    def rope(x, tabs):
        half = ROPE_DIM // 2
        return (x * tabs[0] + pltpu.roll(x, half, axis=1) * tabs[1]
                + pltpu.roll(x, LANES - half, axis=1) * tabs[2])

    if has_prev:
        ri, ci = _iota2((tq, 2 * tq), 0), _iota2((tq, 2 * tq), 1)
        cur_ok = (ci >= tq) & ((ci - tq) <= ri)
        no_prev = jnp.where(i > 0, 0, 2 * tq)
        valid_first = ((ci < tq) & (ci >= ri + no_prev)) | cur_ok
        valid_inner = ((ci < tq) & (ci >= ri)) | cur_ok
    else:
        causal = _iota2((tq, tq), 1) <= _iota2((tq, tq), 0)

    inst = [(r, j) for r in range(dil) for j in range(nb)]
    qs, ks, vs = {}, {}, {}
    for r in range(dil):
        if has_prev:
            rp = rows(0, r)
            ks[r, -1] = rope(kp_ref[rp, :], (cp_ref[rp, :], sp_ref[rp, :], np_ref[rp, :]))
            vs[r, -1] = vp_ref[rp, :]
        for j in range(nb):
            rw = rows(j, r)
            tabs = (cc_ref[rw, :], sc_ref[rw, :], nc_ref[rw, :])
            qs[r, j] = rope(q_ref[rw, :], tabs) * scale
            ks[r, j] = rope(kc_ref[rw, :], tabs)
            vs[r, j] = vc_ref[rw, :]
    if has_prev:
        s = {(r, j): jnp.where(valid_first if j == 0 else valid_inner,
                               _dot_nt(qs[r, j], jnp.concatenate([ks[r, j - 1], ks[r, j]], axis=0)), -1e30)
             for r, j in inst}
    else:
        s = {rj: jnp.where(causal, _dot_nt(qs[rj], ks[rj]), -1e30) for rj in inst}
    m = {rj: jnp.max(s[rj], axis=-1, keepdims=True) for rj in inst}
    p = {rj: jnp.exp(s[rj] - m[rj]) for rj in inst}
    den = {rj: jnp.sum(p[rj], axis=-1, keepdims=True) for rj in inst}
    if has_prev:
        num = {(r, j): _dot(p[r, j], jnp.concatenate([vs[r, j - 1], vs[r, j]], axis=0)) for r, j in inst}
    else:
        num = {rj: _dot(p[rj], vs[rj]) for rj in inst}
    for r, j in inst:
        o_ref[rows(j, r), :] = num[r, j] / den[r, j]
        lse_ref[rows(j, r), :] = jnp.broadcast_to(m[r, j] + jnp.log(den[r, j]), (tq, B_HEAD))


def dilated_attention_group(zb, rope_tabs, gi, *, batch, seq, tq=128, blocks_per_step=8):
    win, dil = B_PAIRS[gi]
    assert win // dil == tq
    nq = seq // (dil * tq)
    has_prev = nq > 1
    nb = min(nq, max(1, blocks_per_step // dil))
    rows_step = nb * tq * dil
    nsteps = seq // rows_step
    prev_rows = tq * dil
    nprev = seq // prev_rows
    nh = B_HEADS_PER_GROUP
    cur = lambda col: pl.BlockSpec((rows_step, B_HEAD), lambda b, i, hh: (b * nsteps + i, col * nh + hh))
    prv = lambda col: pl.BlockSpec(
        (prev_rows, B_HEAD), lambda b, i, hh: (b * nprev + jnp.maximum(i * nb - 1, 0), col * nh + hh))
    tab_cur = pl.BlockSpec((rows_step, LANES), lambda b, i, hh: (i, 0))
    tab_prv = pl.BlockSpec((prev_rows, LANES), lambda b, i, hh: (jnp.maximum(i * nb - 1, 0), 0))
    qc, kc, vc = gi, B_GROUPS + gi, 2 * B_GROUPS + gi
    if has_prev:
        in_specs = [cur(qc), cur(kc), prv(kc), cur(vc), prv(vc)] + [tab_cur] * 3 + [tab_prv] * 3
        args = (zb, zb, zb, zb, zb) + tuple(rope_tabs) * 2
    else:
        in_specs = [cur(qc), cur(kc), cur(vc)] + [tab_cur] * 3
        args = (zb, zb, zb) + tuple(rope_tabs)
    out_spec = pl.BlockSpec((rows_step, B_HEAD), lambda b, i, hh: (b * nsteps + i, hh))
    out_sds = jax.ShapeDtypeStruct((batch * seq, B_OUT), F32)
    return pl.pallas_call(
        functools.partial(_attn_kernel, tq=tq, dil=dil, nb=nb, has_prev=has_prev),
        grid=(batch, nsteps, nh),
        in_specs=in_specs,
        out_specs=[out_spec, out_spec],
        out_shape=[out_sds, out_sds],
        compiler_params=_cparams(("parallel", "parallel", "arbitrary")),
        name=f"dilated_attn_g{gi}",
    )(*args)


def rope_tables(seq):
    half = ROPE_DIM // 2
    inv = ROPE_THETA ** (-jnp.arange(half, dtype=F32) / half)
    ang = jnp.arange(seq, dtype=F32)[:, None] * inv[None, :]
    cos, sin = jnp.cos(ang), jnp.sin(ang)
    z = jnp.zeros((seq, LANES - ROPE_DIM), F32)
    zh = jnp.zeros((seq, half), F32)
    c_tab = jnp.concatenate([cos, cos, jnp.ones_like(z)], axis=1)
    s_pos = jnp.concatenate([zh, sin, z], axis=1)
    s_neg = jnp.concatenate([-sin, zh, z], axis=1)
    return c_tab, s_pos, s_neg


def _gdn_kernel(q_ref, k_ref, v_ref, gate_ref, ba_ref, cw_ref, p_ref, o_ref, s_ref, bq_ref, bk_ref, bv_ref,
                *, tb, pp):
    pidx = pl.program_id(1)
    i = pl.program_id(2)
    h = SUBLANES
    hd = C_HEAD_K

    @pl.when(i == 0)
    def _():
        s_ref[...] = jnp.zeros_like(s_ref)
        for b in (bq_ref, bk_ref, bv_ref):
            b[0:h, :] = jnp.zeros((h, b.shape[1]), F32)

    cw = cw_ref[...]
    prm = p_ref[...]

    def conv_silu(x_ref, buf_ref, w):
        x = x_ref[...]
        buf_ref[h:h + tb, :] = x
        acc = x * w[C_CONV - 1:C_CONV]
        for j in range(C_CONV - 1):
            off = h - (C_CONV - 1) + j
            acc = acc + buf_ref[off:off + tb, :] * w[j:j + 1]
        buf_ref[0:h, :] = x[tb - h:tb, :]
        return acc * _sigmoid(acc)

    q = conv_silu(q_ref, bq_ref, cw[0])
    k = conv_silu(k_ref, bk_ref, cw[1])
    v = conv_silu(v_ref, bv_ref, cw[2])

    ri = _iota2((LANES, LANES), 0)
    ci = _iota2((LANES, LANES), 1)
    same = _chunk_of(ri) == _chunk_of(ci)
    strict = same & (ri > ci)
    incl = same & (ri >= ci)
    eye = jnp.where(ri == ci, 1.0, 0.0)
    ones = jnp.ones((hd, hd), BF16)
    tri = _chunk_tri(tb)

    ba = ba_ref[...]
    nh = 2 * pp
    head_lanes = [slice(hh * hd, (hh + 1) * hd) for hh in range(nh)]
    vh = [v[:, ls] for ls in head_lanes]
    q_ss = [_dot_sel_r(q[:, ls] * q[:, ls], ones) for ls in head_lanes]
    k_ss = [_dot_sel_r(k[:, ls] * k[:, ls], ones) for ls in head_lanes]
    b_raw = [_dot_sel_r(ba, jnp.where(ri == nh * pidx + hh, 1.0, 0.0).astype(BF16)) for hh in range(nh)]
    a_raw = [_dot_sel_r(ba, jnp.where(ri == nh * pidx + hh + C_HEADS, 1.0, 0.0).astype(BF16)) for hh in range(nh)]
    qh = [q[:, ls] / jnp.maximum(jnp.sqrt(ss), 1e-12) * (hd ** -0.5) for ls, ss in zip(head_lanes, q_ss)]
    kh = [k[:, ls] / jnp.maximum(jnp.sqrt(ss), 1e-12) for ls, ss in zip(head_lanes, k_ss)]
    beta = [_sigmoid(b) for b in b_raw]
    glog = [-jnp.exp(prm[0:1, ls]) * _softplus(al + prm[1:2, ls]) for ls, al in zip(head_lanes, a_raw)]
    gam = [_dot_sel_l(tri, gl) for gl in glog]

    nchunk = tb // CHUNK
    inst = [(c, pr) for c in range(nchunk) for pr in range(pp)]
    rows_of = lambda c: slice(c * CHUNK, (c + 1) * CHUNK)

    def stack(xs, cp):
        c, pr = cp
        return jnp.concatenate([xs[2 * pr][rows_of(c)], xs[2 * pr + 1][rows_of(c)]], axis=0)

    k_s = {cp: stack(kh, cp) for cp in inst}
    q_s = {cp: stack(qh, cp) for cp in inst}
    beta_s = {cp: stack(beta, cp) for cp in inst}
    gam_s = {cp: stack(gam, cp) for cp in inst}
    kq = {cp: _dot_nt(jnp.concatenate([k_s[cp], q_s[cp]], axis=0), k_s[cp]) for cp in inst}
    dm = {cp: jnp.exp(jnp.where(incl, gam_s[cp] - gam_s[cp].T, -1e30)) for cp in inst}
    n_mats = [jnp.where(strict, -(beta_s[cp] * kq[cp][0:LANES] * dm[cp]), 0.0) for cp in inst]
    t_inv = dict(zip(inst, _neumann_inverse_many(n_mats, eye)))
    e_gam = {cp: jnp.exp(gam_s[cp]) for cp in inst}
    sol = {cp: _dot(t_inv[cp], jnp.concatenate([stack(vh, cp) * beta_s[cp],
                                                k_s[cp] * (beta_s[cp] * e_gam[cp])], axis=1)) for cp in inst}

    states = [s_ref[hh] for hh in range(nh)]
    for c in range(nchunk):
        sl = rows_of(c)
        ws = []
        for hh in range(nh):
            cp, hs = (c, hh // 2), slice((hh % 2) * CHUNK, (hh % 2 + 1) * CHUNK)
            qg = (q_s[cp] * e_gam[cp])[hs]
            ws.append(_dot(jnp.concatenate([sol[cp][hs, hd:], qg], axis=0), states[hh]))
        v_new = [sol[c, hh // 2][(hh % 2) * CHUNK:(hh % 2 + 1) * CHUNK, 0:hd] - ws[hh][0:CHUNK]
                 for hh in range(nh)]
        for hh in range(nh):
            g_h = gam[hh][sl]
            g_last = g_h[CHUNK - 1:CHUNK, :]
            states[hh] = states[hh] * jnp.exp(g_last) + _dot_tn(kh[hh][sl] * jnp.exp(g_last - g_h), v_new[hh])
        for pr in range(pp):
            h0 = 2 * pr
            attn = kq[c, pr][LANES:] * dm[c, pr]
            o_s = (jnp.concatenate([ws[h0][CHUNK:], ws[h0 + 1][CHUNK:]], axis=0)
                   + _dot(attn, jnp.concatenate([v_new[h0], v_new[h0 + 1]], axis=0)))
            o_ref[sl, h0 * hd:(h0 + 1) * hd] = o_s[0:CHUNK]
            o_ref[sl, (h0 + 1) * hd:(h0 + 2) * hd] = o_s[CHUNK:]
    for hh in range(nh):
        s_ref[hh] = states[hh]

    gate = gate_ref[...]
    o_h = [o_ref[:, ls] for ls in head_lanes]
    o_ms = [_dot_sel_r(o * o, ones) * (1.0 / hd) for o in o_h]
    for o, ms, ls in zip(o_h, o_ms, head_lanes):
        gt = gate[:, ls]
        o_ref[:, ls] = (o * lax.rsqrt(ms + NORM_EPS) * prm[2:3, ls]) * (gt * _sigmoid(gt))


def gated_deltanet(zc, conv_w, prm, *, batch, seq, tb=256, pp=4):
    nt = seq // tb
    npair = C_HEADS // (2 * pp)
    wblk = 2 * pp * C_HEAD_K
    row = lambda b, p, i: b * nt + i
    in_specs = [
        pl.BlockSpec((tb, wblk), lambda b, p, i: (row(b, p, i), p)),
        pl.BlockSpec((tb, wblk), lambda b, p, i: (row(b, p, i), npair + p)),
        pl.BlockSpec((tb, wblk), lambda b, p, i: (row(b, p, i), 2 * npair + p)),
        pl.BlockSpec((tb, wblk), lambda b, p, i: (row(b, p, i), 3 * npair + p)),
        pl.BlockSpec((tb, LANES), lambda b, p, i: (row(b, p, i), ZC_BA // LANES)),
        pl.BlockSpec((3, C_CONV, wblk), lambda b, p, i: (0, 0, p)),
        pl.BlockSpec((SUBLANES, wblk), lambda b, p, i: (0, p)),
    ]
    return pl.pallas_call(
        functools.partial(_gdn_kernel, tb=tb, pp=pp),
        grid=(batch, npair, nt),
        in_specs=in_specs,
        out_specs=pl.BlockSpec((tb, wblk), lambda b, p, i: (row(b, p, i), p)),
        out_shape=jax.ShapeDtypeStruct((batch * seq, C_VW), F32),
        scratch_shapes=[
            pltpu.VMEM((2 * pp, C_HEAD_K, C_HEAD_V), F32),
            pltpu.VMEM((tb + SUBLANES, wblk), F32),
            pltpu.VMEM((tb + SUBLANES, wblk), F32),
            pltpu.VMEM((tb + SUBLANES, wblk), F32),
        ],
        compiler_params=_cparams(("parallel", "parallel", "arbitrary")),
        name="gated_deltanet",
    )(zc, zc, zc, zc, zc, conv_w, prm)


def _merge_kernel(ya_ref, o0_ref, o1_ref, o2_ref, l0_ref, l1_ref, l2_ref, yc_ref, ga_ref, gb_ref, gc_ref,
                  pa_ref, pb_ref, pc_ref, out_ref):
    l0, l1, l2 = l0_ref[...], l1_ref[...], l2_ref[...]
    m = jnp.maximum(jnp.maximum(l0, l1), l2)
    w0, w1, w2 = jnp.exp(l0 - m), jnp.exp(l1 - m), jnp.exp(l2 - m)
    yb = (w0 * o0_ref[...] + w1 * o1_ref[...] + w2 * o2_ref[...]) / (w0 + w1 + w2)
    merged = (_sigmoid(ga_ref[...]) * _dot(ya_ref[...], pa_ref[...])
              + _sigmoid(gb_ref[...]) * _dot(yb, pb_ref[...])
              + _sigmoid(gc_ref[...]) * _dot(yc_ref[...], pc_ref[...]))
    out_ref[...] = merged.astype(out_ref.dtype)


def merge_mixers(ya, attn, yc, zg, pa, pb, pc, tm=256):
    m = ya.shape[0]
    d = D_MODEL
    rows = lambda w: pl.BlockSpec((tm, w), lambda i: (i, 0))
    const = lambda a: pl.BlockSpec(a.shape, lambda i: (0, 0), pipeline_mode=pl.Buffered(1))
    (o0, l0), (o1, l1), (o2, l2) = attn
    in_specs = ([rows(A_WIDTH)] + [rows(B_OUT)] * 6 + [rows(C_VW)]
                + [pl.BlockSpec((tm, d), lambda i, j=j: (i, j)) for j in range(3)]
                + [const(pa), const(pb), const(pc)])
    return pl.pallas_call(
        _merge_kernel,
        grid=(m // tm,),
        in_specs=in_specs,
        out_specs=pl.BlockSpec((tm, d), lambda i: (i, 0)),
        out_shape=jax.ShapeDtypeStruct((m, d), BF16),
        compiler_params=_cparams(("parallel",)),
        name="merge_mixers",
    )(ya, o0, o1, o2, l0, l1, l2, yc, zg, zg, zg, pa, pb, pc)


def _ffn_up_act_kernel(x_ref, wg_ref, wv_ref, cw_ref, o_ref, buf_ref, halo_ref, *, tm, tf, blocks_per_seq):
    i = pl.program_id(0)
    j = pl.program_id(1)
    h = SUBLANES
    first = (i % blocks_per_seq) == 0

    @pl.when(first)
    def _():
        buf_ref[0:h, :] = jnp.zeros((h, 2 * tf), F32)

    @pl.when(jnp.logical_not(first))
    def _():
        buf_ref[0:h, :] = halo_ref[j]

    cw = cw_ref[...]

    def project(s):
        cs = slice(s * LANES, (s + 1) * LANES)
        w = jnp.concatenate([wg_ref[:, cs], wv_ref[:, cs]], axis=1)
        return jnp.dot(x_ref[...], w, preferred_element_type=F32)

    def conv(u, cols):
        buf_ref[h:h + tm, cols] = u
        halo_ref[j, :, cols] = u[tm - h:tm, :]
        acc = u * cw[FFN_CONV - 1:FFN_CONV, cols]
        for t in range(FFN_CONV - 1):
            off = h - (FFN_CONV - 1) + t
            acc = acc + buf_ref[off:off + tm, cols] * cw[t:t + 1, cols]
        return acc

    def gate(s, u):
        cg = conv(u[:, 0:LANES], slice(s * LANES, (s + 1) * LANES))
        cv = conv(u[:, LANES:], slice(tf + s * LANES, tf + (s + 1) * LANES))
        o_ref[:, s * LANES:(s + 1) * LANES] = ((cg * _sigmoid(cg)) * cv).astype(o_ref.dtype)

    nsub = tf // LANES
    pending = project(0)
    for s in range(nsub):
        nxt = project(s + 1) if s + 1 < nsub else None
        gate(s, pending)
        pending = nxt


def ffn_up_act(h2, w_up, conv_w, *, seq, tm=1024, tf=FFN_TF):
    m, d = h2.shape
    tm = min(tm, seq)
    nf = D_FF // tf
    return pl.pallas_call(
        functools.partial(_ffn_up_act_kernel, tm=tm, tf=tf, blocks_per_seq=seq // tm),
        grid=(m // tm, nf),
        in_specs=[
            pl.BlockSpec((tm, d), lambda i, j: (i, 0)),
            pl.BlockSpec((d, tf), lambda i, j: (0, j)),
            pl.BlockSpec((d, tf), lambda i, j: (0, nf + j)),
            pl.BlockSpec((None, FFN_CONV, 2 * tf), lambda i, j: (j, 0, 0)),
        ],
        out_specs=pl.BlockSpec((tm, tf), lambda i, j: (i, j)),
        out_shape=jax.ShapeDtypeStruct((m, D_FF), BF16),
        scratch_shapes=[
            pltpu.VMEM((tm + SUBLANES, 2 * tf), F32),
            pltpu.VMEM((nf, SUBLANES, 2 * tf), F32),
        ],
        compiler_params=_cparams(("arbitrary", "arbitrary")),
        name="ffn_up_act",
    )(h2, w_up, w_up, conv_w)


def _pad_cols(w, width):
    return jnp.pad(w, ((0, 0), (0, width - w.shape[1])))


def _pad_rows(w, rows, at=0):
    return jnp.pad(w, ((at, rows - at - w.shape[0]), (0, 0)))


def _layer_params(l, p):
    w_in = p["w_in_bf16"][l]
    a_end, b_end, c_end = A_IN, A_IN + B_IN, A_IN + B_IN + C_IN
    aw = 3 * A_WIDTH
    wa_cols = w_in[:, :a_end]
    lora_v = (p["rwkv_v1"][l - 1] if l > 0 else jnp.zeros((D_MODEL, A_VRES_LORA), F32)).astype(BF16)
    w_za = jnp.concatenate([
        wa_cols[:, :aw],
        _pad_cols(wa_cols[:, aw + A_DECAY_LORA + A_ICLR_LORA:], 256),
        wa_cols[:, aw:aw + A_DECAY_LORA + A_ICLR_LORA],
        _pad_cols(lora_v, 128)], axis=1)
    wc_cols = w_in[:, b_end:c_end]
    qkv_w = 2 * C_KW + C_VW
    w_zc = jnp.concatenate([
        wc_cols[:, :qkv_w],
        wc_cols[:, qkv_w + 2 * C_HEADS:],
        _pad_cols(wc_cols[:, qkv_w:qkv_w + 2 * C_HEADS], 128)], axis=1)
    mu = p["rwkv_mu"][l]
    rows = [p["rwkv_w0"][l], p["rwkv_a0"][l], p["rwkv_k_k"][l], p["rwkv_k_a"][l], p["rwkv_r_k"][l].reshape(-1),
            p["rwkv_ln_w"][l], p["rwkv_ln_b"][l],
            (p["rwkv_v0"][l - 1] if l > 0 else jnp.zeros((A_WIDTH,), F32)),
            mu[:A_WIDTH], mu[A_WIDTH:2 * A_WIDTH], mu[2 * A_WIDTH:aw]]
    rwkv_prm = jnp.pad(jnp.stack(rows), ((0, 16 - len(rows)), (0, 0)))
    mu2 = jnp.concatenate([_pad_cols(mu[None, aw + 128:], 256), mu[None, aw:aw + 128]], axis=1)
    v2 = (p["rwkv_v2"][l - 1] if l > 0 else jnp.zeros((A_VRES_LORA, A_WIDTH), F32))
    bcast = lambda t: jnp.repeat(t, C_HEAD_K)
    gdn_prm = jnp.pad(jnp.stack([bcast(p["gdn_A_log"][l]), bcast(p["gdn_dt_bias"][l]),
                                 jnp.tile(p["gdn_norm"][l], C_HEADS)]), ((0, SUBLANES - 3), (0, 0)))
    fc = p["ffn_conv"][l]
    return dict(
        w_za=w_za, w_zb=w_in[:, a_end:b_end], w_zc=w_zc, w_zg=w_in[:, c_end:],
        rwkv_prm=rwkv_prm, mu2=mu2,
        w2p=_pad_rows(p["rwkv_w2"][l], 128, 0).astype(BF16),
        a2p=_pad_rows(p["rwkv_a2"][l], 128, A_DECAY_LORA).astype(BF16),
        g2p=_pad_rows(p["rwkv_g2"][l], 256, 0).astype(BF16),
        v2p=_pad_rows(v2, 128, 0).astype(BF16),
        gdn_conv=p["gdn_conv"][l].reshape(C_CONV, 3, C_KW).transpose(1, 0, 2),
        gdn_prm=gdn_prm,
        pa=p["proj_a"][l].astype(BF16), pb=p["proj_b"][l].astype(BF16), pc=p["proj_c"][l].astype(BF16),
        w_out=p["w_out"][l].astype(BF16),
        ffn_up=p["ffn_up"][l].astype(BF16),
        ffn_conv=fc.reshape(FFN_CONV, 2, D_FF // FFN_TF, FFN_TF).transpose(2, 0, 1, 3).reshape(
            D_FF // FFN_TF, FFN_CONV, 2 * FFN_TF),
        ffn_down=p["ffn_down"][l].astype(BF16),
    )


def _forward(x, p):
    batch, seq, d = x.shape
    m = batch * seq
    xf = x.reshape(m, d)
    tab = rope_tables(seq)
    p = dict(p, w_in_bf16=lax.optimization_barrier(p["w_in"].astype(BF16)))
    v_first = None
    for l in range(DEPTH):
        lp = _layer_params(l, p)
        h = rmsnorm(xf, p["attn_norm"][l], BF16)
        za = matmul(h, lp["w_za"], tn=512, tm=2048, name="in_proj_a")
        zb = matmul(h, lp["w_zb"], tn=512, tm=2048, name="in_proj_b")
        zc = matmul(h, lp["w_zc"], tn=1408, tm=1024, name="in_proj_c")
        zg = matmul(h, lp["w_zg"], tn=1024, tm=2048, name="in_proj_g")
        ya, v_l = rwkv_mix(za, v_first, lp["rwkv_prm"], lp["mu2"], lp["w2p"], lp["a2p"], lp["g2p"], lp["v2p"],
                           batch=batch, seq=seq)
        if l == 0:
            v_first = v_l
        attn = [dilated_attention_group(zb, tab, gi, batch=batch, seq=seq) for gi in range(B_GROUPS)]
        yc = gated_deltanet(zc, lp["gdn_conv"], lp["gdn_prm"], batch=batch, seq=seq)
        merged = merge_mixers(ya, attn, yc, zg, lp["pa"], lp["pb"], lp["pc"])
        xf = matmul(merged, lp["w_out"], tn=512, tm=2048, residual=xf, name="out_proj")
        h2 = rmsnorm(xf, p["ffn_norm"][l], BF16)
        act = ffn_up_act(h2, lp["ffn_up"], lp["ffn_conv"], seq=seq)
        xf = matmul(act, lp["ffn_down"], tn=512, tm=1024, residual=xf, name="ffn_down")
    return rmsnorm(xf, p["final_norm"], F32).reshape(batch, seq, d)


def kernel(x, attn_norm, w_in, rwkv_mu, rwkv_w0, rwkv_w2, rwkv_a0, rwkv_a2, rwkv_g2, rwkv_k_k, rwkv_k_a, rwkv_r_k, rwkv_ln_w, rwkv_ln_b, rwkv_v0, rwkv_v1, rwkv_v2, gdn_conv, gdn_A_log, gdn_dt_bias, gdn_norm, proj_a, proj_b, proj_c, w_out, ffn_norm, ffn_up, ffn_conv, ffn_down, final_norm):
    params = dict(
        attn_norm=attn_norm, w_in=w_in, rwkv_mu=rwkv_mu, rwkv_w0=rwkv_w0, rwkv_w2=rwkv_w2, rwkv_a0=rwkv_a0,
        rwkv_a2=rwkv_a2, rwkv_g2=rwkv_g2, rwkv_k_k=rwkv_k_k, rwkv_k_a=rwkv_k_a, rwkv_r_k=rwkv_r_k,
        rwkv_ln_w=rwkv_ln_w, rwkv_ln_b=rwkv_ln_b, rwkv_v0=rwkv_v0, rwkv_v1=rwkv_v1, rwkv_v2=rwkv_v2,
        gdn_conv=gdn_conv, gdn_A_log=gdn_A_log, gdn_dt_bias=gdn_dt_bias, gdn_norm=gdn_norm, proj_a=proj_a,
        proj_b=proj_b, proj_c=proj_c, w_out=w_out, ffn_norm=ffn_norm, ffn_up=ffn_up, ffn_conv=ffn_conv,
        ffn_down=ffn_down, final_norm=final_norm)
    return _forward(x, params)
```

```python
import functools

import jax
import jax.numpy as jnp
from jax import lax
from jax.experimental import pallas as pl
from jax.experimental.pallas import tpu as pltpu

F32 = jnp.float32
BF16 = jnp.bfloat16

D_MODEL = 2048
DEPTH = 2
NORM_EPS = 1e-6

A_HEADS, A_HEAD = 16, 64
A_WIDTH = A_HEADS * A_HEAD
A_DECAY_LORA, A_ICLR_LORA, A_GATE_LORA, A_VRES_LORA = 64, 64, 160, 32
A_GN_EPS = 64e-5
A_IN = 3 * A_WIDTH + A_DECAY_LORA + A_ICLR_LORA + A_GATE_LORA

B_PAIRS = ((128, 1), (512, 4), (2048, 16))
B_GROUPS = 3
B_HEADS_PER_GROUP, B_HEAD = 4, 128
B_WIDTH = B_GROUPS * B_HEADS_PER_GROUP * B_HEAD
B_OUT = B_HEADS_PER_GROUP * B_HEAD
B_IN = 3 * B_WIDTH
ROPE_THETA = 500000.0
ROPE_DIM = B_HEAD // 4

C_HEADS, C_HEAD_K, C_HEAD_V = 8, 128, 128
C_KW = C_HEADS * C_HEAD_K
C_VW = C_HEADS * C_HEAD_V
C_CONV = 4
C_IN = 2 * C_KW + C_VW + 2 * C_HEADS + C_VW

D_FF = 5632
FFN_CONV = 3

LANES = 128
SUBLANES = 8
CHUNK = 64
FFN_TF = 1408
VMEM_LIMIT = 56 * 1024 * 1024

ZA_GD = 3 * A_WIDTH
ZA_WA = ZA_GD + 256
ZA_W = ZA_WA + 256
ZC_BA = 3 * C_KW + C_VW
ZC_W = ZC_BA + 128


def _cparams(sem):
    return pltpu.CompilerParams(dimension_semantics=sem, vmem_limit_bytes=VMEM_LIMIT)


def _dot(a, b):
    return jnp.dot(a.astype(BF16), b.astype(BF16), preferred_element_type=F32)


def _dot_nt(a, b):
    return lax.dot_general(a.astype(BF16), b.astype(BF16), (((1,), (1,)), ((), ())), preferred_element_type=F32)


def _dot_tn(a, b):
    return lax.dot_general(a.astype(BF16), b.astype(BF16), (((0,), (0,)), ((), ())), preferred_element_type=F32)


def _split(a):
    hi = a.astype(BF16)
    lo = (a - hi.astype(F32)).astype(BF16)
    return hi, lo


def _dot_sel_l(sel, a):
    hi, lo = _split(a)
    return (jnp.dot(sel, hi, preferred_element_type=F32) + jnp.dot(sel, lo, preferred_element_type=F32))


def _sigmoid(x):
    return 1.0 / (1.0 + jnp.exp(-x))


def _softplus(x):
    return jnp.maximum(x, 0.0) + jnp.log(1.0 + jnp.exp(-jnp.abs(x)))


def _iota2(shape, axis):
    return lax.broadcasted_iota(jnp.int32, shape, axis)


def _chunk_of(idx):
    return jnp.right_shift(idx, CHUNK.bit_length() - 1)


def _chunk_tri(n):
    ri, ci = _iota2((n, n), 0), _iota2((n, n), 1)
    return jnp.where((_chunk_of(ri) == _chunk_of(ci)) & (ri >= ci), 1.0, 0.0).astype(BF16)


def _neumann_inverse_many(n_mats, eye):
    ps = [eye + n for n in n_mats]
    qs = [_dot(n, n) for n in n_mats]
    levels = CHUNK.bit_length() - 2
    for lvl in range(levels):
        if lvl == levels - 1:
            ps = [p + _dot(p, q) for p, q in zip(ps, qs)]
        else:
            prods = [_dot(jnp.concatenate([q, p], axis=0), q) for p, q in zip(ps, qs)]
            ps = [p + pr[LANES:] for p, pr in zip(ps, prods)]
            qs = [pr[0:LANES] for pr in prods]
    return ps


def _neumann_inverse(n_mat, eye):
    return _neumann_inverse_many([n_mat], eye)[0]


def _rmsnorm_kernel(x_ref, g_ref, o_ref):
    x = x_ref[...]
    ms = jnp.mean(x * x, axis=-1, keepdims=True)
    o_ref[...] = ((x * lax.rsqrt(ms + NORM_EPS)) * g_ref[...]).astype(o_ref.dtype)


def rmsnorm(x, g, out_dtype, tm=512):
    m, d = x.shape
    return pl.pallas_call(
        _rmsnorm_kernel,
        grid=(m // tm,),
        in_specs=[pl.BlockSpec((tm, d), lambda i: (i, 0)), pl.BlockSpec((1, d), lambda i: (0, 0))],
        out_specs=pl.BlockSpec((tm, d), lambda i: (i, 0)),
        out_shape=jax.ShapeDtypeStruct((m, d), out_dtype),
        compiler_params=_cparams(("parallel",)),
        name="rmsnorm",
    )(x, g.reshape(1, d))


def _matmul_kernel(x_ref, w_ref, o_ref):
    o_ref[...] = jnp.dot(x_ref[...], w_ref[...], preferred_element_type=F32).astype(o_ref.dtype)


def matmul(x, w, tn, tm=1024, name="matmul"):
    m, k = x.shape
    n = w.shape[1]
    tm = min(tm, m)
    return pl.pallas_call(
        _matmul_kernel,
        grid=(m // tm, n // tn),
        in_specs=[pl.BlockSpec((tm, k), lambda i, j: (i, 0)), pl.BlockSpec((k, tn), lambda i, j: (0, j))],
        out_specs=pl.BlockSpec((tm, tn), lambda i, j: (i, j)),
        out_shape=jax.ShapeDtypeStruct((m, n), F32),
        compiler_params=_cparams(("parallel", "parallel")),
        name=name,
    )(x, w)


def _matmul_res_norm_kernel(x_ref, w_ref, r_ref, g_ref, *out_refs):
    y = r_ref[...] + jnp.dot(x_ref[...], w_ref[...], preferred_element_type=F32)
    ms = jnp.mean(y * y, axis=-1, keepdims=True)
    hn = (y * lax.rsqrt(ms + NORM_EPS)) * g_ref[...]
    if len(out_refs) == 2:
        out_refs[0][...] = y
    out_refs[-1][...] = hn.astype(out_refs[-1].dtype)


def matmul_res_norm(x, w, residual, gain, *, tm, norm_dtype, emit_sum, name):
    m, k = x.shape
    n = w.shape[1]
    rows = lambda dt: (pl.BlockSpec((tm, n), lambda i: (i, 0)), jax.ShapeDtypeStruct((m, n), dt))
    outs = ([rows(F32)] if emit_sum else []) + [rows(norm_dtype)]
    res = pl.pallas_call(
        _matmul_res_norm_kernel,
        grid=(m // tm,),
        in_specs=[pl.BlockSpec((tm, k), lambda i: (i, 0)),
                  pl.BlockSpec((k, n), lambda i: (0, 0), pipeline_mode=pl.Buffered(1)),
                  pl.BlockSpec((tm, n), lambda i: (i, 0)),
                  pl.BlockSpec((1, n), lambda i: (0, 0))],
        out_specs=[o[0] for o in outs],
        out_shape=[o[1] for o in outs],
        compiler_params=_cparams(("parallel",)),
        name=name,
    )(x, w, residual, gain.reshape(1, n))
    return tuple(res) if emit_sum else res[0]


def _rwkv_kernel(r_ref, k_ref, v_ref, gd_ref, wa_ref, lv_ref, vf_ref, p_ref, mu2_ref, w2_ref, a2_ref, g2_ref,
                 v2_ref, y_ref, vout_ref, s_ref, br_ref, bk_ref, bv_ref, bgd_ref, bwa_ref, *, tb, pp, has_vmix):
    i = pl.program_id(2)
    h = SUBLANES

    @pl.when(i == 0)
    def _():
        s_ref[...] = jnp.zeros_like(s_ref)
        for b in (br_ref, bk_ref, bv_ref, bgd_ref, bwa_ref):
            b[0:h, :] = jnp.zeros((h, b.shape[1]), F32)

    prm = p_ref[...]
    w0, a0, k_k, k_a, r_k, ln_w, ln_b, v0 = [prm[j:j + 1] for j in range(8)]
    mu_r, mu_k, mu_v = prm[8:9], prm[9:10], prm[10:11]
    mu2 = mu2_ref[...]
    mu_gd, mu_wa = mu2[:, 0:256], mu2[:, 256:384]

    def shifted_mix(x_ref, buf_ref, mu):
        x = x_ref[...]
        buf_ref[h:h + tb, :] = x
        xs = buf_ref[h - 1:h - 1 + tb, :]
        buf_ref[0:h, :] = x[tb - h:tb, :]
        return x + (xs - x) * mu

    r = shifted_mix(r_ref, br_ref, mu_r)
    k = shifted_mix(k_ref, bk_ref, mu_k)
    v = shifted_mix(v_ref, bv_ref, mu_v)
    gd = shifted_mix(gd_ref, bgd_ref, mu_gd)
    wa = shifted_mix(wa_ref, bwa_ref, mu_wa)

    lane = _iota2((1, LANES), 1)
    m0 = jnp.where(lane < A_HEAD, 1.0, 0.0)
    m1 = 1.0 - m0
    ri = _iota2((LANES, LANES), 0)
    ci = _iota2((LANES, LANES), 1)
    same = _chunk_of(ri) == _chunk_of(ci)
    strict = same & (ri > ci)
    incl = same & (ri >= ci)
    eye = jnp.where(ri == ci, 1.0, 0.0)
    tri = _chunk_tri(tb)
    pair_lanes = [slice(q * LANES, (q + 1) * LANES) for q in range(pp)]

    def head_sum(x):
        outs = []
        for ls in pair_lanes:
            s0 = jnp.sum(x[:, ls] * m0, axis=-1, keepdims=True)
            s1 = jnp.sum(x[:, ls] * m1, axis=-1, keepdims=True)
            outs.append(jnp.where(lane < A_HEAD, s0, s1))
        return jnp.concatenate(outs, axis=1)

    lw = -jnp.exp(-0.5) * _sigmoid(w0 + _dot(jnp.tanh(wa), w2_ref[...]))
    a = _sigmoid(a0 + _dot(wa, a2_ref[...]))
    g = _dot(_sigmoid(gd), g2_ref[...])
    kk = k * k_k
    kk = kk / jnp.maximum(jnp.sqrt(head_sum(kk * kk)), 1e-12)
    k = k * (1.0 + (a - 1.0) * k_a)
    if has_vmix:
        v_mix = _sigmoid(v0 + _dot(lv_ref[...], v2_ref[...]))
        v = v + (vf_ref[...] - v) * v_mix
    vout_ref[...] = v

    cum = _dot_sel_l(tri, lw)
    e_pos = jnp.exp(cum)
    e_neg = jnp.exp(-cum)
    r_t = r * e_pos
    a_t = -kk * jnp.exp(cum - lw)
    b_t = (kk * a) * e_neg
    k_t = k * e_neg

    def stack_masked(x):
        return jnp.concatenate([x * m0, x * m1], axis=0)

    def stack_dup(x):
        return jnp.concatenate([x, x], axis=0)

    kb = kk * a
    nchunk = tb // CHUNK
    inst = [(c, q) for c in range(nchunk) for q in range(pp)]
    rows_of = lambda c: slice(c * CHUNK, (c + 1) * CHUNK)
    last_of = lambda c: slice((c + 1) * CHUNK - 1, (c + 1) * CHUNK)
    a_s = {cq: stack_masked(a_t[rows_of(cq[0]), pair_lanes[cq[1]]]) for cq in inst}
    r_s = {cq: stack_masked(r_t[rows_of(cq[0]), pair_lanes[cq[1]]]) for cq in inst}
    v_s = {cq: stack_masked(v[rows_of(cq[0]), pair_lanes[cq[1]]]) for cq in inst}
    gm = {}
    for c, q in inst:
        sl, ls = rows_of(c), pair_lanes[q]
        right = jnp.concatenate([stack_dup(b_t[sl, ls]), stack_dup(k_t[sl, ls])], axis=0)
        gm[c, q] = _dot_nt(jnp.concatenate([a_s[c, q], r_s[c, q]], axis=0), right)
    n_ab = [jnp.where(strict, gm[cq][0:LANES, 0:LANES], 0.0) for cq in inst]
    t_inv = dict(zip(inst, _neumann_inverse_many(n_ab, eye)))
    gv = {cq: _dot(jnp.where(strict, gm[cq][0:LANES, LANES:], 0.0), v_s[cq]) for cq in inst}
    tu = {cq: _dot(t_inv[cq], jnp.concatenate([a_s[cq], gv[cq]], axis=1)) for cq in inst}
    g_r = {cq: jnp.where(jnp.concatenate([incl, incl], axis=1), gm[cq][LANES:], 0.0) for cq in inst}

    states = [s_ref[q] for q in range(pp)]
    for c in range(nchunk):
        sl = rows_of(c)
        x0 = [_dot_nt(jnp.concatenate([tu[c, q][:, 0:LANES], r_s[c, q]], axis=0), states[q]) for q in range(pp)]
        uv = [jnp.concatenate([x0[q][0:LANES] + tu[c, q][:, LANES:], v_s[c, q]], axis=0) for q in range(pp)]
        for q, ls in enumerate(pair_lanes):
            cum_last = cum[last_of(c), ls]
            e_end = jnp.exp(cum_last - cum[sl, ls])
            bk_end = jnp.concatenate([stack_dup(kb[sl, ls] * e_end), stack_dup(k[sl, ls] * e_end)], axis=0)
            states[q] = jnp.where(same, states[q] * jnp.exp(cum_last) + _dot_tn(uv[q], bk_end), 0.0)
        for q, ls in enumerate(pair_lanes):
            y_s = x0[q][LANES:] + _dot(g_r[c, q], uv[q])
            y_ref[sl, ls] = y_s[0:CHUNK] + y_s[CHUNK:]
    for q in range(pp):
        s_ref[q] = states[q]

    y = y_ref[...]
    inv_n = 1.0 / A_HEAD
    mean = head_sum(y) * inv_n
    d = y - mean
    var = head_sum(d * d) * inv_n
    yn = d * lax.rsqrt(var + A_GN_EPS) * ln_w + ln_b
    bonus = head_sum(r * k * r_k) * v
    y_ref[...] = (yn + bonus) * g


def rwkv_mix(za, lv, v_first, prm, mu2, w2p, a2p, g2p, v2p, *, batch, seq, tb=256, pp=4):
    has_vmix = v_first is not None
    nt = seq // tb
    w = pp * LANES
    npair = A_WIDTH // w
    row = lambda b, p, i: b * nt + i
    if v_first is None:
        v_first = lv = za
    in_specs = [
        pl.BlockSpec((tb, w), lambda b, p, i: (row(b, p, i), p)),
        pl.BlockSpec((tb, w), lambda b, p, i: (row(b, p, i), npair + p)),
        pl.BlockSpec((tb, w), lambda b, p, i: (row(b, p, i), 2 * npair + p)),
        pl.BlockSpec((tb, 256), lambda b, p, i: (row(b, p, i), ZA_GD // 256)),
        pl.BlockSpec((tb, LANES), lambda b, p, i: (row(b, p, i), ZA_WA // LANES)),
        pl.BlockSpec((tb, LANES), lambda b, p, i: (row(b, p, i), 0)),
        pl.BlockSpec((tb, w), lambda b, p, i: (row(b, p, i), p)),
        pl.BlockSpec((16, w), lambda b, p, i: (0, p)),
        pl.BlockSpec((1, 384), lambda b, p, i: (0, 0)),
        pl.BlockSpec((LANES, w), lambda b, p, i: (0, p)),
        pl.BlockSpec((LANES, w), lambda b, p, i: (0, p)),
        pl.BlockSpec((256, w), lambda b, p, i: (0, p)),
        pl.BlockSpec((LANES, w), lambda b, p, i: (0, p)),
    ]
    out_spec = pl.BlockSpec((tb, w), lambda b, p, i: (row(b, p, i), p))
    out_sds = jax.ShapeDtypeStruct((batch * seq, A_WIDTH), F32)
    return pl.pallas_call(
        functools.partial(_rwkv_kernel, tb=tb, pp=pp, has_vmix=has_vmix),
        grid=(batch, npair, nt),
        in_specs=in_specs,
        out_specs=[out_spec, out_spec],
        out_shape=[out_sds, out_sds],
        scratch_shapes=[
            pltpu.VMEM((pp, LANES, LANES), F32),
            pltpu.VMEM((tb + SUBLANES, w), F32),
            pltpu.VMEM((tb + SUBLANES, w), F32),
            pltpu.VMEM((tb + SUBLANES, w), F32),
            pltpu.VMEM((tb + SUBLANES, 256), F32),
            pltpu.VMEM((tb + SUBLANES, LANES), F32),
        ],
        compiler_params=_cparams(("parallel", "parallel", "arbitrary")),
        name="rwkv7",
    )(za, za, za, za, za, lv, v_first, prm, mu2, w2p, a2p, g2p, v2p)


def _attn_kernel(*refs, tq, dil, nb, has_prev):
    if has_prev:
        (q_ref, kc_ref, kp_ref, vc_ref, vp_ref, cc_ref, sc_ref, nc_ref, cp_ref, sp_ref, np_ref,
         o_ref, lse_ref) = refs
    else:
        q_ref, kc_ref, vc_ref, cc_ref, sc_ref, nc_ref, o_ref, lse_ref = refs
    i = pl.program_id(1)
    scale = B_HEAD ** -0.5

    def rows(j, r):
        start = j * tq * dil + r
        return pl.ds(start, tq, stride=dil) if dil > 1 else pl.ds(start, tq)

    def rope(x, tabs):
        half = ROPE_DIM // 2
        return (x * tabs[0] + pltpu.roll(x, half, axis=1) * tabs[1]
                + pltpu.roll(x, LANES - half, axis=1) * tabs[2])

    if has_prev:
        ri, ci = _iota2((tq, 2 * tq), 0), _iota2((tq, 2 * tq), 1)
        cur_ok = (ci >= tq) & ((ci - tq) <= ri)
        no_prev = jnp.where(i > 0, 0, 2 * tq)
        valid_first = ((ci < tq) & (ci >= ri + no_prev)) | cur_ok
        valid_inner = ((ci < tq) & (ci >= ri)) | cur_ok
    else:
        causal = _iota2((tq, tq), 1) <= _iota2((tq, tq), 0)

    inst = [(r, j) for r in range(dil) for j in range(nb)]
    qs, ks, vs = {}, {}, {}
    for r in range(dil):
        if has_prev:
            rp = rows(0, r)
            ks[r, -1] = rope(kp_ref[rp, :], (cp_ref[rp, :], sp_ref[rp, :], np_ref[rp, :]))
            vs[r, -1] = vp_ref[rp, :]
        for j in range(nb):
            rw = rows(j, r)
            tabs = (cc_ref[rw, :], sc_ref[rw, :], nc_ref[rw, :])
            qs[r, j] = rope(q_ref[rw, :], tabs) * scale
            ks[r, j] = rope(kc_ref[rw, :], tabs)
            vs[r, j] = vc_ref[rw, :]
    if has_prev:
        s = {(r, j): jnp.where(valid_first if j == 0 else valid_inner,
                               _dot_nt(qs[r, j], jnp.concatenate([ks[r, j - 1], ks[r, j]], axis=0)), -1e30)
             for r, j in inst}
    else:
        s = {rj: jnp.where(causal, _dot_nt(qs[rj], ks[rj]), -1e30) for rj in inst}
    m = {rj: jnp.max(s[rj], axis=-1, keepdims=True) for rj in inst}
    p = {rj: jnp.exp(s[rj] - m[rj]) for rj in inst}
    den = {rj: jnp.sum(p[rj], axis=-1, keepdims=True) for rj in inst}
    if has_prev:
        num = {(r, j): _dot(p[r, j], jnp.concatenate([vs[r, j - 1], vs[r, j]], axis=0)) for r, j in inst}
    else:
        num = {rj: _dot(p[rj], vs[rj]) for rj in inst}
    for r, j in inst:
        o_ref[rows(j, r), :] = num[r, j] / den[r, j]
        lse_ref[rows(j, r), :] = jnp.broadcast_to(m[r, j] + jnp.log(den[r, j]), (tq, B_HEAD))


def dilated_attention_group(zb, rope_tabs, gi, *, batch, seq, tq=128, blocks_per_step=8):
    win, dil = B_PAIRS[gi]
    assert win // dil == tq
    nq = seq // (dil * tq)
    has_prev = nq > 1
    nb = min(nq, max(1, blocks_per_step // dil))
    rows_step = nb * tq * dil
    nsteps = seq // rows_step
    prev_rows = tq * dil
    nprev = seq // prev_rows
    nh = B_HEADS_PER_GROUP
    cur = lambda col: pl.BlockSpec((rows_step, B_HEAD), lambda b, i, hh: (b * nsteps + i, col * nh + hh))
    prv = lambda col: pl.BlockSpec(
        (prev_rows, B_HEAD), lambda b, i, hh: (b * nprev + jnp.maximum(i * nb - 1, 0), col * nh + hh))
    tab_cur = pl.BlockSpec((rows_step, LANES), lambda b, i, hh: (i, 0))
    tab_prv = pl.BlockSpec((prev_rows, LANES), lambda b, i, hh: (jnp.maximum(i * nb - 1, 0), 0))
    qc, kc, vc = gi, B_GROUPS + gi, 2 * B_GROUPS + gi
    if has_prev:
        in_specs = [cur(qc), cur(kc), prv(kc), cur(vc), prv(vc)] + [tab_cur] * 3 + [tab_prv] * 3
        args = (zb, zb, zb, zb, zb) + tuple(rope_tabs) * 2
    else:
        in_specs = [cur(qc), cur(kc), cur(vc)] + [tab_cur] * 3
        args = (zb, zb, zb) + tuple(rope_tabs)
    out_spec = pl.BlockSpec((rows_step, B_HEAD), lambda b, i, hh: (b * nsteps + i, hh))
    out_sds = jax.ShapeDtypeStruct((batch * seq, B_OUT), F32)
    return pl.pallas_call(
        functools.partial(_attn_kernel, tq=tq, dil=dil, nb=nb, has_prev=has_prev),
        grid=(batch, nsteps, nh),
        in_specs=in_specs,
        out_specs=[out_spec, out_spec],
        out_shape=[out_sds, out_sds],
        compiler_params=_cparams(("parallel", "parallel", "arbitrary")),
        name=f"dilated_attn_g{gi}",
    )(*args)


def rope_tables(seq):
    half = ROPE_DIM // 2
    inv = ROPE_THETA ** (-jnp.arange(half, dtype=F32) / half)
    ang = jnp.arange(seq, dtype=F32)[:, None] * inv[None, :]
    cos, sin = jnp.cos(ang), jnp.sin(ang)
    z = jnp.zeros((seq, LANES - ROPE_DIM), F32)
    zh = jnp.zeros((seq, half), F32)
    c_tab = jnp.concatenate([cos, cos, jnp.ones_like(z)], axis=1)
    s_pos = jnp.concatenate([zh, sin, z], axis=1)
    s_neg = jnp.concatenate([-sin, zh, z], axis=1)
    return c_tab, s_pos, s_neg


def _gdn_kernel(q_ref, k_ref, v_ref, gate_ref, ba_ref, cw_ref, p_ref, o_ref, s_ref, bq_ref, bk_ref, bv_ref,
                *, tb, pp):
    i = pl.program_id(2)
    h = SUBLANES
    hd = C_HEAD_K

    @pl.when(i == 0)
    def _():
        s_ref[...] = jnp.zeros_like(s_ref)
        for b in (bq_ref, bk_ref, bv_ref):
            b[0:h, :] = jnp.zeros((h, b.shape[1]), F32)

    cw = cw_ref[...]
    prm = p_ref[...]

    def conv_silu(x_ref, buf_ref, w):
        x = x_ref[...]
        buf_ref[h:h + tb, :] = x
        acc = x * w[C_CONV - 1:C_CONV]
        for j in range(C_CONV - 1):
            off = h - (C_CONV - 1) + j
            acc = acc + buf_ref[off:off + tb, :] * w[j:j + 1]
        buf_ref[0:h, :] = x[tb - h:tb, :]
        return acc * _sigmoid(acc)

    q = conv_silu(q_ref, bq_ref, cw[0])
    k = conv_silu(k_ref, bk_ref, cw[1])
    v = conv_silu(v_ref, bv_ref, cw[2])

    ri = _iota2((LANES, LANES), 0)
    ci = _iota2((LANES, LANES), 1)
    same = _chunk_of(ri) == _chunk_of(ci)
    strict = same & (ri > ci)
    incl = same & (ri >= ci)
    eye = jnp.where(ri == ci, 1.0, 0.0)
    tri = _chunk_tri(tb)

    ba = ba_ref[...]
    nh = 2 * pp
    head_lanes = [slice(hh * hd, (hh + 1) * hd) for hh in range(nh)]
    vh = [v[:, ls] for ls in head_lanes]
    q_ss = [jnp.sum(q[:, ls] * q[:, ls], axis=-1, keepdims=True) for ls in head_lanes]
    k_ss = [jnp.sum(k[:, ls] * k[:, ls], axis=-1, keepdims=True) for ls in head_lanes]
    b_raw = [ba[:, hh:hh + 1] for hh in range(nh)]
    a_raw = [ba[:, C_HEADS + hh:C_HEADS + hh + 1] for hh in range(nh)]
    qh = [q[:, ls] / jnp.maximum(jnp.sqrt(ss), 1e-12) * (hd ** -0.5) for ls, ss in zip(head_lanes, q_ss)]
    kh = [k[:, ls] / jnp.maximum(jnp.sqrt(ss), 1e-12) for ls, ss in zip(head_lanes, k_ss)]
    beta = [_sigmoid(b) for b in b_raw]
    glog = [-jnp.exp(prm[0:1, ls]) * _softplus(al + prm[1:2, ls]) for ls, al in zip(head_lanes, a_raw)]
    gam = [_dot_sel_l(tri, gl) for gl in glog]

    nchunk = tb // CHUNK
    inst = [(c, pr) for c in range(nchunk) for pr in range(pp)]
    rows_of = lambda c: slice(c * CHUNK, (c + 1) * CHUNK)

    def stack(xs, cp):
        c, pr = cp
        return jnp.concatenate([xs[2 * pr][rows_of(c)], xs[2 * pr + 1][rows_of(c)]], axis=0)

    k_s = {cp: stack(kh, cp) for cp in inst}
    q_s = {cp: stack(qh, cp) for cp in inst}
    beta_s = {cp: stack(beta, cp) for cp in inst}
    gam_s = {cp: stack(gam, cp) for cp in inst}
    kq = {cp: _dot_nt(jnp.concatenate([k_s[cp], q_s[cp]], axis=0), k_s[cp]) for cp in inst}
    dm = {cp: jnp.exp(jnp.where(incl, gam_s[cp] - gam_s[cp].T, -1e30)) for cp in inst}
    n_mats = [jnp.where(strict, -(beta_s[cp] * kq[cp][0:LANES] * dm[cp]), 0.0) for cp in inst]
    t_inv = dict(zip(inst, _neumann_inverse_many(n_mats, eye)))
    e_gam = {cp: jnp.exp(gam_s[cp]) for cp in inst}
    sol = {cp: _dot(t_inv[cp], jnp.concatenate([stack(vh, cp) * beta_s[cp],
                                                k_s[cp] * (beta_s[cp] * e_gam[cp])], axis=1)) for cp in inst}

    states = [s_ref[hh] for hh in range(nh)]
    for c in range(nchunk):
        sl = rows_of(c)
        ws = []
        for hh in range(nh):
            cp, hs = (c, hh // 2), slice((hh % 2) * CHUNK, (hh % 2 + 1) * CHUNK)
            qg = (q_s[cp] * e_gam[cp])[hs]
            ws.append(_dot(jnp.concatenate([sol[cp][hs, hd:], qg], axis=0), states[hh]))
        v_new = [sol[c, hh // 2][(hh % 2) * CHUNK:(hh % 2 + 1) * CHUNK, 0:hd] - ws[hh][0:CHUNK]
                 for hh in range(nh)]
        for hh in range(nh):
            g_h = gam[hh][sl]
            g_last = g_h[CHUNK - 1:CHUNK, :]
            states[hh] = states[hh] * jnp.exp(g_last) + _dot_tn(kh[hh][sl] * jnp.exp(g_last - g_h), v_new[hh])
        for pr in range(pp):
            h0 = 2 * pr
            attn = kq[c, pr][LANES:] * dm[c, pr]
            o_s = (jnp.concatenate([ws[h0][CHUNK:], ws[h0 + 1][CHUNK:]], axis=0)
                   + _dot(attn, jnp.concatenate([v_new[h0], v_new[h0 + 1]], axis=0)))
            o_ref[sl, h0 * hd:(h0 + 1) * hd] = o_s[0:CHUNK]
            o_ref[sl, (h0 + 1) * hd:(h0 + 2) * hd] = o_s[CHUNK:]
    for hh in range(nh):
        s_ref[hh] = states[hh]

    gate = gate_ref[...]
    o_h = [o_ref[:, ls] for ls in head_lanes]
    o_ms = [jnp.mean(o * o, axis=-1, keepdims=True) for o in o_h]
    for o, ms, ls in zip(o_h, o_ms, head_lanes):
        gt = gate[:, ls]
        o_ref[:, ls] = (o * lax.rsqrt(ms + NORM_EPS) * prm[2:3, ls]) * (gt * _sigmoid(gt))


def gated_deltanet(zc, conv_w, prm, *, batch, seq, tb=256, pp=4):
    assert 2 * pp == C_HEADS, "the kernel indexes the per-head beta/alpha columns statically"
    nt = seq // tb
    npair = C_HEADS // (2 * pp)
    wblk = 2 * pp * C_HEAD_K
    row = lambda b, p, i: b * nt + i
    in_specs = [
        pl.BlockSpec((tb, wblk), lambda b, p, i: (row(b, p, i), p)),
        pl.BlockSpec((tb, wblk), lambda b, p, i: (row(b, p, i), npair + p)),
        pl.BlockSpec((tb, wblk), lambda b, p, i: (row(b, p, i), 2 * npair + p)),
        pl.BlockSpec((tb, wblk), lambda b, p, i: (row(b, p, i), 3 * npair + p)),
        pl.BlockSpec((tb, LANES), lambda b, p, i: (row(b, p, i), ZC_BA // LANES)),
        pl.BlockSpec((3, C_CONV, wblk), lambda b, p, i: (0, 0, p)),
        pl.BlockSpec((SUBLANES, wblk), lambda b, p, i: (0, p)),
    ]
    return pl.pallas_call(
        functools.partial(_gdn_kernel, tb=tb, pp=pp),
        grid=(batch, npair, nt),
        in_specs=in_specs,
        out_specs=pl.BlockSpec((tb, wblk), lambda b, p, i: (row(b, p, i), p)),
        out_shape=jax.ShapeDtypeStruct((batch * seq, C_VW), F32),
        scratch_shapes=[
            pltpu.VMEM((2 * pp, C_HEAD_K, C_HEAD_V), F32),
            pltpu.VMEM((tb + SUBLANES, wblk), F32),
            pltpu.VMEM((tb + SUBLANES, wblk), F32),
            pltpu.VMEM((tb + SUBLANES, wblk), F32),
        ],
        compiler_params=_cparams(("parallel", "parallel", "arbitrary")),
        name="gated_deltanet",
    )(zc, zc, zc, zc, zc, conv_w, prm)


def _merge_kernel(ya_ref, o0_ref, o1_ref, o2_ref, l0_ref, l1_ref, l2_ref, yc_ref, ga_ref, gb_ref, gc_ref,
                  pa_ref, pb_ref, pc_ref, out_ref):
    l0, l1, l2 = l0_ref[...], l1_ref[...], l2_ref[...]
    m = jnp.maximum(jnp.maximum(l0, l1), l2)
    w0, w1, w2 = jnp.exp(l0 - m), jnp.exp(l1 - m), jnp.exp(l2 - m)
    yb = (w0 * o0_ref[...] + w1 * o1_ref[...] + w2 * o2_ref[...]) / (w0 + w1 + w2)
    merged = (_sigmoid(ga_ref[...]) * _dot(ya_ref[...], pa_ref[...])
              + _sigmoid(gb_ref[...]) * _dot(yb, pb_ref[...])
              + _sigmoid(gc_ref[...]) * _dot(yc_ref[...], pc_ref[...]))
    out_ref[...] = merged.astype(out_ref.dtype)


def merge_mixers(ya, attn, yc, zg, pa, pb, pc, tm=256):
    m = ya.shape[0]
    d = D_MODEL
    rows = lambda w: pl.BlockSpec((tm, w), lambda i: (i, 0))
    const = lambda a: pl.BlockSpec(a.shape, lambda i: (0, 0), pipeline_mode=pl.Buffered(1))
    (o0, l0), (o1, l1), (o2, l2) = attn
    in_specs = ([rows(A_WIDTH)] + [rows(B_OUT)] * 6 + [rows(C_VW)]
                + [pl.BlockSpec((tm, d), lambda i, j=j: (i, j)) for j in range(3)]
                + [const(pa), const(pb), const(pc)])
    return pl.pallas_call(
        _merge_kernel,
        grid=(m // tm,),
        in_specs=in_specs,
        out_specs=pl.BlockSpec((tm, d), lambda i: (i, 0)),
        out_shape=jax.ShapeDtypeStruct((m, d), BF16),
        compiler_params=_cparams(("parallel",)),
        name="merge_mixers",
    )(ya, o0, o1, o2, l0, l1, l2, yc, zg, zg, zg, pa, pb, pc)


def _ffn_up_act_kernel(x_ref, wg_ref, wv_ref, cw_ref, o_ref, buf_ref, halo_ref, *, tm, tf, blocks_per_seq):
    i = pl.program_id(0)
    j = pl.program_id(1)
    h = SUBLANES
    first = (i % blocks_per_seq) == 0

    @pl.when(first)
    def _():
        buf_ref[0:h, :] = jnp.zeros((h, 2 * tf), F32)

    @pl.when(jnp.logical_not(first))
    def _():
        buf_ref[0:h, :] = halo_ref[j]

    cw = cw_ref[...]

    def project(s):
        cs = slice(s * LANES, (s + 1) * LANES)
        w = jnp.concatenate([wg_ref[:, cs], wv_ref[:, cs]], axis=1)
        return jnp.dot(x_ref[...], w, preferred_element_type=F32)

    def conv(u, cols):
        buf_ref[h:h + tm, cols] = u
        halo_ref[j, :, cols] = u[tm - h:tm, :]
        acc = u * cw[FFN_CONV - 1:FFN_CONV, cols]
        for t in range(FFN_CONV - 1):
            off = h - (FFN_CONV - 1) + t
            acc = acc + buf_ref[off:off + tm, cols] * cw[t:t + 1, cols]
        return acc

    def gate(s, u):
        cg = conv(u[:, 0:LANES], slice(s * LANES, (s + 1) * LANES))
        cv = conv(u[:, LANES:], slice(tf + s * LANES, tf + (s + 1) * LANES))
        o_ref[:, s * LANES:(s + 1) * LANES] = ((cg * _sigmoid(cg)) * cv).astype(o_ref.dtype)

    nsub = tf // LANES
    pending = project(0)
    for s in range(nsub):
        nxt = project(s + 1) if s + 1 < nsub else None
        gate(s, pending)
        pending = nxt


def ffn_up_act(h2, w_up, conv_w, *, seq, tm=1024, tf=FFN_TF):
    m, d = h2.shape
    tm = min(tm, seq)
    nf = D_FF // tf
    return pl.pallas_call(
        functools.partial(_ffn_up_act_kernel, tm=tm, tf=tf, blocks_per_seq=seq // tm),
        grid=(m // tm, nf),
        in_specs=[
            pl.BlockSpec((tm, d), lambda i, j: (i, 0)),
            pl.BlockSpec((d, tf), lambda i, j: (0, j)),
            pl.BlockSpec((d, tf), lambda i, j: (0, nf + j)),
            pl.BlockSpec((None, FFN_CONV, 2 * tf), lambda i, j: (j, 0, 0)),
        ],
        out_specs=pl.BlockSpec((tm, tf), lambda i, j: (i, j)),
        out_shape=jax.ShapeDtypeStruct((m, D_FF), BF16),
        scratch_shapes=[
            pltpu.VMEM((tm + SUBLANES, 2 * tf), F32),
            pltpu.VMEM((nf, SUBLANES, 2 * tf), F32),
        ],
        compiler_params=_cparams(("arbitrary", "arbitrary")),
        name="ffn_up_act",
    )(h2, w_up, w_up, conv_w)


def _prep_w_in_kernel(w_ref, za_ref, zb_ref, zc_ref, zg_ref):
    tr = w_ref.shape[0]
    aw = 3 * A_WIDTH
    lora = A_DECAY_LORA + A_ICLR_LORA

    def cols(lo, hi):
        lo_al = lo - lo % LANES
        hi_al = min(-(-hi // LANES) * LANES, w_ref.shape[1])
        return w_ref[:, lo_al:hi_al][:, lo - lo_al:hi - lo_al]

    def pad_to(x, width):
        return jnp.concatenate([x, jnp.zeros((tr, width - x.shape[1]), x.dtype)], axis=1)

    za_ref[:, 0:aw] = cols(0, aw).astype(BF16)
    za_ref[:, ZA_GD:ZA_WA] = pad_to(cols(aw + lora, A_IN), ZA_WA - ZA_GD).astype(BF16)
    za_ref[:, ZA_WA:ZA_W] = pad_to(cols(aw, aw + lora), ZA_W - ZA_WA).astype(BF16)
    zb_ref[...] = cols(A_IN, A_IN + B_IN).astype(BF16)
    c0 = A_IN + B_IN
    qkv = 2 * C_KW + C_VW
    zc_ref[:, 0:qkv] = cols(c0, c0 + qkv).astype(BF16)
    zc_ref[:, qkv:ZC_BA] = cols(c0 + qkv + 2 * C_HEADS, c0 + C_IN).astype(BF16)
    zc_ref[:, ZC_BA:ZC_W] = pad_to(cols(c0 + qkv, c0 + qkv + 2 * C_HEADS), ZC_W - ZC_BA).astype(BF16)
    zg_ref[...] = cols(c0 + C_IN, w_ref.shape[1]).astype(BF16)


def prep_w_in(w_in, l, tr=128):
    _, d, n_in = w_in.shape
    widths = (ZA_W, B_IN, ZC_W, 3 * D_MODEL)
    return pl.pallas_call(
        _prep_w_in_kernel,
        grid=(d // tr,),
        in_specs=[pl.BlockSpec((None, tr, n_in), lambda i: (l, i, 0))],
        out_specs=[pl.BlockSpec((tr, w), lambda i: (i, 0)) for w in widths],
        out_shape=[jax.ShapeDtypeStruct((d, w), BF16) for w in widths],
        compiler_params=_cparams(("parallel",)),
        name="prep_w_in",
    )(w_in)


def _pad_cols(w, width):
    return jnp.pad(w, ((0, 0), (0, width - w.shape[1])))


def _pad_rows(w, rows, at=0):
    return jnp.pad(w, ((at, rows - at - w.shape[0]), (0, 0)))


def _layer_params(l, p):
    aw = 3 * A_WIDTH
    mu = p["rwkv_mu"][l]
    rows = [p["rwkv_w0"][l], p["rwkv_a0"][l], p["rwkv_k_k"][l], p["rwkv_k_a"][l], p["rwkv_r_k"][l].reshape(-1),
            p["rwkv_ln_w"][l], p["rwkv_ln_b"][l],
            (p["rwkv_v0"][l - 1] if l > 0 else jnp.zeros((A_WIDTH,), F32)),
            mu[:A_WIDTH], mu[A_WIDTH:2 * A_WIDTH], mu[2 * A_WIDTH:aw]]
    rwkv_prm = jnp.pad(jnp.stack(rows), ((0, 16 - len(rows)), (0, 0)))
    mu2 = jnp.concatenate([_pad_cols(mu[None, aw + 128:], 256), mu[None, aw:aw + 128]], axis=1)
    v2 = (p["rwkv_v2"][l - 1] if l > 0 else jnp.zeros((A_VRES_LORA, A_WIDTH), F32))
    bcast = lambda t: jnp.repeat(t, C_HEAD_K)
    gdn_prm = jnp.pad(jnp.stack([bcast(p["gdn_A_log"][l]), bcast(p["gdn_dt_bias"][l]),
                                 jnp.tile(p["gdn_norm"][l], C_HEADS)]), ((0, SUBLANES - 3), (0, 0)))
    fc = p["ffn_conv"][l]
    return dict(
        v1p=(_pad_cols(p["rwkv_v1"][l - 1], LANES).astype(BF16) if l > 0 else None),
        rwkv_prm=rwkv_prm, mu2=mu2,
        w2p=_pad_rows(p["rwkv_w2"][l], 128, 0).astype(BF16),
        a2p=_pad_rows(p["rwkv_a2"][l], 128, A_DECAY_LORA).astype(BF16),
        g2p=_pad_rows(p["rwkv_g2"][l], 256, 0).astype(BF16),
        v2p=_pad_rows(v2, 128, 0).astype(BF16),
        gdn_conv=p["gdn_conv"][l].reshape(C_CONV, 3, C_KW).transpose(1, 0, 2),
        gdn_prm=gdn_prm,
        pa=p["proj_a"][l].astype(BF16), pb=p["proj_b"][l].astype(BF16), pc=p["proj_c"][l].astype(BF16),
        w_out=p["w_out"][l].astype(BF16),
        ffn_up=p["ffn_up"][l].astype(BF16),
        ffn_conv=fc.reshape(FFN_CONV, 2, D_FF // FFN_TF, FFN_TF).transpose(2, 0, 1, 3).reshape(
            D_FF // FFN_TF, FFN_CONV, 2 * FFN_TF),
        ffn_down=p["ffn_down"][l].astype(BF16),
    )


def _forward(x, p):
    batch, seq, d = x.shape
    m = batch * seq
    xf = x.reshape(m, d)
    tab = rope_tables(seq)
    v_first = None
    h = rmsnorm(xf, p["attn_norm"][0], BF16)
    for l in range(DEPTH):
        lp = _layer_params(l, p)
        w_za, w_zb, w_zc, w_zg = prep_w_in(p["w_in"], l)
        za = matmul(h, w_za, tn=512, tm=2048, name="in_proj_a")
        zb = matmul(h, w_zb, tn=512, tm=2048, name="in_proj_b")
        zc = matmul(h, w_zc, tn=1408, tm=1024, name="in_proj_c")
        zg = matmul(h, w_zg, tn=1024, tm=2048, name="in_proj_g")
        lv = matmul(h, lp["v1p"], tn=LANES, tm=2048, name="vres_lora") if l > 0 else None
        ya, v_l = rwkv_mix(za, lv, v_first, lp["rwkv_prm"], lp["mu2"], lp["w2p"], lp["a2p"], lp["g2p"], lp["v2p"],
                           batch=batch, seq=seq)
        if l == 0:
            v_first = v_l
        attn = [dilated_attention_group(zb, tab, gi, batch=batch, seq=seq) for gi in range(B_GROUPS)]
        yc = gated_deltanet(zc, lp["gdn_conv"], lp["gdn_prm"], batch=batch, seq=seq)
        merged = merge_mixers(ya, attn, yc, zg, lp["pa"], lp["pb"], lp["pc"])
        xf, h2 = matmul_res_norm(merged, lp["w_out"], xf, p["ffn_norm"][l], tm=512, norm_dtype=BF16,
                                 emit_sum=True, name="out_proj_norm")
        act = ffn_up_act(h2, lp["ffn_up"], lp["ffn_conv"], seq=seq)
        if l + 1 < DEPTH:
            xf, h = matmul_res_norm(act, lp["ffn_down"], xf, p["attn_norm"][l + 1], tm=256, norm_dtype=BF16,
                                    emit_sum=True, name="ffn_down_norm")
        else:
            out = matmul_res_norm(act, lp["ffn_down"], xf, p["final_norm"], tm=256, norm_dtype=F32,
                                  emit_sum=False, name="ffn_down_norm")
    return out.reshape(batch, seq, d)


def kernel(x, attn_norm, w_in, rwkv_mu, rwkv_w0, rwkv_w2, rwkv_a0, rwkv_a2, rwkv_g2, rwkv_k_k, rwkv_k_a, rwkv_r_k, rwkv_ln_w, rwkv_ln_b, rwkv_v0, rwkv_v1, rwkv_v2, gdn_conv, gdn_A_log, gdn_dt_bias, gdn_norm, proj_a, proj_b, proj_c, w_out, ffn_norm, ffn_up, ffn_conv, ffn_down, final_norm):
    params = dict(
        attn_norm=attn_norm, w_in=w_in, rwkv_mu=rwkv_mu, rwkv_w0=rwkv_w0, rwkv_w2=rwkv_w2, rwkv_a0=rwkv_a0,
        rwkv_a2=rwkv_a2, rwkv_g2=rwkv_g2, rwkv_k_k=rwkv_k_k, rwkv_k_a=rwkv_k_a, rwkv_r_k=rwkv_r_k,
        rwkv_ln_w=rwkv_ln_w, rwkv_ln_b=rwkv_ln_b, rwkv_v0=rwkv_v0, rwkv_v1=rwkv_v1, rwkv_v2=rwkv_v2,
        gdn_conv=gdn_conv, gdn_A_log=gdn_A_log, gdn_dt_bias=gdn_dt_bias, gdn_norm=gdn_norm, proj_a=proj_a,
        proj_b=proj_b, proj_c=proj_c, w_out=w_out, ffn_norm=ffn_norm, ffn_up=ffn_up, ffn_conv=ffn_conv,
        ffn_down=ffn_down, final_norm=final_norm)
    return _forward(x, params)
```

```python
import functools

import jax
import jax.numpy as jnp
from jax import lax
from jax.experimental import pallas as pl
from jax.experimental.pallas import tpu as pltpu

F32 = jnp.float32
BF16 = jnp.bfloat16

D_MODEL = 2048
DEPTH = 2
NORM_EPS = 1e-6

A_HEADS, A_HEAD = 16, 64
A_WIDTH = A_HEADS * A_HEAD
A_DECAY_LORA, A_ICLR_LORA, A_GATE_LORA, A_VRES_LORA = 64, 64, 160, 32
A_GN_EPS = 64e-5
A_IN = 3 * A_WIDTH + A_DECAY_LORA + A_ICLR_LORA + A_GATE_LORA

B_PAIRS = ((128, 1), (512, 4), (2048, 16))
B_GROUPS = 3
B_HEADS_PER_GROUP, B_HEAD = 4, 128
B_WIDTH = B_GROUPS * B_HEADS_PER_GROUP * B_HEAD
B_OUT = B_HEADS_PER_GROUP * B_HEAD
B_IN = 3 * B_WIDTH
ROPE_THETA = 500000.0
ROPE_DIM = B_HEAD // 4

C_HEADS, C_HEAD_K, C_HEAD_V = 8, 128, 128
C_KW = C_HEADS * C_HEAD_K
C_VW = C_HEADS * C_HEAD_V
C_CONV = 4
C_IN = 2 * C_KW + C_VW + 2 * C_HEADS + C_VW

D_FF = 5632
FFN_CONV = 3

LANES = 128
SUBLANES = 8
CHUNK = 64
FFN_TF = 1408
VMEM_LIMIT = 56 * 1024 * 1024

ZA_GD = 3 * A_WIDTH
ZA_WA = ZA_GD + 256
ZA_W = ZA_WA + 256
ZC_BA = 3 * C_KW + C_VW
ZC_W = ZC_BA + 128


def _cparams(sem):
    return pltpu.CompilerParams(dimension_semantics=sem, vmem_limit_bytes=VMEM_LIMIT)


def _dot(a, b):
    return jnp.dot(a.astype(BF16), b.astype(BF16), preferred_element_type=F32)


def _dot_nt(a, b):
    return lax.dot_general(a.astype(BF16), b.astype(BF16), (((1,), (1,)), ((), ())), preferred_element_type=F32)


def _dot_tn(a, b):
    return lax.dot_general(a.astype(BF16), b.astype(BF16), (((0,), (0,)), ((), ())), preferred_element_type=F32)


def _split(a):
    hi = a.astype(BF16)
    lo = (a - hi.astype(F32)).astype(BF16)
    return hi, lo


def _dot_sel_l(sel, a):
    hi, lo = _split(a)
    return (jnp.dot(sel, hi, preferred_element_type=F32) + jnp.dot(sel, lo, preferred_element_type=F32))


def _sigmoid(x):
    return 1.0 / (1.0 + jnp.exp(-x))


def _softplus(x):
    return jnp.maximum(x, 0.0) + jnp.log(1.0 + jnp.exp(-jnp.abs(x)))


def _iota2(shape, axis):
    return lax.broadcasted_iota(jnp.int32, shape, axis)


def _chunk_of(idx):
    return jnp.right_shift(idx, CHUNK.bit_length() - 1)


def _chunk_tri(n):
    ri, ci = _iota2((n, n), 0), _iota2((n, n), 1)
    return jnp.where((_chunk_of(ri) == _chunk_of(ci)) & (ri >= ci), 1.0, 0.0).astype(BF16)


def _neumann_inverse_many(n_mats, eye):
    ps = [eye + n for n in n_mats]
    qs = [_dot(n, n) for n in n_mats]
    levels = CHUNK.bit_length() - 2
    for lvl in range(levels):
        if lvl == levels - 1:
            ps = [p + _dot(p, q) for p, q in zip(ps, qs)]
        else:
            prods = [_dot(jnp.concatenate([q, p], axis=0), q) for p, q in zip(ps, qs)]
            ps = [p + pr[LANES:] for p, pr in zip(ps, prods)]
            qs = [pr[0:LANES] for pr in prods]
    return ps


def _neumann_inverse(n_mat, eye):
    return _neumann_inverse_many([n_mat], eye)[0]


def _rmsnorm_kernel(x_ref, g_ref, o_ref):
    x = x_ref[...]
    ms = jnp.mean(x * x, axis=-1, keepdims=True)
    o_ref[...] = ((x * lax.rsqrt(ms + NORM_EPS)) * g_ref[...]).astype(o_ref.dtype)


def rmsnorm(x, g, out_dtype, tm=512):
    m, d = x.shape
    return pl.pallas_call(
        _rmsnorm_kernel,
        grid=(m // tm,),
        in_specs=[pl.BlockSpec((tm, d), lambda i: (i, 0)), pl.BlockSpec((1, d), lambda i: (0, 0))],
        out_specs=pl.BlockSpec((tm, d), lambda i: (i, 0)),
        out_shape=jax.ShapeDtypeStruct((m, d), out_dtype),
        compiler_params=_cparams(("parallel",)),
        name="rmsnorm",
    )(x, g.reshape(1, d))


def _matmul_kernel(x_ref, w_ref, o_ref):
    o_ref[...] = jnp.dot(x_ref[...], w_ref[...], preferred_element_type=F32).astype(o_ref.dtype)


def matmul(x, w, tn, tm=1024, name="matmul"):
    m, k = x.shape
    n = w.shape[1]
    tm = min(tm, m)
    return pl.pallas_call(
        _matmul_kernel,
        grid=(m // tm, n // tn),
        in_specs=[pl.BlockSpec((tm, k), lambda i, j: (i, 0)), pl.BlockSpec((k, tn), lambda i, j: (0, j))],
        out_specs=pl.BlockSpec((tm, tn), lambda i, j: (i, j)),
        out_shape=jax.ShapeDtypeStruct((m, n), F32),
        compiler_params=_cparams(("parallel", "parallel")),
        name=name,
    )(x, w)


def _matmul_res_norm_kernel(x_ref, w_ref, r_ref, g_ref, *out_refs):
    y = r_ref[...] + jnp.dot(x_ref[...], w_ref[...], preferred_element_type=F32)
    ms = jnp.mean(y * y, axis=-1, keepdims=True)
    hn = (y * lax.rsqrt(ms + NORM_EPS)) * g_ref[...]
    if len(out_refs) == 2:
        out_refs[0][...] = y
    out_refs[-1][...] = hn.astype(out_refs[-1].dtype)


def matmul_res_norm(x, w, layer, residual, gain, *, tm, norm_dtype, emit_sum, name):
    m, k = x.shape
    n = w.shape[1]
    rows = lambda dt: (pl.BlockSpec((tm, n), lambda i: (i, 0)), jax.ShapeDtypeStruct((m, n), dt))
    outs = ([rows(F32)] if emit_sum else []) + [rows(norm_dtype)]
    res = pl.pallas_call(
        _matmul_res_norm_kernel,
        grid=(m // tm,),
        in_specs=[pl.BlockSpec((tm, k), lambda i: (i, 0)),
                  pl.BlockSpec((k, n), lambda i: (layer, 0), pipeline_mode=pl.Buffered(1)),
                  pl.BlockSpec((tm, n), lambda i: (i, 0)),
                  pl.BlockSpec((1, n), lambda i: (0, 0))],
        out_specs=[o[0] for o in outs],
        out_shape=[o[1] for o in outs],
        compiler_params=_cparams(("parallel",)),
        name=name,
    )(x, w, residual, gain.reshape(1, n))
    return tuple(res) if emit_sum else res[0]


def _rwkv_kernel(r_ref, k_ref, v_ref, gd_ref, wa_ref, lv_ref, vf_ref, p_ref, mu2_ref, w2_ref, a2_ref, g2_ref,
                 v2_ref, y_ref, vout_ref, s_ref, br_ref, bk_ref, bv_ref, bgd_ref, bwa_ref, *, tb, pp, has_vmix):
    i = pl.program_id(2)
    h = SUBLANES

    @pl.when(i == 0)
    def _():
        s_ref[...] = jnp.zeros_like(s_ref)
        for b in (br_ref, bk_ref, bv_ref, bgd_ref, bwa_ref):
            b[0:h, :] = jnp.zeros((h, b.shape[1]), F32)

    prm = p_ref[...]
    w0, a0, k_k, k_a, r_k, ln_w, ln_b, v0 = [prm[j:j + 1] for j in range(8)]
    mu_r, mu_k, mu_v = prm[8:9], prm[9:10], prm[10:11]
    mu2 = mu2_ref[...]
    mu_gd, mu_wa = mu2[:, 0:256], mu2[:, 256:384]

    def shifted_mix(x_ref, buf_ref, mu):
        x = x_ref[...]
        buf_ref[h:h + tb, :] = x
        xs = buf_ref[h - 1:h - 1 + tb, :]
        buf_ref[0:h, :] = x[tb - h:tb, :]
        return x + (xs - x) * mu

    r = shifted_mix(r_ref, br_ref, mu_r)
    k = shifted_mix(k_ref, bk_ref, mu_k)
    v = shifted_mix(v_ref, bv_ref, mu_v)
    gd = shifted_mix(gd_ref, bgd_ref, mu_gd)
    wa = shifted_mix(wa_ref, bwa_ref, mu_wa)

    lane = _iota2((1, LANES), 1)
    m0 = jnp.where(lane < A_HEAD, 1.0, 0.0)
    m1 = 1.0 - m0
    ri = _iota2((LANES, LANES), 0)
    ci = _iota2((LANES, LANES), 1)
    same = _chunk_of(ri) == _chunk_of(ci)
    strict = same & (ri > ci)
    incl = same & (ri >= ci)
    eye = jnp.where(ri == ci, 1.0, 0.0)
    tri = _chunk_tri(tb)
    pair_lanes = [slice(q * LANES, (q + 1) * LANES) for q in range(pp)]

    def head_sum(x):
        outs = []
        for ls in pair_lanes:
            s0 = jnp.sum(x[:, ls] * m0, axis=-1, keepdims=True)
            s1 = jnp.sum(x[:, ls] * m1, axis=-1, keepdims=True)
            outs.append(jnp.where(lane < A_HEAD, s0, s1))
        return jnp.concatenate(outs, axis=1)

    lw = -jnp.exp(-0.5) * _sigmoid(w0 + _dot(jnp.tanh(wa), w2_ref[...]))
    a = _sigmoid(a0 + _dot(wa, a2_ref[...]))
    g = _dot(_sigmoid(gd), g2_ref[...])
    kk = k * k_k
    kk = kk / jnp.maximum(jnp.sqrt(head_sum(kk * kk)), 1e-12)
    k = k * (1.0 + (a - 1.0) * k_a)
    if has_vmix:
        v_mix = _sigmoid(v0 + _dot(lv_ref[...], v2_ref[...]))
        v = v + (vf_ref[...] - v) * v_mix
    vout_ref[...] = v

    cum = _dot_sel_l(tri, lw)
    e_pos = jnp.exp(cum)
    e_neg = jnp.exp(-cum)
    r_t = r * e_pos
    a_t = -kk * jnp.exp(cum - lw)
    b_t = (kk * a) * e_neg
    k_t = k * e_neg

    def stack_masked(x):
        return jnp.concatenate([x * m0, x * m1], axis=0)

    def stack_dup(x):
        return jnp.concatenate([x, x], axis=0)

    kb = kk * a
    nchunk = tb // CHUNK
    inst = [(c, q) for c in range(nchunk) for q in range(pp)]
    rows_of = lambda c: slice(c * CHUNK, (c + 1) * CHUNK)
    last_of = lambda c: slice((c + 1) * CHUNK - 1, (c + 1) * CHUNK)
    a_s = {cq: stack_masked(a_t[rows_of(cq[0]), pair_lanes[cq[1]]]) for cq in inst}
    r_s = {cq: stack_masked(r_t[rows_of(cq[0]), pair_lanes[cq[1]]]) for cq in inst}
    v_s = {cq: stack_masked(v[rows_of(cq[0]), pair_lanes[cq[1]]]) for cq in inst}
    gm = {}
    for c, q in inst:
        sl, ls = rows_of(c), pair_lanes[q]
        right = jnp.concatenate([stack_dup(b_t[sl, ls]), stack_dup(k_t[sl, ls])], axis=0)
        gm[c, q] = _dot_nt(jnp.concatenate([a_s[c, q], r_s[c, q]], axis=0), right)
    n_ab = [jnp.where(strict, gm[cq][0:LANES, 0:LANES], 0.0) for cq in inst]
    t_inv = dict(zip(inst, _neumann_inverse_many(n_ab, eye)))
    gv = {cq: _dot(jnp.where(strict, gm[cq][0:LANES, LANES:], 0.0), v_s[cq]) for cq in inst}
    tu = {cq: _dot(t_inv[cq], jnp.concatenate([a_s[cq], gv[cq]], axis=1)) for cq in inst}
    g_r = {cq: jnp.where(jnp.concatenate([incl, incl], axis=1), gm[cq][LANES:], 0.0) for cq in inst}

    states = [s_ref[q] for q in range(pp)]
    for c in range(nchunk):
        sl = rows_of(c)
        x0 = [_dot_nt(jnp.concatenate([tu[c, q][:, 0:LANES], r_s[c, q]], axis=0), states[q]) for q in range(pp)]
        uv = [jnp.concatenate([x0[q][0:LANES] + tu[c, q][:, LANES:], v_s[c, q]], axis=0) for q in range(pp)]
        for q, ls in enumerate(pair_lanes):
            cum_last = cum[last_of(c), ls]
            e_end = jnp.exp(cum_last - cum[sl, ls])
            bk_end = jnp.concatenate([stack_dup(kb[sl, ls] * e_end), stack_dup(k[sl, ls] * e_end)], axis=0)
            states[q] = jnp.where(same, states[q] * jnp.exp(cum_last) + _dot_tn(uv[q], bk_end), 0.0)
        for q, ls in enumerate(pair_lanes):
            y_s = x0[q][LANES:] + _dot(g_r[c, q], uv[q])
            y_ref[sl, ls] = y_s[0:CHUNK] + y_s[CHUNK:]
    for q in range(pp):
        s_ref[q] = states[q]

    y = y_ref[...]
    inv_n = 1.0 / A_HEAD
    mean = head_sum(y) * inv_n
    d = y - mean
    var = head_sum(d * d) * inv_n
    yn = d * lax.rsqrt(var + A_GN_EPS) * ln_w + ln_b
    bonus = head_sum(r * k * r_k) * v
    y_ref[...] = (yn + bonus) * g


def rwkv_mix(za, lv, v_first, prm, mu2, w2p, a2p, g2p, v2p, *, batch, seq, tb=256, pp=4):
    has_vmix = v_first is not None
    nt = seq // tb
    w = pp * LANES
    npair = A_WIDTH // w
    row = lambda b, p, i: b * nt + i
    if v_first is None:
        v_first = lv = za
    in_specs = [
        pl.BlockSpec((tb, w), lambda b, p, i: (row(b, p, i), p)),
        pl.BlockSpec((tb, w), lambda b, p, i: (row(b, p, i), npair + p)),
        pl.BlockSpec((tb, w), lambda b, p, i: (row(b, p, i), 2 * npair + p)),
        pl.BlockSpec((tb, 256), lambda b, p, i: (row(b, p, i), ZA_GD // 256)),
        pl.BlockSpec((tb, LANES), lambda b, p, i: (row(b, p, i), ZA_WA // LANES)),
        pl.BlockSpec((tb, LANES), lambda b, p, i: (row(b, p, i), 0)),
        pl.BlockSpec((tb, w), lambda b, p, i: (row(b, p, i), p)),
        pl.BlockSpec((16, w), lambda b, p, i: (0, p)),
        pl.BlockSpec((1, 384), lambda b, p, i: (0, 0)),
        pl.BlockSpec((LANES, w), lambda b, p, i: (0, p)),
        pl.BlockSpec((LANES, w), lambda b, p, i: (0, p)),
        pl.BlockSpec((256, w), lambda b, p, i: (0, p)),
        pl.BlockSpec((LANES, w), lambda b, p, i: (0, p)),
    ]
    out_spec = pl.BlockSpec((tb, w), lambda b, p, i: (row(b, p, i), p))
    out_sds = jax.ShapeDtypeStruct((batch * seq, A_WIDTH), F32)
    return pl.pallas_call(
        functools.partial(_rwkv_kernel, tb=tb, pp=pp, has_vmix=has_vmix),
        grid=(batch, npair, nt),
        in_specs=in_specs,
        out_specs=[out_spec, out_spec],
        out_shape=[out_sds, out_sds],
        scratch_shapes=[
            pltpu.VMEM((pp, LANES, LANES), F32),
            pltpu.VMEM((tb + SUBLANES, w), F32),
            pltpu.VMEM((tb + SUBLANES, w), F32),
            pltpu.VMEM((tb + SUBLANES, w), F32),
            pltpu.VMEM((tb + SUBLANES, 256), F32),
            pltpu.VMEM((tb + SUBLANES, LANES), F32),
        ],
        compiler_params=_cparams(("parallel", "parallel", "arbitrary")),
        name="rwkv7",
    )(za, za, za, za, za, lv, v_first, prm, mu2, w2p, a2p, g2p, v2p)


def _attn_kernel(*refs, tq, dil, nb, has_prev):
    if has_prev:
        (q_ref, kc_ref, kp_ref, vc_ref, vp_ref, cc_ref, sc_ref, nc_ref, cp_ref, sp_ref, np_ref,
         o_ref, lse_ref) = refs
    else:
        q_ref, kc_ref, vc_ref, cc_ref, sc_ref, nc_ref, o_ref, lse_ref = refs
    i = pl.program_id(1)
    scale = B_HEAD ** -0.5

    def rows(j, r):
        start = j * tq * dil + r
        return pl.ds(start, tq, stride=dil) if dil > 1 else pl.ds(start, tq)

    def rope(x, tabs):
        half = ROPE_DIM // 2
        return (x * tabs[0] + pltpu.roll(x, half, axis=1) * tabs[1]
                + pltpu.roll(x, LANES - half, axis=1) * tabs[2])

    if has_prev:
        ri, ci = _iota2((tq, 2 * tq), 0), _iota2((tq, 2 * tq), 1)
        cur_ok = (ci >= tq) & ((ci - tq) <= ri)
        no_prev = jnp.where(i > 0, 0, 2 * tq)
        valid_first = ((ci < tq) & (ci >= ri + no_prev)) | cur_ok
        valid_inner = ((ci < tq) & (ci >= ri)) | cur_ok
    else:
        causal = _iota2((tq, tq), 1) <= _iota2((tq, tq), 0)

    inst = [(r, j) for r in range(dil) for j in range(nb)]
    qs, ks, vs = {}, {}, {}
    for r in range(dil):
        if has_prev:
            rp = rows(0, r)
            ks[r, -1] = rope(kp_ref[rp, :], (cp_ref[rp, :], sp_ref[rp, :], np_ref[rp, :]))
            vs[r, -1] = vp_ref[rp, :]
        for j in range(nb):
            rw = rows(j, r)
            tabs = (cc_ref[rw, :], sc_ref[rw, :], nc_ref[rw, :])
            qs[r, j] = rope(q_ref[rw, :], tabs) * scale
            ks[r, j] = rope(kc_ref[rw, :], tabs)
            vs[r, j] = vc_ref[rw, :]
    if has_prev:
        s = {(r, j): jnp.where(valid_first if j == 0 else valid_inner,
                               _dot_nt(qs[r, j], jnp.concatenate([ks[r, j - 1], ks[r, j]], axis=0)), -1e30)
             for r, j in inst}
    else:
        s = {rj: jnp.where(causal, _dot_nt(qs[rj], ks[rj]), -1e30) for rj in inst}
    m = {rj: jnp.max(s[rj], axis=-1, keepdims=True) for rj in inst}
    p = {rj: jnp.exp(s[rj] - m[rj]) for rj in inst}
    den = {rj: jnp.sum(p[rj], axis=-1, keepdims=True) for rj in inst}
    if has_prev:
        num = {(r, j): _dot(p[r, j], jnp.concatenate([vs[r, j - 1], vs[r, j]], axis=0)) for r, j in inst}
    else:
        num = {rj: _dot(p[rj], vs[rj]) for rj in inst}
    for r, j in inst:
        o_ref[rows(j, r), :] = num[r, j] / den[r, j]
        lse_ref[rows(j, r), :] = jnp.broadcast_to(m[r, j] + jnp.log(den[r, j]), (tq, B_HEAD))


def dilated_attention_group(zb, rope_tabs, gi, *, batch, seq, tq=128, blocks_per_step=8):
    win, dil = B_PAIRS[gi]
    assert win // dil == tq
    nq = seq // (dil * tq)
    has_prev = nq > 1
    nb = min(nq, max(1, blocks_per_step // dil))
    rows_step = nb * tq * dil
    nsteps = seq // rows_step
    prev_rows = tq * dil
    nprev = seq // prev_rows
    nh = B_HEADS_PER_GROUP
    cur = lambda col: pl.BlockSpec((rows_step, B_HEAD), lambda b, i, hh: (b * nsteps + i, col * nh + hh))
    prv = lambda col: pl.BlockSpec(
        (prev_rows, B_HEAD), lambda b, i, hh: (b * nprev + jnp.maximum(i * nb - 1, 0), col * nh + hh))
    tab_cur = pl.BlockSpec((rows_step, LANES), lambda b, i, hh: (i, 0))
    tab_prv = pl.BlockSpec((prev_rows, LANES), lambda b, i, hh: (jnp.maximum(i * nb - 1, 0), 0))
    qc, kc, vc = gi, B_GROUPS + gi, 2 * B_GROUPS + gi
    if has_prev:
        in_specs = [cur(qc), cur(kc), prv(kc), cur(vc), prv(vc)] + [tab_cur] * 3 + [tab_prv] * 3
        args = (zb, zb, zb, zb, zb) + tuple(rope_tabs) * 2
    else:
        in_specs = [cur(qc), cur(kc), cur(vc)] + [tab_cur] * 3
        args = (zb, zb, zb) + tuple(rope_tabs)
    out_spec = pl.BlockSpec((rows_step, B_HEAD), lambda b, i, hh: (b * nsteps + i, hh))
    out_sds = jax.ShapeDtypeStruct((batch * seq, B_OUT), F32)
    return pl.pallas_call(
        functools.partial(_attn_kernel, tq=tq, dil=dil, nb=nb, has_prev=has_prev),
        grid=(batch, nsteps, nh),
        in_specs=in_specs,
        out_specs=[out_spec, out_spec],
        out_shape=[out_sds, out_sds],
        compiler_params=_cparams(("parallel", "parallel", "arbitrary")),
        name=f"dilated_attn_g{gi}",
    )(*args)


def rope_tables(seq):
    half = ROPE_DIM // 2
    inv = ROPE_THETA ** (-jnp.arange(half, dtype=F32) / half)
    ang = jnp.arange(seq, dtype=F32)[:, None] * inv[None, :]
    cos, sin = jnp.cos(ang), jnp.sin(ang)
    z = jnp.zeros((seq, LANES - ROPE_DIM), F32)
    zh = jnp.zeros((seq, half), F32)
    c_tab = jnp.concatenate([cos, cos, jnp.ones_like(z)], axis=1)
    s_pos = jnp.concatenate([zh, sin, z], axis=1)
    s_neg = jnp.concatenate([-sin, zh, z], axis=1)
    return c_tab, s_pos, s_neg


def _gdn_kernel(q_ref, k_ref, v_ref, gate_ref, ba_ref, cw_ref, p_ref, o_ref, s_ref, bq_ref, bk_ref, bv_ref,
                *, tb, pp):
    i = pl.program_id(2)
    h = SUBLANES
    hd = C_HEAD_K

    @pl.when(i == 0)
    def _():
        s_ref[...] = jnp.zeros_like(s_ref)
        for b in (bq_ref, bk_ref, bv_ref):
            b[0:h, :] = jnp.zeros((h, b.shape[1]), F32)

    cw = cw_ref[...]
    prm = p_ref[...]

    def conv_silu(x_ref, buf_ref, w):
        x = x_ref[...]
        buf_ref[h:h + tb, :] = x
        acc = x * w[C_CONV - 1:C_CONV]
        for j in range(C_CONV - 1):
            off = h - (C_CONV - 1) + j
            acc = acc + buf_ref[off:off + tb, :] * w[j:j + 1]
        buf_ref[0:h, :] = x[tb - h:tb, :]
        return acc * _sigmoid(acc)

    q = conv_silu(q_ref, bq_ref, cw[0])
    k = conv_silu(k_ref, bk_ref, cw[1])
    v = conv_silu(v_ref, bv_ref, cw[2])

    ri = _iota2((LANES, LANES), 0)
    ci = _iota2((LANES, LANES), 1)
    same = _chunk_of(ri) == _chunk_of(ci)
    strict = same & (ri > ci)
    incl = same & (ri >= ci)
    eye = jnp.where(ri == ci, 1.0, 0.0)
    tri = _chunk_tri(tb)

    ba = ba_ref[...]
    nh = 2 * pp
    head_lanes = [slice(hh * hd, (hh + 1) * hd) for hh in range(nh)]
    vh = [v[:, ls] for ls in head_lanes]
    q_ss = [jnp.sum(q[:, ls] * q[:, ls], axis=-1, keepdims=True) for ls in head_lanes]
    k_ss = [jnp.sum(k[:, ls] * k[:, ls], axis=-1, keepdims=True) for ls in head_lanes]
    b_raw = [ba[:, hh:hh + 1] for hh in range(nh)]
    a_raw = [ba[:, C_HEADS + hh:C_HEADS + hh + 1] for hh in range(nh)]
    qh = [q[:, ls] / jnp.maximum(jnp.sqrt(ss), 1e-12) * (hd ** -0.5) for ls, ss in zip(head_lanes, q_ss)]
    kh = [k[:, ls] / jnp.maximum(jnp.sqrt(ss), 1e-12) for ls, ss in zip(head_lanes, k_ss)]
    beta = [_sigmoid(b) for b in b_raw]
    glog = [-jnp.exp(prm[0:1, ls]) * _softplus(al + prm[1:2, ls]) for ls, al in zip(head_lanes, a_raw)]
    gam = [_dot_sel_l(tri, gl) for gl in glog]

    nchunk = tb // CHUNK
    inst = [(c, pr) for c in range(nchunk) for pr in range(pp)]
    rows_of = lambda c: slice(c * CHUNK, (c + 1) * CHUNK)

    def stack(xs, cp):
        c, pr = cp
        return jnp.concatenate([xs[2 * pr][rows_of(c)], xs[2 * pr + 1][rows_of(c)]], axis=0)

    k_s = {cp: stack(kh, cp) for cp in inst}
    q_s = {cp: stack(qh, cp) for cp in inst}
    beta_s = {cp: stack(beta, cp) for cp in inst}
    gam_s = {cp: stack(gam, cp) for cp in inst}
    kq = {cp: _dot_nt(jnp.concatenate([k_s[cp], q_s[cp]], axis=0), k_s[cp]) for cp in inst}
    dm = {cp: jnp.exp(jnp.where(incl, gam_s[cp] - gam_s[cp].T, -1e30)) for cp in inst}
    n_mats = [jnp.where(strict, -(beta_s[cp] * kq[cp][0:LANES] * dm[cp]), 0.0) for cp in inst]
    t_inv = dict(zip(inst, _neumann_inverse_many(n_mats, eye)))
    e_gam = {cp: jnp.exp(gam_s[cp]) for cp in inst}
    sol = {cp: _dot(t_inv[cp], jnp.concatenate([stack(vh, cp) * beta_s[cp],
                                                k_s[cp] * (beta_s[cp] * e_gam[cp])], axis=1)) for cp in inst}

    states = [s_ref[hh] for hh in range(nh)]
    for c in range(nchunk):
        sl = rows_of(c)
        ws = []
        for hh in range(nh):
            cp, hs = (c, hh // 2), slice((hh % 2) * CHUNK, (hh % 2 + 1) * CHUNK)
            qg = (q_s[cp] * e_gam[cp])[hs]
            ws.append(_dot(jnp.concatenate([sol[cp][hs, hd:], qg], axis=0), states[hh]))
        v_new = [sol[c, hh // 2][(hh % 2) * CHUNK:(hh % 2 + 1) * CHUNK, 0:hd] - ws[hh][0:CHUNK]
                 for hh in range(nh)]
        for hh in range(nh):
            g_h = gam[hh][sl]
            g_last = g_h[CHUNK - 1:CHUNK, :]
            states[hh] = states[hh] * jnp.exp(g_last) + _dot_tn(kh[hh][sl] * jnp.exp(g_last - g_h), v_new[hh])
        for pr in range(pp):
            h0 = 2 * pr
            attn = kq[c, pr][LANES:] * dm[c, pr]
            o_s = (jnp.concatenate([ws[h0][CHUNK:], ws[h0 + 1][CHUNK:]], axis=0)
                   + _dot(attn, jnp.concatenate([v_new[h0], v_new[h0 + 1]], axis=0)))
            o_ref[sl, h0 * hd:(h0 + 1) * hd] = o_s[0:CHUNK]
            o_ref[sl, (h0 + 1) * hd:(h0 + 2) * hd] = o_s[CHUNK:]
    for hh in range(nh):
        s_ref[hh] = states[hh]

    gate = gate_ref[...]
    o_h = [o_ref[:, ls] for ls in head_lanes]
    o_ms = [jnp.mean(o * o, axis=-1, keepdims=True) for o in o_h]
    for o, ms, ls in zip(o_h, o_ms, head_lanes):
        gt = gate[:, ls]
        o_ref[:, ls] = (o * lax.rsqrt(ms + NORM_EPS) * prm[2:3, ls]) * (gt * _sigmoid(gt))


def gated_deltanet(zc, conv_w, prm, *, batch, seq, tb=256, pp=4):
    assert 2 * pp == C_HEADS, "the kernel indexes the per-head beta/alpha columns statically"
    nt = seq // tb
    npair = C_HEADS // (2 * pp)
    wblk = 2 * pp * C_HEAD_K
    row = lambda b, p, i: b * nt + i
    in_specs = [
        pl.BlockSpec((tb, wblk), lambda b, p, i: (row(b, p, i), p)),
        pl.BlockSpec((tb, wblk), lambda b, p, i: (row(b, p, i), npair + p)),
        pl.BlockSpec((tb, wblk), lambda b, p, i: (row(b, p, i), 2 * npair + p)),
        pl.BlockSpec((tb, wblk), lambda b, p, i: (row(b, p, i), 3 * npair + p)),
        pl.BlockSpec((tb, LANES), lambda b, p, i: (row(b, p, i), ZC_BA // LANES)),
        pl.BlockSpec((3, C_CONV, wblk), lambda b, p, i: (0, 0, p)),
        pl.BlockSpec((SUBLANES, wblk), lambda b, p, i: (0, p)),
    ]
    return pl.pallas_call(
        functools.partial(_gdn_kernel, tb=tb, pp=pp),
        grid=(batch, npair, nt),
        in_specs=in_specs,
        out_specs=pl.BlockSpec((tb, wblk), lambda b, p, i: (row(b, p, i), p)),
        out_shape=jax.ShapeDtypeStruct((batch * seq, C_VW), F32),
        scratch_shapes=[
            pltpu.VMEM((2 * pp, C_HEAD_K, C_HEAD_V), F32),
            pltpu.VMEM((tb + SUBLANES, wblk), F32),
            pltpu.VMEM((tb + SUBLANES, wblk), F32),
            pltpu.VMEM((tb + SUBLANES, wblk), F32),
        ],
        compiler_params=_cparams(("parallel", "parallel", "arbitrary")),
        name="gated_deltanet",
    )(zc, zc, zc, zc, zc, conv_w, prm)


def _merge_kernel(ya_ref, o0_ref, o1_ref, o2_ref, l0_ref, l1_ref, l2_ref, yc_ref, ga_ref, gb_ref, gc_ref,
                  pa_ref, pb_ref, pc_ref, out_ref):
    l0, l1, l2 = l0_ref[...], l1_ref[...], l2_ref[...]
    m = jnp.maximum(jnp.maximum(l0, l1), l2)
    w0, w1, w2 = jnp.exp(l0 - m), jnp.exp(l1 - m), jnp.exp(l2 - m)
    yb = (w0 * o0_ref[...] + w1 * o1_ref[...] + w2 * o2_ref[...]) / (w0 + w1 + w2)
    merged = (_sigmoid(ga_ref[...]) * _dot(ya_ref[...], pa_ref[...])
              + _sigmoid(gb_ref[...]) * _dot(yb, pb_ref[...])
              + _sigmoid(gc_ref[...]) * _dot(yc_ref[...], pc_ref[...]))
    out_ref[...] = merged.astype(out_ref.dtype)


def merge_mixers(ya, attn, yc, zg, pa, pb, pc, tm=256):
    m = ya.shape[0]
    d = D_MODEL
    rows = lambda w: pl.BlockSpec((tm, w), lambda i: (i, 0))
    const = lambda a: pl.BlockSpec(a.shape, lambda i: (0, 0), pipeline_mode=pl.Buffered(1))
    (o0, l0), (o1, l1), (o2, l2) = attn
    in_specs = ([rows(A_WIDTH)] + [rows(B_OUT)] * 6 + [rows(C_VW)]
                + [pl.BlockSpec((tm, d), lambda i, j=j: (i, j)) for j in range(3)]
                + [const(pa), const(pb), const(pc)])
    return pl.pallas_call(
        _merge_kernel,
        grid=(m // tm,),
        in_specs=in_specs,
        out_specs=pl.BlockSpec((tm, d), lambda i: (i, 0)),
        out_shape=jax.ShapeDtypeStruct((m, d), BF16),
        compiler_params=_cparams(("parallel",)),
        name="merge_mixers",
    )(ya, o0, o1, o2, l0, l1, l2, yc, zg, zg, zg, pa, pb, pc)


def _ffn_up_act_kernel(x_ref, wg_ref, wv_ref, cw_ref, o_ref, buf_ref, halo_ref, *, tm, tf, blocks_per_seq):
    i = pl.program_id(0)
    j = pl.program_id(1)
    h = SUBLANES
    first = (i % blocks_per_seq) == 0

    @pl.when(first)
    def _():
        buf_ref[0:h, :] = jnp.zeros((h, 2 * tf), F32)

    @pl.when(jnp.logical_not(first))
    def _():
        buf_ref[0:h, :] = halo_ref[j]

    cw = cw_ref[...]

    def project(s):
        cs = slice(s * LANES, (s + 1) * LANES)
        w = jnp.concatenate([wg_ref[:, cs], wv_ref[:, cs]], axis=1)
        return jnp.dot(x_ref[...], w, preferred_element_type=F32)

    def conv(u, cols):
        buf_ref[h:h + tm, cols] = u
        halo_ref[j, :, cols] = u[tm - h:tm, :]
        acc = u * cw[FFN_CONV - 1:FFN_CONV, cols]
        for t in range(FFN_CONV - 1):
            off = h - (FFN_CONV - 1) + t
            acc = acc + buf_ref[off:off + tm, cols] * cw[t:t + 1, cols]
        return acc

    def gate(s, u):
        cg = conv(u[:, 0:LANES], slice(s * LANES, (s + 1) * LANES))
        cv = conv(u[:, LANES:], slice(tf + s * LANES, tf + (s + 1) * LANES))
        o_ref[:, s * LANES:(s + 1) * LANES] = ((cg * _sigmoid(cg)) * cv).astype(o_ref.dtype)

    nsub = tf // LANES
    pending = project(0)
    for s in range(nsub):
        nxt = project(s + 1) if s + 1 < nsub else None
        gate(s, pending)
        pending = nxt


def ffn_up_act(h2, w_up, layer, conv_w, *, seq, tm=1024, tf=FFN_TF):
    m, d = h2.shape
    tm = min(tm, seq)
    nf = D_FF // tf
    return pl.pallas_call(
        functools.partial(_ffn_up_act_kernel, tm=tm, tf=tf, blocks_per_seq=seq // tm),
        grid=(m // tm, nf),
        in_specs=[
            pl.BlockSpec((tm, d), lambda i, j: (i, 0)),
            pl.BlockSpec((d, tf), lambda i, j: (layer, j)),
            pl.BlockSpec((d, tf), lambda i, j: (layer, nf + j)),
            pl.BlockSpec((None, FFN_CONV, 2 * tf), lambda i, j: (j, 0, 0)),
        ],
        out_specs=pl.BlockSpec((tm, tf), lambda i, j: (i, j)),
        out_shape=jax.ShapeDtypeStruct((m, D_FF), BF16),
        scratch_shapes=[
            pltpu.VMEM((tm + SUBLANES, 2 * tf), F32),
            pltpu.VMEM((nf, SUBLANES, 2 * tf), F32),
        ],
        compiler_params=_cparams(("arbitrary", "arbitrary")),
        name="ffn_up_act",
    )(h2, w_up, w_up, conv_w)


def _prep_w_in_kernel(w_ref, za_ref, zb_ref, zc_ref, zg_ref):
    tr = w_ref.shape[0]
    aw = 3 * A_WIDTH
    lora = A_DECAY_LORA + A_ICLR_LORA

    def cols(lo, hi):
        lo_al = lo - lo % LANES
        hi_al = min(-(-hi // LANES) * LANES, w_ref.shape[1])
        return w_ref[:, lo_al:hi_al][:, lo - lo_al:hi - lo_al]

    def pad_to(x, width):
        return jnp.concatenate([x, jnp.zeros((tr, width - x.shape[1]), x.dtype)], axis=1)

    za_ref[:, 0:aw] = cols(0, aw).astype(BF16)
    za_ref[:, ZA_GD:ZA_WA] = pad_to(cols(aw + lora, A_IN), ZA_WA - ZA_GD).astype(BF16)
    za_ref[:, ZA_WA:ZA_W] = pad_to(cols(aw, aw + lora), ZA_W - ZA_WA).astype(BF16)
    zb_ref[...] = cols(A_IN, A_IN + B_IN).astype(BF16)
    c0 = A_IN + B_IN
    qkv = 2 * C_KW + C_VW
    zc_ref[:, 0:qkv] = cols(c0, c0 + qkv).astype(BF16)
    zc_ref[:, qkv:ZC_BA] = cols(c0 + qkv + 2 * C_HEADS, c0 + C_IN).astype(BF16)
    zc_ref[:, ZC_BA:ZC_W] = pad_to(cols(c0 + qkv, c0 + qkv + 2 * C_HEADS), ZC_W - ZC_BA).astype(BF16)
    zg_ref[...] = cols(c0 + C_IN, w_ref.shape[1]).astype(BF16)


def prep_w_in(w_in, l, tr=128):
    depth, d, n_in = w_in.shape
    widths = (ZA_W, B_IN, ZC_W, 3 * D_MODEL)
    nblk = d // tr
    return pl.pallas_call(
        _prep_w_in_kernel,
        grid=(nblk,),
        in_specs=[pl.BlockSpec((tr, n_in), lambda i: (l * nblk + i, 0))],
        out_specs=[pl.BlockSpec((tr, w), lambda i: (i, 0)) for w in widths],
        out_shape=[jax.ShapeDtypeStruct((d, w), BF16) for w in widths],
        compiler_params=_cparams(("parallel",)),
        name="prep_w_in",
    )(w_in.reshape(depth * d, n_in))


def _pad_cols(w, width):
    return jnp.pad(w, ((0, 0), (0, width - w.shape[1])))


def _pad_rows(w, rows, at=0):
    return jnp.pad(w, ((at, rows - at - w.shape[0]), (0, 0)))


def _layer_params(l, p):
    aw = 3 * A_WIDTH
    mu = p["rwkv_mu"][l]
    rows = [p["rwkv_w0"][l], p["rwkv_a0"][l], p["rwkv_k_k"][l], p["rwkv_k_a"][l], p["rwkv_r_k"][l].reshape(-1),
            p["rwkv_ln_w"][l], p["rwkv_ln_b"][l],
            (p["rwkv_v0"][l - 1] if l > 0 else jnp.zeros((A_WIDTH,), F32)),
            mu[:A_WIDTH], mu[A_WIDTH:2 * A_WIDTH], mu[2 * A_WIDTH:aw]]
    rwkv_prm = jnp.pad(jnp.stack(rows), ((0, 16 - len(rows)), (0, 0)))
    mu2 = jnp.concatenate([_pad_cols(mu[None, aw + 128:], 256), mu[None, aw:aw + 128]], axis=1)
    v2 = (p["rwkv_v2"][l - 1] if l > 0 else jnp.zeros((A_VRES_LORA, A_WIDTH), F32))
    bcast = lambda t: jnp.repeat(t, C_HEAD_K)
    gdn_prm = jnp.pad(jnp.stack([bcast(p["gdn_A_log"][l]), bcast(p["gdn_dt_bias"][l]),
                                 jnp.tile(p["gdn_norm"][l], C_HEADS)]), ((0, SUBLANES - 3), (0, 0)))
    fc = p["ffn_conv"][l]
    return dict(
        v1p=(_pad_cols(p["rwkv_v1"][l - 1], LANES).astype(BF16) if l > 0 else None),
        rwkv_prm=rwkv_prm, mu2=mu2,
        w2p=_pad_rows(p["rwkv_w2"][l], 128, 0).astype(BF16),
        a2p=_pad_rows(p["rwkv_a2"][l], 128, A_DECAY_LORA).astype(BF16),
        g2p=_pad_rows(p["rwkv_g2"][l], 256, 0).astype(BF16),
        v2p=_pad_rows(v2, 128, 0).astype(BF16),
        gdn_conv=p["gdn_conv"][l].reshape(C_CONV, 3, C_KW).transpose(1, 0, 2),
        gdn_prm=gdn_prm,
        pa=p["proj_a"][l].astype(BF16), pb=p["proj_b"][l].astype(BF16), pc=p["proj_c"][l].astype(BF16),
        ffn_conv=fc.reshape(FFN_CONV, 2, D_FF // FFN_TF, FFN_TF).transpose(2, 0, 1, 3).reshape(
            D_FF // FFN_TF, FFN_CONV, 2 * FFN_TF),
    )


def _forward(x, p):
    batch, seq, d = x.shape
    m = batch * seq
    xf = x.reshape(m, d)
    tab = rope_tables(seq)
    v_first = None
    stack_rows = lambda w: w.astype(BF16).reshape(w.shape[0] * w.shape[1], w.shape[2])
    w_out_all, ffn_up_all, ffn_down_all = stack_rows(p["w_out"]), stack_rows(p["ffn_up"]), stack_rows(p["ffn_down"])
    h = rmsnorm(xf, p["attn_norm"][0], BF16)
    for l in range(DEPTH):
        lp = _layer_params(l, p)
        w_za, w_zb, w_zc, w_zg = prep_w_in(p["w_in"], l)
        za = matmul(h, w_za, tn=512, tm=2048, name="in_proj_a")
        zb = matmul(h, w_zb, tn=512, tm=2048, name="in_proj_b")
        zc = matmul(h, w_zc, tn=1408, tm=1024, name="in_proj_c")
        zg = matmul(h, w_zg, tn=1024, tm=2048, name="in_proj_g")
        lv = matmul(h, lp["v1p"], tn=LANES, tm=2048, name="vres_lora") if l > 0 else None
        ya, v_l = rwkv_mix(za, lv, v_first, lp["rwkv_prm"], lp["mu2"], lp["w2p"], lp["a2p"], lp["g2p"], lp["v2p"],
                           batch=batch, seq=seq)
        if l == 0:
            v_first = v_l
        attn = [dilated_attention_group(zb, tab, gi, batch=batch, seq=seq) for gi in range(B_GROUPS)]
        yc = gated_deltanet(zc, lp["gdn_conv"], lp["gdn_prm"], batch=batch, seq=seq)
        merged = merge_mixers(ya, attn, yc, zg, lp["pa"], lp["pb"], lp["pc"])
        xf, h2 = matmul_res_norm(merged, w_out_all, l, xf, p["ffn_norm"][l], tm=512, norm_dtype=BF16,
                                 emit_sum=True, name="out_proj_norm")
        act = ffn_up_act(h2, ffn_up_all, l, lp["ffn_conv"], seq=seq)
        if l + 1 < DEPTH:
            xf, h = matmul_res_norm(act, ffn_down_all, l, xf, p["attn_norm"][l + 1], tm=256, norm_dtype=BF16,
                                    emit_sum=True, name="ffn_down_norm")
        else:
            out = matmul_res_norm(act, ffn_down_all, l, xf, p["final_norm"], tm=256, norm_dtype=F32,
                                  emit_sum=False, name="ffn_down_norm")
    return out.reshape(batch, seq, d)


def kernel(x, attn_norm, w_in, rwkv_mu, rwkv_w0, rwkv_w2, rwkv_a0, rwkv_a2, rwkv_g2, rwkv_k_k, rwkv_k_a, rwkv_r_k, rwkv_ln_w, rwkv_ln_b, rwkv_v0, rwkv_v1, rwkv_v2, gdn_conv, gdn_A_log, gdn_dt_bias, gdn_norm, proj_a, proj_b, proj_c, w_out, ffn_norm, ffn_up, ffn_conv, ffn_down, final_norm):
    params = dict(
        attn_norm=attn_norm, w_in=w_in, rwkv_mu=rwkv_mu, rwkv_w0=rwkv_w0, rwkv_w2=rwkv_w2, rwkv_a0=rwkv_a0,
        rwkv_a2=rwkv_a2, rwkv_g2=rwkv_g2, rwkv_k_k=rwkv_k_k, rwkv_k_a=rwkv_k_a, rwkv_r_k=rwkv_r_k,
        rwkv_ln_w=rwkv_ln_w, rwkv_ln_b=rwkv_ln_b, rwkv_v0=rwkv_v0, rwkv_v1=rwkv_v1, rwkv_v2=rwkv_v2,
        gdn_conv=gdn_conv, gdn_A_log=gdn_A_log, gdn_dt_bias=gdn_dt_bias, gdn_norm=gdn_norm, proj_a=proj_a,
        proj_b=proj_b, proj_c=proj_c, w_out=w_out, ffn_norm=ffn_norm, ffn_up=ffn_up, ffn_conv=ffn_conv,
        ffn_down=ffn_down, final_norm=final_norm)
    return _forward(x, params)
```

```python
import functools

import jax
import jax.numpy as jnp
from jax import lax
from jax.experimental import pallas as pl
from jax.experimental.pallas import tpu as pltpu

F32 = jnp.float32
BF16 = jnp.bfloat16

D_MODEL = 2048
DEPTH = 2
NORM_EPS = 1e-6

A_HEADS, A_HEAD = 16, 64
A_WIDTH = A_HEADS * A_HEAD
A_DECAY_LORA, A_ICLR_LORA, A_GATE_LORA, A_VRES_LORA = 64, 64, 160, 32
A_GN_EPS = 64e-5
A_IN = 3 * A_WIDTH + A_DECAY_LORA + A_ICLR_LORA + A_GATE_LORA

B_PAIRS = ((128, 1), (512, 4), (2048, 16))
B_GROUPS = 3
B_HEADS_PER_GROUP, B_HEAD = 4, 128
B_WIDTH = B_GROUPS * B_HEADS_PER_GROUP * B_HEAD
B_OUT = B_HEADS_PER_GROUP * B_HEAD
B_IN = 3 * B_WIDTH
ROPE_THETA = 500000.0
ROPE_DIM = B_HEAD // 4

C_HEADS, C_HEAD_K, C_HEAD_V = 8, 128, 128
C_KW = C_HEADS * C_HEAD_K
C_VW = C_HEADS * C_HEAD_V
C_CONV = 4
C_IN = 2 * C_KW + C_VW + 2 * C_HEADS + C_VW

D_FF = 5632
FFN_CONV = 3

LANES = 128
SUBLANES = 8
CHUNK = 64
FFN_TF = 1408
VMEM_LIMIT = 56 * 1024 * 1024

ZA_GD = 3 * A_WIDTH
ZA_WA = ZA_GD + 256
ZA_W = ZA_WA + 256
ZC_BA = 3 * C_KW + C_VW
ZC_W = ZC_BA + 128


def _cparams(sem):
    return pltpu.CompilerParams(dimension_semantics=sem, vmem_limit_bytes=VMEM_LIMIT)


def _dot(a, b):
    return jnp.dot(a.astype(BF16), b.astype(BF16), preferred_element_type=F32)


def _dot_nt(a, b):
    return lax.dot_general(a.astype(BF16), b.astype(BF16), (((1,), (1,)), ((), ())), preferred_element_type=F32)


def _dot_tn(a, b):
    return lax.dot_general(a.astype(BF16), b.astype(BF16), (((0,), (0,)), ((), ())), preferred_element_type=F32)


def _split(a):
    hi = a.astype(BF16)
    lo = (a - hi.astype(F32)).astype(BF16)
    return hi, lo


def _dot_sel_l(sel, a):
    hi, lo = _split(a)
    return (jnp.dot(sel, hi, preferred_element_type=F32) + jnp.dot(sel, lo, preferred_element_type=F32))


def _sigmoid(x):
    return 1.0 / (1.0 + jnp.exp(-x))


def _softplus(x):
    return jnp.maximum(x, 0.0) + jnp.log(1.0 + jnp.exp(-jnp.abs(x)))


def _iota2(shape, axis):
    return lax.broadcasted_iota(jnp.int32, shape, axis)


def _chunk_of(idx):
    return jnp.right_shift(idx, CHUNK.bit_length() - 1)


def _chunk_tri(n):
    ri, ci = _iota2((n, n), 0), _iota2((n, n), 1)
    return jnp.where((_chunk_of(ri) == _chunk_of(ci)) & (ri >= ci), 1.0, 0.0).astype(BF16)


def _neumann_inverse_many(n_mats, eye):
    ps = [eye + n for n in n_mats]
    qs = [_dot(n, n) for n in n_mats]
    levels = CHUNK.bit_length() - 2
    for lvl in range(levels):
        if lvl == levels - 1:
            ps = [p + _dot(p, q) for p, q in zip(ps, qs)]
        else:
            prods = [_dot(jnp.concatenate([q, p], axis=0), q) for p, q in zip(ps, qs)]
            ps = [p + pr[LANES:] for p, pr in zip(ps, prods)]
            qs = [pr[0:LANES] for pr in prods]
    return ps


def _neumann_inverse(n_mat, eye):
    return _neumann_inverse_many([n_mat], eye)[0]


def _rmsnorm_kernel(x_ref, g_ref, o_ref):
    x = x_ref[...]
    ms = jnp.mean(x * x, axis=-1, keepdims=True)
    o_ref[...] = ((x * lax.rsqrt(ms + NORM_EPS)) * g_ref[...]).astype(o_ref.dtype)


def rmsnorm(x, g, out_dtype, tm=512):
    m, d = x.shape
    return pl.pallas_call(
        _rmsnorm_kernel,
        grid=(m // tm,),
        in_specs=[pl.BlockSpec((tm, d), lambda i: (i, 0)), pl.BlockSpec((1, d), lambda i: (0, 0))],
        out_specs=pl.BlockSpec((tm, d), lambda i: (i, 0)),
        out_shape=jax.ShapeDtypeStruct((m, d), out_dtype),
        compiler_params=_cparams(("parallel",)),
        name="rmsnorm",
    )(x, g.reshape(1, d))


def _matmul_kernel(x_ref, w_ref, o_ref):
    o_ref[...] = jnp.dot(x_ref[...], w_ref[...], preferred_element_type=F32).astype(o_ref.dtype)


def _matmul_nt_kernel(x_ref, wt_ref, o_ref):
    o_ref[...] = lax.dot_general(x_ref[...], wt_ref[...], (((1,), (1,)), ((), ())), preferred_element_type=F32)


def matmul(x, w, tn, tm=1024, name="matmul"):
    m, k = x.shape
    n = w.shape[1]
    tm = min(tm, m)
    return pl.pallas_call(
        _matmul_kernel,
        grid=(m // tm, n // tn),
        in_specs=[pl.BlockSpec((tm, k), lambda i, j: (i, 0)), pl.BlockSpec((k, tn), lambda i, j: (0, j))],
        out_specs=pl.BlockSpec((tm, tn), lambda i, j: (i, j)),
        out_shape=jax.ShapeDtypeStruct((m, n), F32),
        compiler_params=_cparams(("parallel", "parallel")),
        name=name,
    )(x, w)


def _matmul_res_norm_kernel(x_ref, w_ref, r_ref, g_ref, *out_refs):
    y = r_ref[...] + jnp.dot(x_ref[...], w_ref[...], preferred_element_type=F32)
    ms = jnp.mean(y * y, axis=-1, keepdims=True)
    hn = (y * lax.rsqrt(ms + NORM_EPS)) * g_ref[...]
    if len(out_refs) == 2:
        out_refs[0][...] = y
    out_refs[-1][...] = hn.astype(out_refs[-1].dtype)


def matmul_res_norm(x, w, layer, residual, gain, *, tm, norm_dtype, emit_sum, name):
    m, k = x.shape
    n = w.shape[1]
    rows = lambda dt: (pl.BlockSpec((tm, n), lambda i: (i, 0)), jax.ShapeDtypeStruct((m, n), dt))
    outs = ([rows(F32)] if emit_sum else []) + [rows(norm_dtype)]
    res = pl.pallas_call(
        _matmul_res_norm_kernel,
        grid=(m // tm,),
        in_specs=[pl.BlockSpec((tm, k), lambda i: (i, 0)),
                  pl.BlockSpec((k, n), lambda i: (layer, 0), pipeline_mode=pl.Buffered(1)),
                  pl.BlockSpec((tm, n), lambda i: (i, 0)),
                  pl.BlockSpec((1, n), lambda i: (0, 0))],
        out_specs=[o[0] for o in outs],
        out_shape=[o[1] for o in outs],
        compiler_params=_cparams(("parallel",)),
        name=name,
    )(x, w, residual, gain.reshape(1, n))
    return tuple(res) if emit_sum else res[0]


def _rwkv_kernel(r_ref, k_ref, v_ref, gd_ref, wa_ref, lv_ref, vf_ref, p_ref, mu2_ref, w2_ref, a2_ref, g2_ref,
                 v2_ref, y_ref, vout_ref, s_ref, br_ref, bk_ref, bv_ref, bgd_ref, bwa_ref, *, tb, pp, has_vmix):
    i = pl.program_id(2)
    h = SUBLANES

    @pl.when(i == 0)
    def _():
        s_ref[...] = jnp.zeros_like(s_ref)
        for b in (br_ref, bk_ref, bv_ref, bgd_ref, bwa_ref):
            b[0:h, :] = jnp.zeros((h, b.shape[1]), F32)

    prm = p_ref[...]
    w0, a0, k_k, k_a, r_k, ln_w, ln_b, v0 = [prm[j:j + 1] for j in range(8)]
    mu_r, mu_k, mu_v = prm[8:9], prm[9:10], prm[10:11]
    mu2 = mu2_ref[...]
    mu_gd, mu_wa = mu2[:, 0:256], mu2[:, 256:384]

    def shifted_mix(x_ref, buf_ref, mu):
        x = x_ref[...]
        buf_ref[h:h + tb, :] = x
        xs = buf_ref[h - 1:h - 1 + tb, :]
        buf_ref[0:h, :] = x[tb - h:tb, :]
        return x + (xs - x) * mu

    r = shifted_mix(r_ref, br_ref, mu_r)
    k = shifted_mix(k_ref, bk_ref, mu_k)
    v = shifted_mix(v_ref, bv_ref, mu_v)
    gd = shifted_mix(gd_ref, bgd_ref, mu_gd)
    wa = shifted_mix(wa_ref, bwa_ref, mu_wa)

    lane = _iota2((1, LANES), 1)
    m0 = jnp.where(lane < A_HEAD, 1.0, 0.0)
    m1 = 1.0 - m0
    ri = _iota2((LANES, LANES), 0)
    ci = _iota2((LANES, LANES), 1)
    same = _chunk_of(ri) == _chunk_of(ci)
    strict = same & (ri > ci)
    incl = same & (ri >= ci)
    eye = jnp.where(ri == ci, 1.0, 0.0)
    tri = _chunk_tri(tb)
    pair_lanes = [slice(q * LANES, (q + 1) * LANES) for q in range(pp)]

    def head_sum(x):
        outs = []
        for ls in pair_lanes:
            s0 = jnp.sum(x[:, ls] * m0, axis=-1, keepdims=True)
            s1 = jnp.sum(x[:, ls] * m1, axis=-1, keepdims=True)
            outs.append(jnp.where(lane < A_HEAD, s0, s1))
        return jnp.concatenate(outs, axis=1)

    lw = -jnp.exp(-0.5) * _sigmoid(w0 + _dot(jnp.tanh(wa), w2_ref[...]))
    a = _sigmoid(a0 + _dot(wa, a2_ref[...]))
    g = _dot(_sigmoid(gd), g2_ref[...])
    kk = k * k_k
    kk = kk / jnp.maximum(jnp.sqrt(head_sum(kk * kk)), 1e-12)
    k = k * (1.0 + (a - 1.0) * k_a)
    if has_vmix:
        v_mix = _sigmoid(v0 + _dot(lv_ref[...], v2_ref[...]))
        v = v + (vf_ref[...] - v) * v_mix
    vout_ref[...] = v

    cum = _dot_sel_l(tri, lw)
    e_pos = jnp.exp(cum)
    e_neg = jnp.exp(-cum)
    r_t = r * e_pos
    a_t = -kk * jnp.exp(cum - lw)
    b_t = (kk * a) * e_neg
    k_t = k * e_neg

    def stack_masked(x):
        return jnp.concatenate([x * m0, x * m1], axis=0)

    def stack_dup(x):
        return jnp.concatenate([x, x], axis=0)

    kb = kk * a
    nchunk = tb // CHUNK
    inst = [(c, q) for c in range(nchunk) for q in range(pp)]
    rows_of = lambda c: slice(c * CHUNK, (c + 1) * CHUNK)
    last_of = lambda c: slice((c + 1) * CHUNK - 1, (c + 1) * CHUNK)
    a_s = {cq: stack_masked(a_t[rows_of(cq[0]), pair_lanes[cq[1]]]) for cq in inst}
    r_s = {cq: stack_masked(r_t[rows_of(cq[0]), pair_lanes[cq[1]]]) for cq in inst}
    v_s = {cq: stack_masked(v[rows_of(cq[0]), pair_lanes[cq[1]]]) for cq in inst}
    gm = {}
    for c, q in inst:
        sl, ls = rows_of(c), pair_lanes[q]
        right = jnp.concatenate([stack_dup(b_t[sl, ls]), stack_dup(k_t[sl, ls])], axis=0)
        gm[c, q] = _dot_nt(jnp.concatenate([a_s[c, q], r_s[c, q]], axis=0), right)
    n_ab = [jnp.where(strict, gm[cq][0:LANES, 0:LANES], 0.0) for cq in inst]
    t_inv = dict(zip(inst, _neumann_inverse_many(n_ab, eye)))
    gv = {cq: _dot(jnp.where(strict, gm[cq][0:LANES, LANES:], 0.0), v_s[cq]) for cq in inst}
    tu = {cq: _dot(t_inv[cq], jnp.concatenate([a_s[cq], gv[cq]], axis=1)) for cq in inst}
    g_r = {cq: jnp.where(jnp.concatenate([incl, incl], axis=1), gm[cq][LANES:], 0.0) for cq in inst}

    states = [s_ref[q] for q in range(pp)]
    for c in range(nchunk):
        sl = rows_of(c)
        x0 = [_dot_nt(jnp.concatenate([tu[c, q][:, 0:LANES], r_s[c, q]], axis=0), states[q]) for q in range(pp)]
        uv = [jnp.concatenate([x0[q][0:LANES] + tu[c, q][:, LANES:], v_s[c, q]], axis=0) for q in range(pp)]
        for q, ls in enumerate(pair_lanes):
            cum_last = cum[last_of(c), ls]
            e_end = jnp.exp(cum_last - cum[sl, ls])
            bk_end = jnp.concatenate([stack_dup(kb[sl, ls] * e_end), stack_dup(k[sl, ls] * e_end)], axis=0)
            states[q] = jnp.where(same, states[q] * jnp.exp(cum_last) + _dot_tn(uv[q], bk_end), 0.0)
        for q, ls in enumerate(pair_lanes):
            y_s = x0[q][LANES:] + _dot(g_r[c, q], uv[q])
            y_ref[sl, ls] = y_s[0:CHUNK] + y_s[CHUNK:]
    for q in range(pp):
        s_ref[q] = states[q]

    y = y_ref[...]
    inv_n = 1.0 / A_HEAD
    mean = head_sum(y) * inv_n
    d = y - mean
    var = head_sum(d * d) * inv_n
    yn = d * lax.rsqrt(var + A_GN_EPS) * ln_w + ln_b
    bonus = head_sum(r * k * r_k) * v
    y_ref[...] = (yn + bonus) * g


def rwkv_mix(za, lv, v_first, prm, mu2, w2p, a2p, g2p, v2p, *, batch, seq, tb=256, pp=4):
    has_vmix = v_first is not None
    nt = seq // tb
    w = pp * LANES
    npair = A_WIDTH // w
    row = lambda b, p, i: b * nt + i
    if v_first is None:
        v_first = lv = za
    in_specs = [
        pl.BlockSpec((tb, w), lambda b, p, i: (row(b, p, i), p)),
        pl.BlockSpec((tb, w), lambda b, p, i: (row(b, p, i), npair + p)),
        pl.BlockSpec((tb, w), lambda b, p, i: (row(b, p, i), 2 * npair + p)),
        pl.BlockSpec((tb, 256), lambda b, p, i: (row(b, p, i), ZA_GD // 256)),
        pl.BlockSpec((tb, LANES), lambda b, p, i: (row(b, p, i), ZA_WA // LANES)),
        pl.BlockSpec((tb, LANES), lambda b, p, i: (row(b, p, i), 0)),
        pl.BlockSpec((tb, w), lambda b, p, i: (row(b, p, i), p)),
        pl.BlockSpec((16, w), lambda b, p, i: (0, p)),
        pl.BlockSpec((1, 384), lambda b, p, i: (0, 0)),
        pl.BlockSpec((LANES, w), lambda b, p, i: (0, p)),
        pl.BlockSpec((LANES, w), lambda b, p, i: (0, p)),
        pl.BlockSpec((256, w), lambda b, p, i: (0, p)),
        pl.BlockSpec((LANES, w), lambda b, p, i: (0, p)),
    ]
    out_spec = pl.BlockSpec((tb, w), lambda b, p, i: (row(b, p, i), p))
    out_sds = jax.ShapeDtypeStruct((batch * seq, A_WIDTH), F32)
    return pl.pallas_call(
        functools.partial(_rwkv_kernel, tb=tb, pp=pp, has_vmix=has_vmix),
        grid=(batch, npair, nt),
        in_specs=in_specs,
        out_specs=[out_spec, out_spec],
        out_shape=[out_sds, out_sds],
        scratch_shapes=[
            pltpu.VMEM((pp, LANES, LANES), F32),
            pltpu.VMEM((tb + SUBLANES, w), F32),
            pltpu.VMEM((tb + SUBLANES, w), F32),
            pltpu.VMEM((tb + SUBLANES, w), F32),
            pltpu.VMEM((tb + SUBLANES, 256), F32),
            pltpu.VMEM((tb + SUBLANES, LANES), F32),
        ],
        compiler_params=_cparams(("parallel", "parallel", "arbitrary")),
        name="rwkv7",
    )(za, za, za, za, za, lv, v_first, prm, mu2, w2p, a2p, g2p, v2p)


def _attn_kernel(*refs, tq, dil, nb, has_prev):
    if has_prev:
        (q_ref, kc_ref, kp_ref, vc_ref, vp_ref, cc_ref, sc_ref, nc_ref, cp_ref, sp_ref, np_ref,
         o_ref, lse_ref) = refs
    else:
        q_ref, kc_ref, vc_ref, cc_ref, sc_ref, nc_ref, o_ref, lse_ref = refs
    i = pl.program_id(1)
    scale = B_HEAD ** -0.5

    def rows(j, r):
        start = j * tq * dil + r
        return pl.ds(start, tq, stride=dil) if dil > 1 else pl.ds(start, tq)

    def rope(x, tabs):
        half = ROPE_DIM // 2
        return (x * tabs[0] + pltpu.roll(x, half, axis=1) * tabs[1]
                + pltpu.roll(x, LANES - half, axis=1) * tabs[2])

    if has_prev:
        ri, ci = _iota2((tq, 2 * tq), 0), _iota2((tq, 2 * tq), 1)
        cur_ok = (ci >= tq) & ((ci - tq) <= ri)
        no_prev = jnp.where(i > 0, 0, 2 * tq)
        valid_first = ((ci < tq) & (ci >= ri + no_prev)) | cur_ok
        valid_inner = ((ci < tq) & (ci >= ri)) | cur_ok
    else:
        causal = _iota2((tq, tq), 1) <= _iota2((tq, tq), 0)

    inst = [(r, j) for r in range(dil) for j in range(nb)]
    qs, ks, vs = {}, {}, {}
    for r in range(dil):
        if has_prev:
            rp = rows(0, r)
            ks[r, -1] = rope(kp_ref[rp, :], (cp_ref[rp, :], sp_ref[rp, :], np_ref[rp, :]))
            vs[r, -1] = vp_ref[rp, :]
        for j in range(nb):
            rw = rows(j, r)
            tabs = (cc_ref[rw, :], sc_ref[rw, :], nc_ref[rw, :])
            qs[r, j] = rope(q_ref[rw, :], tabs) * scale
            ks[r, j] = rope(kc_ref[rw, :], tabs)
            vs[r, j] = vc_ref[rw, :]
    if has_prev:
        s = {(r, j): jnp.where(valid_first if j == 0 else valid_inner,
                               _dot_nt(qs[r, j], jnp.concatenate([ks[r, j - 1], ks[r, j]], axis=0)), -1e30)
             for r, j in inst}
    else:
        s = {rj: jnp.where(causal, _dot_nt(qs[rj], ks[rj]), -1e30) for rj in inst}
    m = {rj: jnp.max(s[rj], axis=-1, keepdims=True) for rj in inst}
    p = {rj: jnp.exp(s[rj] - m[rj]) for rj in inst}
    den = {rj: jnp.sum(p[rj], axis=-1, keepdims=True) for rj in inst}
    if has_prev:
        num = {(r, j): _dot(p[r, j], jnp.concatenate([vs[r, j - 1], vs[r, j]], axis=0)) for r, j in inst}
    else:
        num = {rj: _dot(p[rj], vs[rj]) for rj in inst}
    for r, j in inst:
        o_ref[rows(j, r), :] = num[r, j] / den[r, j]
        lse_ref[rows(j, r), :] = jnp.broadcast_to(m[r, j] + jnp.log(den[r, j]), (tq, B_HEAD))


def dilated_attention_group(zb, rope_tabs, gi, *, batch, seq, tq=128, blocks_per_step=8):
    win, dil = B_PAIRS[gi]
    assert win // dil == tq
    nq = seq // (dil * tq)
    has_prev = nq > 1
    nb = min(nq, max(1, blocks_per_step // dil))
    rows_step = nb * tq * dil
    nsteps = seq // rows_step
    prev_rows = tq * dil
    nprev = seq // prev_rows
    nh = B_HEADS_PER_GROUP
    cur = lambda col: pl.BlockSpec((rows_step, B_HEAD), lambda b, i, hh: (b * nsteps + i, col * nh + hh))
    prv = lambda col: pl.BlockSpec(
        (prev_rows, B_HEAD), lambda b, i, hh: (b * nprev + jnp.maximum(i * nb - 1, 0), col * nh + hh))
    tab_cur = pl.BlockSpec((rows_step, LANES), lambda b, i, hh: (i, 0))
    tab_prv = pl.BlockSpec((prev_rows, LANES), lambda b, i, hh: (jnp.maximum(i * nb - 1, 0), 0))
    qc, kc, vc = gi, B_GROUPS + gi, 2 * B_GROUPS + gi
    if has_prev:
        in_specs = [cur(qc), cur(kc), prv(kc), cur(vc), prv(vc)] + [tab_cur] * 3 + [tab_prv] * 3
        args = (zb, zb, zb, zb, zb) + tuple(rope_tabs) * 2
    else:
        in_specs = [cur(qc), cur(kc), cur(vc)] + [tab_cur] * 3
        args = (zb, zb, zb) + tuple(rope_tabs)
    out_spec = pl.BlockSpec((rows_step, B_HEAD), lambda b, i, hh: (b * nsteps + i, hh))
    out_sds = jax.ShapeDtypeStruct((batch * seq, B_OUT), F32)
    return pl.pallas_call(
        functools.partial(_attn_kernel, tq=tq, dil=dil, nb=nb, has_prev=has_prev),
        grid=(batch, nsteps, nh),
        in_specs=in_specs,
        out_specs=[out_spec, out_spec],
        out_shape=[out_sds, out_sds],
        compiler_params=_cparams(("parallel", "parallel", "arbitrary")),
        name=f"dilated_attn_g{gi}",
    )(*args)


def rope_tables(seq):
    half = ROPE_DIM // 2
    inv = ROPE_THETA ** (-jnp.arange(half, dtype=F32) / half)
    ang = jnp.arange(seq, dtype=F32)[:, None] * inv[None, :]
    cos, sin = jnp.cos(ang), jnp.sin(ang)
    z = jnp.zeros((seq, LANES - ROPE_DIM), F32)
    zh = jnp.zeros((seq, half), F32)
    c_tab = jnp.concatenate([cos, cos, jnp.ones_like(z)], axis=1)
    s_pos = jnp.concatenate([zh, sin, z], axis=1)
    s_neg = jnp.concatenate([-sin, zh, z], axis=1)
    return c_tab, s_pos, s_neg


def _gdn_kernel(q_ref, k_ref, v_ref, gate_ref, ba_ref, cw_ref, p_ref, o_ref, s_ref, bq_ref, bk_ref, bv_ref,
                *, tb, pp):
    i = pl.program_id(2)
    h = SUBLANES
    hd = C_HEAD_K

    @pl.when(i == 0)
    def _():
        s_ref[...] = jnp.zeros_like(s_ref)
        for b in (bq_ref, bk_ref, bv_ref):
            b[0:h, :] = jnp.zeros((h, b.shape[1]), F32)

    cw = cw_ref[...]
    prm = p_ref[...]

    def conv_silu(x_ref, buf_ref, w):
        x = x_ref[...]
        buf_ref[h:h + tb, :] = x
        acc = x * w[C_CONV - 1:C_CONV]
        for j in range(C_CONV - 1):
            off = h - (C_CONV - 1) + j
            acc = acc + buf_ref[off:off + tb, :] * w[j:j + 1]
        buf_ref[0:h, :] = x[tb - h:tb, :]
        return acc * _sigmoid(acc)

    q = conv_silu(q_ref, bq_ref, cw[0])
    k = conv_silu(k_ref, bk_ref, cw[1])
    v = conv_silu(v_ref, bv_ref, cw[2])

    ri = _iota2((LANES, LANES), 0)
    ci = _iota2((LANES, LANES), 1)
    same = _chunk_of(ri) == _chunk_of(ci)
    strict = same & (ri > ci)
    incl = same & (ri >= ci)
    eye = jnp.where(ri == ci, 1.0, 0.0)
    tri = _chunk_tri(tb)

    ba = ba_ref[...]
    nh = 2 * pp
    head_lanes = [slice(hh * hd, (hh + 1) * hd) for hh in range(nh)]
    vh = [v[:, ls] for ls in head_lanes]
    q_ss = [jnp.sum(q[:, ls] * q[:, ls], axis=-1, keepdims=True) for ls in head_lanes]
    k_ss = [jnp.sum(k[:, ls] * k[:, ls], axis=-1, keepdims=True) for ls in head_lanes]
    b_raw = [ba[:, hh:hh + 1] for hh in range(nh)]
    a_raw = [ba[:, C_HEADS + hh:C_HEADS + hh + 1] for hh in range(nh)]
    qh = [q[:, ls] / jnp.maximum(jnp.sqrt(ss), 1e-12) * (hd ** -0.5) for ls, ss in zip(head_lanes, q_ss)]
    kh = [k[:, ls] / jnp.maximum(jnp.sqrt(ss), 1e-12) for ls, ss in zip(head_lanes, k_ss)]
    beta = [_sigmoid(b) for b in b_raw]
    glog = [-jnp.exp(prm[0:1, ls]) * _softplus(al + prm[1:2, ls]) for ls, al in zip(head_lanes, a_raw)]
    gam = [_dot_sel_l(tri, gl) for gl in glog]

    nchunk = tb // CHUNK
    inst = [(c, pr) for c in range(nchunk) for pr in range(pp)]
    rows_of = lambda c: slice(c * CHUNK, (c + 1) * CHUNK)

    def stack(xs, cp):
        c, pr = cp
        return jnp.concatenate([xs[2 * pr][rows_of(c)], xs[2 * pr + 1][rows_of(c)]], axis=0)

    k_s = {cp: stack(kh, cp) for cp in inst}
    q_s = {cp: stack(qh, cp) for cp in inst}
    beta_s = {cp: stack(beta, cp) for cp in inst}
    gam_s = {cp: stack(gam, cp) for cp in inst}
    kq = {cp: _dot_nt(jnp.concatenate([k_s[cp], q_s[cp]], axis=0), k_s[cp]) for cp in inst}
    dm = {cp: jnp.exp(jnp.where(incl, gam_s[cp] - gam_s[cp].T, -1e30)) for cp in inst}
    n_mats = [jnp.where(strict, -(beta_s[cp] * kq[cp][0:LANES] * dm[cp]), 0.0) for cp in inst]
    t_inv = dict(zip(inst, _neumann_inverse_many(n_mats, eye)))
    e_gam = {cp: jnp.exp(gam_s[cp]) for cp in inst}
    sol = {cp: _dot(t_inv[cp], jnp.concatenate([stack(vh, cp) * beta_s[cp],
                                                k_s[cp] * (beta_s[cp] * e_gam[cp])], axis=1)) for cp in inst}

    states = [s_ref[hh] for hh in range(nh)]
    for c in range(nchunk):
        sl = rows_of(c)
        ws = []
        for hh in range(nh):
            cp, hs = (c, hh // 2), slice((hh % 2) * CHUNK, (hh % 2 + 1) * CHUNK)
            qg = (q_s[cp] * e_gam[cp])[hs]
            ws.append(_dot(jnp.concatenate([sol[cp][hs, hd:], qg], axis=0), states[hh]))
        v_new = [sol[c, hh // 2][(hh % 2) * CHUNK:(hh % 2 + 1) * CHUNK, 0:hd] - ws[hh][0:CHUNK]
                 for hh in range(nh)]
        for hh in range(nh):
            g_h = gam[hh][sl]
            g_last = g_h[CHUNK - 1:CHUNK, :]
            states[hh] = states[hh] * jnp.exp(g_last) + _dot_tn(kh[hh][sl] * jnp.exp(g_last - g_h), v_new[hh])
        for pr in range(pp):
            h0 = 2 * pr
            attn = kq[c, pr][LANES:] * dm[c, pr]
            o_s = (jnp.concatenate([ws[h0][CHUNK:], ws[h0 + 1][CHUNK:]], axis=0)
                   + _dot(attn, jnp.concatenate([v_new[h0], v_new[h0 + 1]], axis=0)))
            o_ref[sl, h0 * hd:(h0 + 1) * hd] = o_s[0:CHUNK]
            o_ref[sl, (h0 + 1) * hd:(h0 + 2) * hd] = o_s[CHUNK:]
    for hh in range(nh):
        s_ref[hh] = states[hh]

    gate = gate_ref[...]
    o_h = [o_ref[:, ls] for ls in head_lanes]
    o_ms = [jnp.mean(o * o, axis=-1, keepdims=True) for o in o_h]
    for o, ms, ls in zip(o_h, o_ms, head_lanes):
        gt = gate[:, ls]
        o_ref[:, ls] = (o * lax.rsqrt(ms + NORM_EPS) * prm[2:3, ls]) * (gt * _sigmoid(gt))


def gated_deltanet(zc, conv_w, prm, *, batch, seq, tb=256, pp=4):
    assert 2 * pp == C_HEADS, "the kernel indexes the per-head beta/alpha columns statically"
    nt = seq // tb
    npair = C_HEADS // (2 * pp)
    wblk = 2 * pp * C_HEAD_K
    row = lambda b, p, i: b * nt + i
    in_specs = [
        pl.BlockSpec((tb, wblk), lambda b, p, i: (row(b, p, i), p)),
        pl.BlockSpec((tb, wblk), lambda b, p, i: (row(b, p, i), npair + p)),
        pl.BlockSpec((tb, wblk), lambda b, p, i: (row(b, p, i), 2 * npair + p)),
        pl.BlockSpec((tb, wblk), lambda b, p, i: (row(b, p, i), 3 * npair + p)),
        pl.BlockSpec((tb, LANES), lambda b, p, i: (row(b, p, i), ZC_BA // LANES)),
        pl.BlockSpec((3, C_CONV, wblk), lambda b, p, i: (0, 0, p)),
        pl.BlockSpec((SUBLANES, wblk), lambda b, p, i: (0, p)),
    ]
    return pl.pallas_call(
        functools.partial(_gdn_kernel, tb=tb, pp=pp),
        grid=(batch, npair, nt),
        in_specs=in_specs,
        out_specs=pl.BlockSpec((tb, wblk), lambda b, p, i: (row(b, p, i), p)),
        out_shape=jax.ShapeDtypeStruct((batch * seq, C_VW), F32),
        scratch_shapes=[
            pltpu.VMEM((2 * pp, C_HEAD_K, C_HEAD_V), F32),
            pltpu.VMEM((tb + SUBLANES, wblk), F32),
            pltpu.VMEM((tb + SUBLANES, wblk), F32),
            pltpu.VMEM((tb + SUBLANES, wblk), F32),
        ],
        compiler_params=_cparams(("parallel", "parallel", "arbitrary")),
        name="gated_deltanet",
    )(zc, zc, zc, zc, zc, conv_w, prm)


def _merge_kernel(ya_ref, o0_ref, o1_ref, o2_ref, l0_ref, l1_ref, l2_ref, yc_ref, ga_ref, gb_ref, gc_ref,
                  pa_ref, pb_ref, pc_ref, out_ref):
    l0, l1, l2 = l0_ref[...], l1_ref[...], l2_ref[...]
    m = jnp.maximum(jnp.maximum(l0, l1), l2)
    w0, w1, w2 = jnp.exp(l0 - m), jnp.exp(l1 - m), jnp.exp(l2 - m)
    yb = (w0 * o0_ref[...] + w1 * o1_ref[...] + w2 * o2_ref[...]) / (w0 + w1 + w2)
    merged = (_sigmoid(ga_ref[...]) * _dot(ya_ref[...], pa_ref[...])
              + _sigmoid(gb_ref[...]) * _dot(yb, pb_ref[...])
              + _sigmoid(gc_ref[...]) * _dot(yc_ref[...], pc_ref[...]))
    out_ref[...] = merged.astype(out_ref.dtype)


def merge_mixers(ya, attn, yc, zg, pa, pb, pc, tm=256):
    m = ya.shape[0]
    d = D_MODEL
    rows = lambda w: pl.BlockSpec((tm, w), lambda i: (i, 0))
    const = lambda a: pl.BlockSpec(a.shape, lambda i: (0, 0), pipeline_mode=pl.Buffered(1))
    (o0, l0), (o1, l1), (o2, l2) = attn
    in_specs = ([rows(A_WIDTH)] + [rows(B_OUT)] * 6 + [rows(C_VW)]
                + [pl.BlockSpec((tm, d), lambda i, j=j: (i, j)) for j in range(3)]
                + [const(pa), const(pb), const(pc)])
    return pl.pallas_call(
        _merge_kernel,
        grid=(m // tm,),
        in_specs=in_specs,
        out_specs=pl.BlockSpec((tm, d), lambda i: (i, 0)),
        out_shape=jax.ShapeDtypeStruct((m, d), BF16),
        compiler_params=_cparams(("parallel",)),
        name="merge_mixers",
    )(ya, o0, o1, o2, l0, l1, l2, yc, zg, zg, zg, pa, pb, pc)


def _ffn_up_act_kernel(x_ref, wg_ref, wv_ref, cw_ref, o_ref, buf_ref, halo_ref, *, tm, tf, blocks_per_seq):
    i = pl.program_id(0)
    j = pl.program_id(1)
    h = SUBLANES
    first = (i % blocks_per_seq) == 0

    @pl.when(first)
    def _():
        buf_ref[0:h, :] = jnp.zeros((h, 2 * tf), F32)

    @pl.when(jnp.logical_not(first))
    def _():
        buf_ref[0:h, :] = halo_ref[j]

    cw = cw_ref[...]

    def project(s):
        cs = slice(s * LANES, (s + 1) * LANES)
        w = jnp.concatenate([wg_ref[:, cs], wv_ref[:, cs]], axis=1)
        return jnp.dot(x_ref[...], w, preferred_element_type=F32)

    def conv(u, cols):
        buf_ref[h:h + tm, cols] = u
        halo_ref[j, :, cols] = u[tm - h:tm, :]
        acc = u * cw[FFN_CONV - 1:FFN_CONV, cols]
        for t in range(FFN_CONV - 1):
            off = h - (FFN_CONV - 1) + t
            acc = acc + buf_ref[off:off + tm, cols] * cw[t:t + 1, cols]
        return acc

    def gate(s, u):
        cg = conv(u[:, 0:LANES], slice(s * LANES, (s + 1) * LANES))
        cv = conv(u[:, LANES:], slice(tf + s * LANES, tf + (s + 1) * LANES))
        o_ref[:, s * LANES:(s + 1) * LANES] = ((cg * _sigmoid(cg)) * cv).astype(o_ref.dtype)

    nsub = tf // LANES
    pending = project(0)
    for s in range(nsub):
        nxt = project(s + 1) if s + 1 < nsub else None
        gate(s, pending)
        pending = nxt


def ffn_up_act(h2, w_up, layer, conv_w, *, seq, tm=1024, tf=FFN_TF):
    m, d = h2.shape
    tm = min(tm, seq)
    nf = D_FF // tf
    return pl.pallas_call(
        functools.partial(_ffn_up_act_kernel, tm=tm, tf=tf, blocks_per_seq=seq // tm),
        grid=(m // tm, nf),
        in_specs=[
            pl.BlockSpec((tm, d), lambda i, j: (i, 0)),
            pl.BlockSpec((d, tf), lambda i, j: (layer, j)),
            pl.BlockSpec((d, tf), lambda i, j: (layer, nf + j)),
            pl.BlockSpec((None, FFN_CONV, 2 * tf), lambda i, j: (j, 0, 0)),
        ],
        out_specs=pl.BlockSpec((tm, tf), lambda i, j: (i, j)),
        out_shape=jax.ShapeDtypeStruct((m, D_FF), BF16),
        scratch_shapes=[
            pltpu.VMEM((tm + SUBLANES, 2 * tf), F32),
            pltpu.VMEM((nf, SUBLANES, 2 * tf), F32),
        ],
        compiler_params=_cparams(("arbitrary", "arbitrary")),
        name="ffn_up_act",
    )(h2, w_up, w_up, conv_w)


WT_ZA, WT_ZB, WT_ZC, WT_ZG = 0, ZA_W, ZA_W + B_IN, ZA_W + B_IN + ZC_W
WT_ROWS = WT_ZG + 3 * D_MODEL


def _w_in_tile_table():
    aw = 3 * A_WIDTH
    lora = A_DECAY_LORA + A_ICLR_LORA
    c0 = A_IN + B_IN
    qkv = 2 * C_KW + C_VW
    runs = [(0, aw), (aw + lora, A_GATE_LORA), (None, ZA_WA - ZA_GD - A_GATE_LORA), (aw, lora),
            (None, ZA_W - ZA_WA - lora),
            (A_IN, B_IN),
            (c0, qkv), (c0 + qkv + 2 * C_HEADS, C_VW), (c0 + qkv, 2 * C_HEADS), (None, ZC_W - ZC_BA - 2 * C_HEADS),
            (c0 + C_IN, 3 * D_MODEL)]
    src, valid = [], []
    pending = 0
    for start, n in runs:
        if start is None:
            assert pending + n == LANES or pending == 0 and n % LANES == 0
            if pending == 0:
                src += [0] * (n // LANES)
                valid += [0] * (n // LANES)
            pending = 0
            continue
        assert pending == 0
        for off in range(0, n, LANES):
            src.append(start + off)
            valid.append(min(LANES, n - off))
        pending = n % LANES
    assert len(src) == WT_ROWS // LANES
    return src, valid


def _prep_w_in_kernel(src_ref, valid_ref, w_ref, o_ref):
    t = pl.program_id(0)
    rows = _iota2(w_ref.shape, 0)
    o_ref[...] = jnp.where(rows < valid_ref[t], w_ref[...], 0.0).astype(BF16)


def prep_w_in(w_in, l):
    depth, d, n_in = w_in.shape
    wt = jnp.swapaxes(w_in, 1, 2).reshape(depth * n_in, d)
    src, valid = _w_in_tile_table()
    src = jnp.asarray(src, jnp.int32) + l * n_in
    valid = jnp.asarray(valid, jnp.int32)
    ntile = WT_ROWS // LANES
    return pl.pallas_call(
        _prep_w_in_kernel,
        grid_spec=pltpu.PrefetchScalarGridSpec(
            num_scalar_prefetch=2, grid=(ntile,),
            in_specs=[pl.BlockSpec((pl.Element(LANES), pl.Element(d)),
                                   lambda t, src, valid: (pl.multiple_of(src[t], 2 * SUBLANES), 0))],
            out_specs=pl.BlockSpec((LANES, d), lambda t, src, valid: (t, 0))),
        out_shape=jax.ShapeDtypeStruct((WT_ROWS, d), BF16),
        compiler_params=_cparams(("parallel",)),
        name="prep_w_in",
    )(src, valid, wt)


def matmul_nt(x, wt, row0, n, tn, tm=2048, name="matmul_nt"):
    m, k = x.shape
    return pl.pallas_call(
        _matmul_nt_kernel,
        grid=(m // tm, n // tn),
        in_specs=[pl.BlockSpec((tm, k), lambda i, j: (i, 0)),
                  pl.BlockSpec((pl.Element(tn), pl.Element(k)),
                               lambda i, j: (pl.multiple_of(row0 + j * tn, 2 * SUBLANES), 0))],
        out_specs=pl.BlockSpec((tm, tn), lambda i, j: (i, j)),
        out_shape=jax.ShapeDtypeStruct((m, n), F32),
        compiler_params=_cparams(("parallel", "parallel")),
        name=name,
    )(x, wt)


def _pad_cols(w, width):
    return jnp.pad(w, ((0, 0), (0, width - w.shape[1])))


def _pad_rows(w, rows, at=0):
    return jnp.pad(w, ((at, rows - at - w.shape[0]), (0, 0)))


def _layer_params(l, p):
    aw = 3 * A_WIDTH
    mu = p["rwkv_mu"][l]
    rows = [p["rwkv_w0"][l], p["rwkv_a0"][l], p["rwkv_k_k"][l], p["rwkv_k_a"][l], p["rwkv_r_k"][l].reshape(-1),
            p["rwkv_ln_w"][l], p["rwkv_ln_b"][l],
            (p["rwkv_v0"][l - 1] if l > 0 else jnp.zeros((A_WIDTH,), F32)),
            mu[:A_WIDTH], mu[A_WIDTH:2 * A_WIDTH], mu[2 * A_WIDTH:aw]]
    rwkv_prm = jnp.pad(jnp.stack(rows), ((0, 16 - len(rows)), (0, 0)))
    mu2 = jnp.concatenate([_pad_cols(mu[None, aw + 128:], 256), mu[None, aw:aw + 128]], axis=1)
    v2 = (p["rwkv_v2"][l - 1] if l > 0 else jnp.zeros((A_VRES_LORA, A_WIDTH), F32))
    bcast = lambda t: jnp.repeat(t, C_HEAD_K)
    gdn_prm = jnp.pad(jnp.stack([bcast(p["gdn_A_log"][l]), bcast(p["gdn_dt_bias"][l]),
                                 jnp.tile(p["gdn_norm"][l], C_HEADS)]), ((0, SUBLANES - 3), (0, 0)))
    fc = p["ffn_conv"][l]
    return dict(
        v1p=(_pad_cols(p["rwkv_v1"][l - 1], LANES).astype(BF16) if l > 0 else None),
        rwkv_prm=rwkv_prm, mu2=mu2,
        w2p=_pad_rows(p["rwkv_w2"][l], 128, 0).astype(BF16),
        a2p=_pad_rows(p["rwkv_a2"][l], 128, A_DECAY_LORA).astype(BF16),
        g2p=_pad_rows(p["rwkv_g2"][l], 256, 0).astype(BF16),
        v2p=_pad_rows(v2, 128, 0).astype(BF16),
        gdn_conv=p["gdn_conv"][l].reshape(C_CONV, 3, C_KW).transpose(1, 0, 2),
        gdn_prm=gdn_prm,
        pa=p["proj_a"][l].astype(BF16), pb=p["proj_b"][l].astype(BF16), pc=p["proj_c"][l].astype(BF16),
        ffn_conv=fc.reshape(FFN_CONV, 2, D_FF // FFN_TF, FFN_TF).transpose(2, 0, 1, 3).reshape(
            D_FF // FFN_TF, FFN_CONV, 2 * FFN_TF),
    )


def _forward(x, p):
    batch, seq, d = x.shape
    m = batch * seq
    xf = x.reshape(m, d)
    tab = rope_tables(seq)
    v_first = None
    stack_rows = lambda w: w.astype(BF16).reshape(w.shape[0] * w.shape[1], w.shape[2])
    w_out_all, ffn_up_all, ffn_down_all = stack_rows(p["w_out"]), stack_rows(p["ffn_up"]), stack_rows(p["ffn_down"])
    h = rmsnorm(xf, p["attn_norm"][0], BF16)
    for l in range(DEPTH):
        lp = _layer_params(l, p)
        wt = prep_w_in(p["w_in"], l)
        za = matmul_nt(h, wt, WT_ZA, ZA_W, tn=512, name="in_proj_a")
        zb = matmul_nt(h, wt, WT_ZB, B_IN, tn=512, name="in_proj_b")
        zc = matmul_nt(h, wt, WT_ZC, ZC_W, tn=1408, tm=1024, name="in_proj_c")
        zg = matmul_nt(h, wt, WT_ZG, 3 * D_MODEL, tn=1024, name="in_proj_g")
        lv = matmul(h, lp["v1p"], tn=LANES, tm=2048, name="vres_lora") if l > 0 else None
        ya, v_l = rwkv_mix(za, lv, v_first, lp["rwkv_prm"], lp["mu2"], lp["w2p"], lp["a2p"], lp["g2p"], lp["v2p"],
                           batch=batch, seq=seq)
        if l == 0:
            v_first = v_l
        attn = [dilated_attention_group(zb, tab, gi, batch=batch, seq=seq) for gi in range(B_GROUPS)]
        yc = gated_deltanet(zc, lp["gdn_conv"], lp["gdn_prm"], batch=batch, seq=seq)
        merged = merge_mixers(ya, attn, yc, zg, lp["pa"], lp["pb"], lp["pc"])
        xf, h2 = matmul_res_norm(merged, w_out_all, l, xf, p["ffn_norm"][l], tm=512, norm_dtype=BF16,
                                 emit_sum=True, name="out_proj_norm")
        act = ffn_up_act(h2, ffn_up_all, l, lp["ffn_conv"], seq=seq)
        if l + 1 < DEPTH:
            xf, h = matmul_res_norm(act, ffn_down_all, l, xf, p["attn_norm"][l + 1], tm=256, norm_dtype=BF16,
                                    emit_sum=True, name="ffn_down_norm")
        else:
            out = matmul_res_norm(act, ffn_down_all, l, xf, p["final_norm"], tm=256, norm_dtype=F32,
                                  emit_sum=False, name="ffn_down_norm")
    return out.reshape(batch, seq, d)


def kernel(x, attn_norm, w_in, rwkv_mu, rwkv_w0, rwkv_w2, rwkv_a0, rwkv_a2, rwkv_g2, rwkv_k_k, rwkv_k_a, rwkv_r_k, rwkv_ln_w, rwkv_ln_b, rwkv_v0, rwkv_v1, rwkv_v2, gdn_conv, gdn_A_log, gdn_dt_bias, gdn_norm, proj_a, proj_b, proj_c, w_out, ffn_norm, ffn_up, ffn_conv, ffn_down, final_norm):
    params = dict(
        attn_norm=attn_norm, w_in=w_in, rwkv_mu=rwkv_mu, rwkv_w0=rwkv_w0, rwkv_w2=rwkv_w2, rwkv_a0=rwkv_a0,
        rwkv_a2=rwkv_a2, rwkv_g2=rwkv_g2, rwkv_k_k=rwkv_k_k, rwkv_k_a=rwkv_k_a, rwkv_r_k=rwkv_r_k,
        rwkv_ln_w=rwkv_ln_w, rwkv_ln_b=rwkv_ln_b, rwkv_v0=rwkv_v0, rwkv_v1=rwkv_v1, rwkv_v2=rwkv_v2,
        gdn_conv=gdn_conv, gdn_A_log=gdn_A_log, gdn_dt_bias=gdn_dt_bias, gdn_norm=gdn_norm, proj_a=proj_a,
        proj_b=proj_b, proj_c=proj_c, w_out=w_out, ffn_norm=ffn_norm, ffn_up=ffn_up, ffn_conv=ffn_conv,
        ffn_down=ffn_down, final_norm=final_norm)
    return _forward(x, params)
```

```python
import functools

import jax
import jax.numpy as jnp
from jax import lax
from jax.experimental import pallas as pl
from jax.experimental.pallas import tpu as pltpu

F32 = jnp.float32
BF16 = jnp.bfloat16

D_MODEL = 2048
DEPTH = 2
NORM_EPS = 1e-6

A_HEADS, A_HEAD = 16, 64
A_WIDTH = A_HEADS * A_HEAD
A_DECAY_LORA, A_ICLR_LORA, A_GATE_LORA, A_VRES_LORA = 64, 64, 160, 32
A_GN_EPS = 64e-5
A_IN = 3 * A_WIDTH + A_DECAY_LORA + A_ICLR_LORA + A_GATE_LORA

B_PAIRS = ((128, 1), (512, 4), (2048, 16))
B_GROUPS = 3
B_HEADS_PER_GROUP, B_HEAD = 4, 128
B_WIDTH = B_GROUPS * B_HEADS_PER_GROUP * B_HEAD
B_OUT = B_HEADS_PER_GROUP * B_HEAD
B_IN = 3 * B_WIDTH
ROPE_THETA = 500000.0
ROPE_DIM = B_HEAD // 4

C_HEADS, C_HEAD_K, C_HEAD_V = 8, 128, 128
C_KW = C_HEADS * C_HEAD_K
C_VW = C_HEADS * C_HEAD_V
C_CONV = 4
C_IN = 2 * C_KW + C_VW + 2 * C_HEADS + C_VW

D_FF = 5632
FFN_CONV = 3

LANES = 128
SUBLANES = 8
CHUNK = 64
FFN_TF = 1408
VMEM_LIMIT = 56 * 1024 * 1024

ZA_GD = 3 * A_WIDTH
ZA_WA = ZA_GD + 256
ZA_W = ZA_WA + 256
ZC_BA = 3 * C_KW + C_VW
ZC_W = ZC_BA + 128


def _cparams(sem):
    return pltpu.CompilerParams(dimension_semantics=sem, vmem_limit_bytes=VMEM_LIMIT)


def _dot(a, b):
    return jnp.dot(a.astype(BF16), b.astype(BF16), preferred_element_type=F32)


def _dot_nt(a, b):
    return lax.dot_general(a.astype(BF16), b.astype(BF16), (((1,), (1,)), ((), ())), preferred_element_type=F32)


def _dot_tn(a, b):
    return lax.dot_general(a.astype(BF16), b.astype(BF16), (((0,), (0,)), ((), ())), preferred_element_type=F32)


def _split(a):
    hi = a.astype(BF16)
    lo = (a - hi.astype(F32)).astype(BF16)
    return hi, lo


def _dot_sel_l(sel, a):
    hi, lo = _split(a)
    return (jnp.dot(sel, hi, preferred_element_type=F32) + jnp.dot(sel, lo, preferred_element_type=F32))


def _sigmoid(x):
    return 1.0 / (1.0 + jnp.exp(-x))


def _softplus(x):
    return jnp.maximum(x, 0.0) + jnp.log(1.0 + jnp.exp(-jnp.abs(x)))


def _iota2(shape, axis):
    return lax.broadcasted_iota(jnp.int32, shape, axis)


def _chunk_of(idx):
    return jnp.right_shift(idx, CHUNK.bit_length() - 1)


def _chunk_tri(n):
    ri, ci = _iota2((n, n), 0), _iota2((n, n), 1)
    return jnp.where((_chunk_of(ri) == _chunk_of(ci)) & (ri >= ci), 1.0, 0.0).astype(BF16)


def _neumann_inverse_many(n_mats, eye):
    ps = [eye + n for n in n_mats]
    qs = [_dot(n, n) for n in n_mats]
    levels = CHUNK.bit_length() - 2
    for lvl in range(levels):
        if lvl == levels - 1:
            ps = [p + _dot(p, q) for p, q in zip(ps, qs)]
        else:
            prods = [_dot(jnp.concatenate([q, p], axis=0), q) for p, q in zip(ps, qs)]
            ps = [p + pr[LANES:] for p, pr in zip(ps, prods)]
            qs = [pr[0:LANES] for pr in prods]
    return ps


def _neumann_inverse(n_mat, eye):
    return _neumann_inverse_many([n_mat], eye)[0]


def _rmsnorm_kernel(x_ref, g_ref, o_ref):
    x = x_ref[...]
    ms = jnp.mean(x * x, axis=-1, keepdims=True)
    o_ref[...] = ((x * lax.rsqrt(ms + NORM_EPS)) * g_ref[...]).astype(o_ref.dtype)


def rmsnorm(x, g, out_dtype, tm=512):
    m, d = x.shape
    return pl.pallas_call(
        _rmsnorm_kernel,
        grid=(m // tm,),
        in_specs=[pl.BlockSpec((tm, d), lambda i: (i, 0)), pl.BlockSpec((1, d), lambda i: (0, 0))],
        out_specs=pl.BlockSpec((tm, d), lambda i: (i, 0)),
        out_shape=jax.ShapeDtypeStruct((m, d), out_dtype),
        compiler_params=_cparams(("parallel",)),
        name="rmsnorm",
    )(x, g.reshape(1, d))


def _matmul_kernel(x_ref, w_ref, o_ref):
    o_ref[...] = jnp.dot(x_ref[...], w_ref[...], preferred_element_type=F32).astype(o_ref.dtype)


def _matmul_nt_kernel(x_ref, wt_ref, o_ref):
    o_ref[...] = lax.dot_general(x_ref[...], wt_ref[...], (((1,), (1,)), ((), ())), preferred_element_type=F32)


def matmul(x, w, tn, tm=1024, name="matmul"):
    m, k = x.shape
    n = w.shape[1]
    tm = min(tm, m)
    return pl.pallas_call(
        _matmul_kernel,
        grid=(m // tm, n // tn),
        in_specs=[pl.BlockSpec((tm, k), lambda i, j: (i, 0)), pl.BlockSpec((k, tn), lambda i, j: (0, j))],
        out_specs=pl.BlockSpec((tm, tn), lambda i, j: (i, j)),
        out_shape=jax.ShapeDtypeStruct((m, n), F32),
        compiler_params=_cparams(("parallel", "parallel")),
        name=name,
    )(x, w)


def _matmul_res_norm_kernel(x_ref, w_ref, r_ref, g_ref, *out_refs):
    y = r_ref[...] + jnp.dot(x_ref[...], w_ref[...], preferred_element_type=F32)
    ms = jnp.mean(y * y, axis=-1, keepdims=True)
    hn = (y * lax.rsqrt(ms + NORM_EPS)) * g_ref[...]
    if len(out_refs) == 2:
        out_refs[0][...] = y
    out_refs[-1][...] = hn.astype(out_refs[-1].dtype)


def matmul_res_norm(x, w, layer, residual, gain, *, tm, norm_dtype, emit_sum, name):
    m, k = x.shape
    n = w.shape[1]
    rows = lambda dt: (pl.BlockSpec((tm, n), lambda i: (i, 0)), jax.ShapeDtypeStruct((m, n), dt))
    outs = ([rows(F32)] if emit_sum else []) + [rows(norm_dtype)]
    res = pl.pallas_call(
        _matmul_res_norm_kernel,
        grid=(m // tm,),
        in_specs=[pl.BlockSpec((tm, k), lambda i: (i, 0)),
                  pl.BlockSpec((k, n), lambda i: (layer, 0), pipeline_mode=pl.Buffered(1)),
                  pl.BlockSpec((tm, n), lambda i: (i, 0)),
                  pl.BlockSpec((1, n), lambda i: (0, 0))],
        out_specs=[o[0] for o in outs],
        out_shape=[o[1] for o in outs],
        compiler_params=_cparams(("parallel",)),
        name=name,
    )(x, w, residual, gain.reshape(1, n))
    return tuple(res) if emit_sum else res[0]


def _rwkv_kernel(r_ref, k_ref, v_ref, gd_ref, wa_ref, lv_ref, vf_ref, p_ref, mu2_ref, w2_ref, a2_ref, g2_ref,
                 v2_ref, y_ref, vout_ref, s_ref, br_ref, bk_ref, bv_ref, bgd_ref, bwa_ref, *, tb, pp, has_vmix):
    i = pl.program_id(2)
    h = SUBLANES

    @pl.when(i == 0)
    def _():
        s_ref[...] = jnp.zeros_like(s_ref)
        for b in (br_ref, bk_ref, bv_ref, bgd_ref, bwa_ref):
            b[0:h, :] = jnp.zeros((h, b.shape[1]), F32)

    prm = p_ref[...]
    w0, a0, k_k, k_a, r_k, ln_w, ln_b, v0 = [prm[j:j + 1] for j in range(8)]
    mu_r, mu_k, mu_v = prm[8:9], prm[9:10], prm[10:11]
    mu2 = mu2_ref[...]
    mu_gd, mu_wa = mu2[:, 0:256], mu2[:, 256:384]

    def shifted_mix(x_ref, buf_ref, mu):
        x = x_ref[...]
        buf_ref[h:h + tb, :] = x
        xs = buf_ref[h - 1:h - 1 + tb, :]
        buf_ref[0:h, :] = x[tb - h:tb, :]
        return x + (xs - x) * mu

    r = shifted_mix(r_ref, br_ref, mu_r)
    k = shifted_mix(k_ref, bk_ref, mu_k)
    v = shifted_mix(v_ref, bv_ref, mu_v)
    gd = shifted_mix(gd_ref, bgd_ref, mu_gd)
    wa = shifted_mix(wa_ref, bwa_ref, mu_wa)

    lane = _iota2((1, LANES), 1)
    m0 = jnp.where(lane < A_HEAD, 1.0, 0.0)
    m1 = 1.0 - m0
    ri = _iota2((LANES, LANES), 0)
    ci = _iota2((LANES, LANES), 1)
    same = _chunk_of(ri) == _chunk_of(ci)
    strict = same & (ri > ci)
    incl = same & (ri >= ci)
    eye = jnp.where(ri == ci, 1.0, 0.0)
    tri = _chunk_tri(tb)
    pair_lanes = [slice(q * LANES, (q + 1) * LANES) for q in range(pp)]

    def head_sum(x):
        outs = []
        for ls in pair_lanes:
            s0 = jnp.sum(x[:, ls] * m0, axis=-1, keepdims=True)
            s1 = jnp.sum(x[:, ls] * m1, axis=-1, keepdims=True)
            outs.append(jnp.where(lane < A_HEAD, s0, s1))
        return jnp.concatenate(outs, axis=1)

    lw = -jnp.exp(-0.5) * _sigmoid(w0 + _dot(jnp.tanh(wa), w2_ref[...]))
    a = _sigmoid(a0 + _dot(wa, a2_ref[...]))
    g = _dot(_sigmoid(gd), g2_ref[...])
    kk = k * k_k
    kk = kk / jnp.maximum(jnp.sqrt(head_sum(kk * kk)), 1e-12)
    k = k * (1.0 + (a - 1.0) * k_a)
    if has_vmix:
        v_mix = _sigmoid(v0 + _dot(lv_ref[...], v2_ref[...]))
        v = v + (vf_ref[...] - v) * v_mix
    vout_ref[...] = v

    cum = _dot_sel_l(tri, lw)
    e_pos = jnp.exp(cum)
    e_neg = jnp.exp(-cum)
    r_t = r * e_pos
    a_t = -kk * jnp.exp(cum - lw)
    b_t = (kk * a) * e_neg
    k_t = k * e_neg

    def stack_masked(x):
        return jnp.concatenate([x * m0, x * m1], axis=0)

    def stack_dup(x):
        return jnp.concatenate([x, x], axis=0)

    kb = kk * a
    nchunk = tb // CHUNK
    inst = [(c, q) for c in range(nchunk) for q in range(pp)]
    rows_of = lambda c: slice(c * CHUNK, (c + 1) * CHUNK)
    last_of = lambda c: slice((c + 1) * CHUNK - 1, (c + 1) * CHUNK)
    a_s = {cq: stack_masked(a_t[rows_of(cq[0]), pair_lanes[cq[1]]]) for cq in inst}
    r_s = {cq: stack_masked(r_t[rows_of(cq[0]), pair_lanes[cq[1]]]) for cq in inst}
    v_s = {cq: stack_masked(v[rows_of(cq[0]), pair_lanes[cq[1]]]) for cq in inst}
    gm = {}
    for c, q in inst:
        sl, ls = rows_of(c), pair_lanes[q]
        right = jnp.concatenate([stack_dup(b_t[sl, ls]), stack_dup(k_t[sl, ls])], axis=0)
        gm[c, q] = _dot_nt(jnp.concatenate([a_s[c, q], r_s[c, q]], axis=0), right)
    n_ab = [jnp.where(strict, gm[cq][0:LANES, 0:LANES], 0.0) for cq in inst]
    t_inv = dict(zip(inst, _neumann_inverse_many(n_ab, eye)))
    gv = {cq: _dot(jnp.where(strict, gm[cq][0:LANES, LANES:], 0.0), v_s[cq]) for cq in inst}
    tu = {cq: _dot(t_inv[cq], jnp.concatenate([a_s[cq], gv[cq]], axis=1)) for cq in inst}
    g_r = {cq: jnp.where(jnp.concatenate([incl, incl], axis=1), gm[cq][LANES:], 0.0) for cq in inst}

    states = [s_ref[q] for q in range(pp)]
    for c in range(nchunk):
        sl = rows_of(c)
        x0 = [_dot_nt(jnp.concatenate([tu[c, q][:, 0:LANES], r_s[c, q]], axis=0), states[q]) for q in range(pp)]
        uv = [jnp.concatenate([x0[q][0:LANES] + tu[c, q][:, LANES:], v_s[c, q]], axis=0) for q in range(pp)]
        for q, ls in enumerate(pair_lanes):
            cum_last = cum[last_of(c), ls]
            e_end = jnp.exp(cum_last - cum[sl, ls])
            bk_end = jnp.concatenate([stack_dup(kb[sl, ls] * e_end), stack_dup(k[sl, ls] * e_end)], axis=0)
            states[q] = jnp.where(same, states[q] * jnp.exp(cum_last) + _dot_tn(uv[q], bk_end), 0.0)
        for q, ls in enumerate(pair_lanes):
            y_s = x0[q][LANES:] + _dot(g_r[c, q], uv[q])
            y_ref[sl, ls] = y_s[0:CHUNK] + y_s[CHUNK:]
    for q in range(pp):
        s_ref[q] = states[q]

    y = y_ref[...]
    inv_n = 1.0 / A_HEAD
    mean = head_sum(y) * inv_n
    d = y - mean
    var = head_sum(d * d) * inv_n
    yn = d * lax.rsqrt(var + A_GN_EPS) * ln_w + ln_b
    bonus = head_sum(r * k * r_k) * v
    y_ref[...] = (yn + bonus) * g


def rwkv_mix(za, lv, v_first, prm, mu2, w2p, a2p, g2p, v2p, *, batch, seq, tb=256, pp=4):
    has_vmix = v_first is not None
    nt = seq // tb
    w = pp * LANES
    npair = A_WIDTH // w
    row = lambda b, p, i: b * nt + i
    if v_first is None:
        v_first = lv = za
    in_specs = [
        pl.BlockSpec((tb, w), lambda b, p, i: (row(b, p, i), p)),
        pl.BlockSpec((tb, w), lambda b, p, i: (row(b, p, i), npair + p)),
        pl.BlockSpec((tb, w), lambda b, p, i: (row(b, p, i), 2 * npair + p)),
        pl.BlockSpec((tb, 256), lambda b, p, i: (row(b, p, i), ZA_GD // 256)),
        pl.BlockSpec((tb, LANES), lambda b, p, i: (row(b, p, i), ZA_WA // LANES)),
        pl.BlockSpec((tb, LANES), lambda b, p, i: (row(b, p, i), 0)),
        pl.BlockSpec((tb, w), lambda b, p, i: (row(b, p, i), p)),
        pl.BlockSpec((16, w), lambda b, p, i: (0, p)),
        pl.BlockSpec((1, 384), lambda b, p, i: (0, 0)),
        pl.BlockSpec((LANES, w), lambda b, p, i: (0, p)),
        pl.BlockSpec((LANES, w), lambda b, p, i: (0, p)),
        pl.BlockSpec((256, w), lambda b, p, i: (0, p)),
        pl.BlockSpec((LANES, w), lambda b, p, i: (0, p)),
    ]
    out_spec = pl.BlockSpec((tb, w), lambda b, p, i: (row(b, p, i), p))
    out_sds = jax.ShapeDtypeStruct((batch * seq, A_WIDTH), F32)
    return pl.pallas_call(
        functools.partial(_rwkv_kernel, tb=tb, pp=pp, has_vmix=has_vmix),
        grid=(batch, npair, nt),
        in_specs=in_specs,
        out_specs=[out_spec, out_spec],
        out_shape=[out_sds, out_sds],
        scratch_shapes=[
            pltpu.VMEM((pp, LANES, LANES), F32),
            pltpu.VMEM((tb + SUBLANES, w), F32),
            pltpu.VMEM((tb + SUBLANES, w), F32),
            pltpu.VMEM((tb + SUBLANES, w), F32),
            pltpu.VMEM((tb + SUBLANES, 256), F32),
            pltpu.VMEM((tb + SUBLANES, LANES), F32),
        ],
        compiler_params=_cparams(("parallel", "parallel", "arbitrary")),
        name="rwkv7",
    )(za, za, za, za, za, lv, v_first, prm, mu2, w2p, a2p, g2p, v2p)


def _attn_kernel(*refs, tq, dil, nb, has_prev):
    if has_prev:
        (q_ref, kc_ref, kp_ref, vc_ref, vp_ref, cc_ref, sc_ref, nc_ref, cp_ref, sp_ref, np_ref,
         o_ref, lse_ref) = refs
    else:
        q_ref, kc_ref, vc_ref, cc_ref, sc_ref, nc_ref, o_ref, lse_ref = refs
    i = pl.program_id(1)
    scale = B_HEAD ** -0.5

    def rows(j, r):
        start = j * tq * dil + r
        return pl.ds(start, tq, stride=dil) if dil > 1 else pl.ds(start, tq)

    def rope(x, tabs):
        half = ROPE_DIM // 2
        return (x * tabs[0] + pltpu.roll(x, half, axis=1) * tabs[1]
                + pltpu.roll(x, LANES - half, axis=1) * tabs[2])

    if has_prev:
        ri, ci = _iota2((tq, 2 * tq), 0), _iota2((tq, 2 * tq), 1)
        cur_ok = (ci >= tq) & ((ci - tq) <= ri)
        no_prev = jnp.where(i > 0, 0, 2 * tq)
        valid_first = ((ci < tq) & (ci >= ri + no_prev)) | cur_ok
        valid_inner = ((ci < tq) & (ci >= ri)) | cur_ok
    else:
        causal = _iota2((tq, tq), 1) <= _iota2((tq, tq), 0)

    inst = [(r, j) for r in range(dil) for j in range(nb)]
    qs, ks, vs = {}, {}, {}
    for r in range(dil):
        if has_prev:
            rp = rows(0, r)
            ks[r, -1] = rope(kp_ref[rp, :], (cp_ref[rp, :], sp_ref[rp, :], np_ref[rp, :]))
            vs[r, -1] = vp_ref[rp, :]
        for j in range(nb):
            rw = rows(j, r)
            tabs = (cc_ref[rw, :], sc_ref[rw, :], nc_ref[rw, :])
            qs[r, j] = rope(q_ref[rw, :], tabs) * scale
            ks[r, j] = rope(kc_ref[rw, :], tabs)
            vs[r, j] = vc_ref[rw, :]
    if has_prev:
        s = {(r, j): jnp.where(valid_first if j == 0 else valid_inner,
                               _dot_nt(qs[r, j], jnp.concatenate([ks[r, j - 1], ks[r, j]], axis=0)), -1e30)
             for r, j in inst}
    else:
        s = {rj: jnp.where(causal, _dot_nt(qs[rj], ks[rj]), -1e30) for rj in inst}
    m = {rj: jnp.max(s[rj], axis=-1, keepdims=True) for rj in inst}
    p = {rj: jnp.exp(s[rj] - m[rj]) for rj in inst}
    den = {rj: jnp.sum(p[rj], axis=-1, keepdims=True) for rj in inst}
    if has_prev:
        num = {(r, j): _dot(p[r, j], jnp.concatenate([vs[r, j - 1], vs[r, j]], axis=0)) for r, j in inst}
    else:
        num = {rj: _dot(p[rj], vs[rj]) for rj in inst}
    for r, j in inst:
        o_ref[rows(j, r), :] = num[r, j] / den[r, j]
        lse_ref[rows(j, r), :] = jnp.broadcast_to(m[r, j] + jnp.log(den[r, j]), (tq, B_HEAD))


def dilated_attention_group(zb, rope_tabs, gi, *, batch, seq, tq=128, blocks_per_step=8):
    win, dil = B_PAIRS[gi]
    assert win // dil == tq
    nq = seq // (dil * tq)
    has_prev = nq > 1
    nb = min(nq, max(1, blocks_per_step // dil))
    rows_step = nb * tq * dil
    nsteps = seq // rows_step
    prev_rows = tq * dil
    nprev = seq // prev_rows
    nh = B_HEADS_PER_GROUP
    cur = lambda col: pl.BlockSpec((rows_step, B_HEAD), lambda b, i, hh: (b * nsteps + i, col * nh + hh))
    prv = lambda col: pl.BlockSpec(
        (prev_rows, B_HEAD), lambda b, i, hh: (b * nprev + jnp.maximum(i * nb - 1, 0), col * nh + hh))
    tab_cur = pl.BlockSpec((rows_step, LANES), lambda b, i, hh: (i, 0))
    tab_prv = pl.BlockSpec((prev_rows, LANES), lambda b, i, hh: (jnp.maximum(i * nb - 1, 0), 0))
    qc, kc, vc = gi, B_GROUPS + gi, 2 * B_GROUPS + gi
    if has_prev:
        in_specs = [cur(qc), cur(kc), prv(kc), cur(vc), prv(vc)] + [tab_cur] * 3 + [tab_prv] * 3
        args = (zb, zb, zb, zb, zb) + tuple(rope_tabs) * 2
    else:
        in_specs = [cur(qc), cur(kc), cur(vc)] + [tab_cur] * 3
        args = (zb, zb, zb) + tuple(rope_tabs)
    out_spec = pl.BlockSpec((rows_step, B_HEAD), lambda b, i, hh: (b * nsteps + i, hh))
    out_sds = jax.ShapeDtypeStruct((batch * seq, B_OUT), F32)
    return pl.pallas_call(
        functools.partial(_attn_kernel, tq=tq, dil=dil, nb=nb, has_prev=has_prev),
        grid=(batch, nsteps, nh),
        in_specs=in_specs,
        out_specs=[out_spec, out_spec],
        out_shape=[out_sds, out_sds],
        compiler_params=_cparams(("parallel", "parallel", "arbitrary")),
        name=f"dilated_attn_g{gi}",
    )(*args)


def rope_tables(seq):
    half = ROPE_DIM // 2
    inv = ROPE_THETA ** (-jnp.arange(half, dtype=F32) / half)
    ang = jnp.arange(seq, dtype=F32)[:, None] * inv[None, :]
    cos, sin = jnp.cos(ang), jnp.sin(ang)
    z = jnp.zeros((seq, LANES - ROPE_DIM), F32)
    zh = jnp.zeros((seq, half), F32)
    c_tab = jnp.concatenate([cos, cos, jnp.ones_like(z)], axis=1)
    s_pos = jnp.concatenate([zh, sin, z], axis=1)
    s_neg = jnp.concatenate([-sin, zh, z], axis=1)
    return c_tab, s_pos, s_neg


def _gdn_kernel(q_ref, k_ref, v_ref, gate_ref, ba_ref, cw_ref, p_ref, o_ref, s_ref, bq_ref, bk_ref, bv_ref,
                *, tb, pp):
    i = pl.program_id(2)
    h = SUBLANES
    hd = C_HEAD_K

    @pl.when(i == 0)
    def _():
        s_ref[...] = jnp.zeros_like(s_ref)
        for b in (bq_ref, bk_ref, bv_ref):
            b[0:h, :] = jnp.zeros((h, b.shape[1]), F32)

    cw = cw_ref[...]
    prm = p_ref[...]

    def conv_silu(x_ref, buf_ref, w):
        x = x_ref[...]
        buf_ref[h:h + tb, :] = x
        acc = x * w[C_CONV - 1:C_CONV]
        for j in range(C_CONV - 1):
            off = h - (C_CONV - 1) + j
            acc = acc + buf_ref[off:off + tb, :] * w[j:j + 1]
        buf_ref[0:h, :] = x[tb - h:tb, :]
        return acc * _sigmoid(acc)

    q = conv_silu(q_ref, bq_ref, cw[0])
    k = conv_silu(k_ref, bk_ref, cw[1])
    v = conv_silu(v_ref, bv_ref, cw[2])

    ri = _iota2((LANES, LANES), 0)
    ci = _iota2((LANES, LANES), 1)
    same = _chunk_of(ri) == _chunk_of(ci)
    strict = same & (ri > ci)
    incl = same & (ri >= ci)
    eye = jnp.where(ri == ci, 1.0, 0.0)
    tri = _chunk_tri(tb)

    ba = ba_ref[...]
    nh = 2 * pp
    head_lanes = [slice(hh * hd, (hh + 1) * hd) for hh in range(nh)]
    vh = [v[:, ls] for ls in head_lanes]
    q_ss = [jnp.sum(q[:, ls] * q[:, ls], axis=-1, keepdims=True) for ls in head_lanes]
    k_ss = [jnp.sum(k[:, ls] * k[:, ls], axis=-1, keepdims=True) for ls in head_lanes]
    b_raw = [ba[:, hh:hh + 1] for hh in range(nh)]
    a_raw = [ba[:, C_HEADS + hh:C_HEADS + hh + 1] for hh in range(nh)]
    qh = [q[:, ls] / jnp.maximum(jnp.sqrt(ss), 1e-12) * (hd ** -0.5) for ls, ss in zip(head_lanes, q_ss)]
    kh = [k[:, ls] / jnp.maximum(jnp.sqrt(ss), 1e-12) for ls, ss in zip(head_lanes, k_ss)]
    beta = [_sigmoid(b) for b in b_raw]
    glog = [-jnp.exp(prm[0:1, ls]) * _softplus(al + prm[1:2, ls]) for ls, al in zip(head_lanes, a_raw)]
    gam = [_dot_sel_l(tri, gl) for gl in glog]

    nchunk = tb // CHUNK
    inst = [(c, pr) for c in range(nchunk) for pr in range(pp)]
    rows_of = lambda c: slice(c * CHUNK, (c + 1) * CHUNK)

    def stack(xs, cp):
        c, pr = cp
        return jnp.concatenate([xs[2 * pr][rows_of(c)], xs[2 * pr + 1][rows_of(c)]], axis=0)

    k_s = {cp: stack(kh, cp) for cp in inst}
    q_s = {cp: stack(qh, cp) for cp in inst}
    beta_s = {cp: stack(beta, cp) for cp in inst}
    gam_s = {cp: stack(gam, cp) for cp in inst}
    kq = {cp: _dot_nt(jnp.concatenate([k_s[cp], q_s[cp]], axis=0), k_s[cp]) for cp in inst}
    dm = {cp: jnp.exp(jnp.where(incl, gam_s[cp] - gam_s[cp].T, -1e30)) for cp in inst}
    n_mats = [jnp.where(strict, -(beta_s[cp] * kq[cp][0:LANES] * dm[cp]), 0.0) for cp in inst]
    t_inv = dict(zip(inst, _neumann_inverse_many(n_mats, eye)))
    e_gam = {cp: jnp.exp(gam_s[cp]) for cp in inst}
    sol = {cp: _dot(t_inv[cp], jnp.concatenate([stack(vh, cp) * beta_s[cp],
                                                k_s[cp] * (beta_s[cp] * e_gam[cp])], axis=1)) for cp in inst}

    states = [s_ref[hh] for hh in range(nh)]
    for c in range(nchunk):
        sl = rows_of(c)
        ws = []
        for hh in range(nh):
            cp, hs = (c, hh // 2), slice((hh % 2) * CHUNK, (hh % 2 + 1) * CHUNK)
            qg = (q_s[cp] * e_gam[cp])[hs]
            ws.append(_dot(jnp.concatenate([sol[cp][hs, hd:], qg], axis=0), states[hh]))
        v_new = [sol[c, hh // 2][(hh % 2) * CHUNK:(hh % 2 + 1) * CHUNK, 0:hd] - ws[hh][0:CHUNK]
                 for hh in range(nh)]
        for hh in range(nh):
            g_h = gam[hh][sl]
            g_last = g_h[CHUNK - 1:CHUNK, :]
            states[hh] = states[hh] * jnp.exp(g_last) + _dot_tn(kh[hh][sl] * jnp.exp(g_last - g_h), v_new[hh])
        for pr in range(pp):
            h0 = 2 * pr
            attn = kq[c, pr][LANES:] * dm[c, pr]
            o_s = (jnp.concatenate([ws[h0][CHUNK:], ws[h0 + 1][CHUNK:]], axis=0)
                   + _dot(attn, jnp.concatenate([v_new[h0], v_new[h0 + 1]], axis=0)))
            o_ref[sl, h0 * hd:(h0 + 1) * hd] = o_s[0:CHUNK]
            o_ref[sl, (h0 + 1) * hd:(h0 + 2) * hd] = o_s[CHUNK:]
    for hh in range(nh):
        s_ref[hh] = states[hh]

    gate = gate_ref[...]
    o_h = [o_ref[:, ls] for ls in head_lanes]
    o_ms = [jnp.mean(o * o, axis=-1, keepdims=True) for o in o_h]
    for o, ms, ls in zip(o_h, o_ms, head_lanes):
        gt = gate[:, ls]
        o_ref[:, ls] = (o * lax.rsqrt(ms + NORM_EPS) * prm[2:3, ls]) * (gt * _sigmoid(gt))


def gated_deltanet(zc, conv_w, prm, *, batch, seq, tb=256, pp=4):
    assert 2 * pp == C_HEADS, "the kernel indexes the per-head beta/alpha columns statically"
    nt = seq // tb
    npair = C_HEADS // (2 * pp)
    wblk = 2 * pp * C_HEAD_K
    row = lambda b, p, i: b * nt + i
    in_specs = [
        pl.BlockSpec((tb, wblk), lambda b, p, i: (row(b, p, i), p)),
        pl.BlockSpec((tb, wblk), lambda b, p, i: (row(b, p, i), npair + p)),
        pl.BlockSpec((tb, wblk), lambda b, p, i: (row(b, p, i), 2 * npair + p)),
        pl.BlockSpec((tb, wblk), lambda b, p, i: (row(b, p, i), 3 * npair + p)),
        pl.BlockSpec((tb, LANES), lambda b, p, i: (row(b, p, i), ZC_BA // LANES)),
        pl.BlockSpec((3, C_CONV, wblk), lambda b, p, i: (0, 0, p)),
        pl.BlockSpec((SUBLANES, wblk), lambda b, p, i: (0, p)),
    ]
    return pl.pallas_call(
        functools.partial(_gdn_kernel, tb=tb, pp=pp),
        grid=(batch, npair, nt),
        in_specs=in_specs,
        out_specs=pl.BlockSpec((tb, wblk), lambda b, p, i: (row(b, p, i), p)),
        out_shape=jax.ShapeDtypeStruct((batch * seq, C_VW), F32),
        scratch_shapes=[
            pltpu.VMEM((2 * pp, C_HEAD_K, C_HEAD_V), F32),
            pltpu.VMEM((tb + SUBLANES, wblk), F32),
            pltpu.VMEM((tb + SUBLANES, wblk), F32),
            pltpu.VMEM((tb + SUBLANES, wblk), F32),
        ],
        compiler_params=_cparams(("parallel", "parallel", "arbitrary")),
        name="gated_deltanet",
    )(zc, zc, zc, zc, zc, conv_w, prm)


def _merge_kernel(ya_ref, o0_ref, o1_ref, o2_ref, l0_ref, l1_ref, l2_ref, yc_ref, ga_ref, gb_ref, gc_ref,
                  pa_ref, pb_ref, pc_ref, out_ref):
    l0, l1, l2 = l0_ref[...], l1_ref[...], l2_ref[...]
    m = jnp.maximum(jnp.maximum(l0, l1), l2)
    w0, w1, w2 = jnp.exp(l0 - m), jnp.exp(l1 - m), jnp.exp(l2 - m)
    yb = (w0 * o0_ref[...] + w1 * o1_ref[...] + w2 * o2_ref[...]) / (w0 + w1 + w2)
    merged = (_sigmoid(ga_ref[...]) * _dot(ya_ref[...], pa_ref[...])
              + _sigmoid(gb_ref[...]) * _dot(yb, pb_ref[...])
              + _sigmoid(gc_ref[...]) * _dot(yc_ref[...], pc_ref[...]))
    out_ref[...] = merged.astype(out_ref.dtype)


def merge_mixers(ya, attn, yc, zg, pa, pb, pc, tm=256):
    m = ya.shape[0]
    d = D_MODEL
    rows = lambda w: pl.BlockSpec((tm, w), lambda i: (i, 0))
    const = lambda a: pl.BlockSpec(a.shape, lambda i: (0, 0), pipeline_mode=pl.Buffered(1))
    (o0, l0), (o1, l1), (o2, l2) = attn
    in_specs = ([rows(A_WIDTH)] + [rows(B_OUT)] * 6 + [rows(C_VW)]
                + [pl.BlockSpec((tm, d), lambda i, j=j: (i, j)) for j in range(3)]
                + [const(pa), const(pb), const(pc)])
    return pl.pallas_call(
        _merge_kernel,
        grid=(m // tm,),
        in_specs=in_specs,
        out_specs=pl.BlockSpec((tm, d), lambda i: (i, 0)),
        out_shape=jax.ShapeDtypeStruct((m, d), BF16),
        compiler_params=_cparams(("parallel",)),
        name="merge_mixers",
    )(ya, o0, o1, o2, l0, l1, l2, yc, zg, zg, zg, pa, pb, pc)


def _ffn_up_act_kernel(x_ref, wg_ref, wv_ref, cw_ref, o_ref, buf_ref, halo_ref, *, tm, tf, blocks_per_seq):
    i = pl.program_id(0)
    j = pl.program_id(1)
    h = SUBLANES
    first = (i % blocks_per_seq) == 0

    @pl.when(first)
    def _():
        buf_ref[0:h, :] = jnp.zeros((h, 2 * tf), F32)

    @pl.when(jnp.logical_not(first))
    def _():
        buf_ref[0:h, :] = halo_ref[j]

    cw = cw_ref[...]

    def project(s):
        cs = slice(s * LANES, (s + 1) * LANES)
        w = jnp.concatenate([wg_ref[:, cs], wv_ref[:, cs]], axis=1)
        return jnp.dot(x_ref[...], w, preferred_element_type=F32)

    def conv(u, cols):
        buf_ref[h:h + tm, cols] = u
        halo_ref[j, :, cols] = u[tm - h:tm, :]
        acc = u * cw[FFN_CONV - 1:FFN_CONV, cols]
        for t in range(FFN_CONV - 1):
            off = h - (FFN_CONV - 1) + t
            acc = acc + buf_ref[off:off + tm, cols] * cw[t:t + 1, cols]
        return acc

    def gate(s, u):
        cg = conv(u[:, 0:LANES], slice(s * LANES, (s + 1) * LANES))
        cv = conv(u[:, LANES:], slice(tf + s * LANES, tf + (s + 1) * LANES))
        o_ref[:, s * LANES:(s + 1) * LANES] = ((cg * _sigmoid(cg)) * cv).astype(o_ref.dtype)

    nsub = tf // LANES
    pending = project(0)
    for s in range(nsub):
        nxt = project(s + 1) if s + 1 < nsub else None
        gate(s, pending)
        pending = nxt


def ffn_up_act(h2, w_up, layer, conv_w, *, seq, tm=1024, tf=FFN_TF):
    m, d = h2.shape
    tm = min(tm, seq)
    nf = D_FF // tf
    return pl.pallas_call(
        functools.partial(_ffn_up_act_kernel, tm=tm, tf=tf, blocks_per_seq=seq // tm),
        grid=(m // tm, nf),
        in_specs=[
            pl.BlockSpec((tm, d), lambda i, j: (i, 0)),
            pl.BlockSpec((d, tf), lambda i, j: (layer, j)),
            pl.BlockSpec((d, tf), lambda i, j: (layer, nf + j)),
            pl.BlockSpec((None, FFN_CONV, 2 * tf), lambda i, j: (j, 0, 0)),
        ],
        out_specs=pl.BlockSpec((tm, tf), lambda i, j: (i, j)),
        out_shape=jax.ShapeDtypeStruct((m, D_FF), BF16),
        scratch_shapes=[
            pltpu.VMEM((tm + SUBLANES, 2 * tf), F32),
            pltpu.VMEM((nf, SUBLANES, 2 * tf), F32),
        ],
        compiler_params=_cparams(("arbitrary", "arbitrary")),
        name="ffn_up_act",
    )(h2, w_up, w_up, conv_w)


WT_ZA, WT_ZB, WT_ZC, WT_ZG = 0, ZA_W, ZA_W + B_IN, ZA_W + B_IN + ZC_W
WT_TILES_PER_STEP = 4
WT_ROWS = -(-(WT_ZG + 3 * D_MODEL) // (WT_TILES_PER_STEP * LANES)) * (WT_TILES_PER_STEP * LANES)


def _w_in_tile_table():
    aw = 3 * A_WIDTH
    lora = A_DECAY_LORA + A_ICLR_LORA
    c0 = A_IN + B_IN
    qkv = 2 * C_KW + C_VW
    runs = [(0, aw), (aw + lora, A_GATE_LORA), (None, ZA_WA - ZA_GD - A_GATE_LORA), (aw, lora),
            (None, ZA_W - ZA_WA - lora),
            (A_IN, B_IN),
            (c0, qkv), (c0 + qkv + 2 * C_HEADS, C_VW), (c0 + qkv, 2 * C_HEADS), (None, ZC_W - ZC_BA - 2 * C_HEADS),
            (c0 + C_IN, 3 * D_MODEL)]
    src, valid = [], []
    pending = 0
    for start, n in runs:
        if start is None:
            assert pending + n == LANES or pending == 0 and n % LANES == 0
            if pending == 0:
                src += [0] * (n // LANES)
                valid += [0] * (n // LANES)
            pending = 0
            continue
        assert pending == 0
        for off in range(0, n, LANES):
            src.append(start + off)
            valid.append(min(LANES, n - off))
        pending = n % LANES
    assert pending == 0 and len(src) == (WT_ZG + 3 * D_MODEL) // LANES
    tail = WT_ROWS // LANES - len(src)
    return src + [0] * tail, valid + [0] * tail


def _prep_w_in_kernel(src_ref, valid_ref, *refs):
    *w_refs, o_ref = refs
    t = pl.program_id(0)
    rows = _iota2(w_refs[0].shape, 0)
    for g, w_ref in enumerate(w_refs):
        tile = jnp.where(rows < valid_ref[t * len(w_refs) + g], w_ref[...], 0.0)
        o_ref[g * LANES:(g + 1) * LANES, :] = tile.astype(BF16)


def prep_w_in(w_in, l):
    depth, d, n_in = w_in.shape
    wt = jnp.swapaxes(w_in, 1, 2).reshape(depth * n_in, d)
    src, valid = _w_in_tile_table()
    src = jnp.asarray(src, jnp.int32) + l * n_in
    valid = jnp.asarray(valid, jnp.int32)
    g = WT_TILES_PER_STEP
    tile_spec = lambda k: pl.BlockSpec(
        (pl.Element(LANES), pl.Element(d)),
        lambda t, src, valid: (pl.multiple_of(src[t * g + k], 2 * SUBLANES), 0))
    return pl.pallas_call(
        _prep_w_in_kernel,
        grid_spec=pltpu.PrefetchScalarGridSpec(
            num_scalar_prefetch=2, grid=(WT_ROWS // (g * LANES),),
            in_specs=[tile_spec(k) for k in range(g)],
            out_specs=pl.BlockSpec((g * LANES, d), lambda t, src, valid: (t, 0))),
        out_shape=jax.ShapeDtypeStruct((WT_ROWS, d), BF16),
        compiler_params=_cparams(("parallel",)),
        name="prep_w_in",
    )(src, valid, *([wt] * g))


def matmul_nt(x, wt, row0, n, tn, tm=2048, name="matmul_nt"):
    m, k = x.shape
    return pl.pallas_call(
        _matmul_nt_kernel,
        grid=(m // tm, n // tn),
        in_specs=[pl.BlockSpec((tm, k), lambda i, j: (i, 0)),
                  pl.BlockSpec((pl.Element(tn), pl.Element(k)),
                               lambda i, j: (pl.multiple_of(row0 + j * tn, 2 * SUBLANES), 0))],
        out_specs=pl.BlockSpec((tm, tn), lambda i, j: (i, j)),
        out_shape=jax.ShapeDtypeStruct((m, n), F32),
        compiler_params=_cparams(("parallel", "parallel")),
        name=name,
    )(x, wt)


def _pad_cols(w, width):
    return jnp.pad(w, ((0, 0), (0, width - w.shape[1])))


def _pad_rows(w, rows, at=0):
    return jnp.pad(w, ((at, rows - at - w.shape[0]), (0, 0)))


def _layer_params(l, p):
    aw = 3 * A_WIDTH
    mu = p["rwkv_mu"][l]
    rows = [p["rwkv_w0"][l], p["rwkv_a0"][l], p["rwkv_k_k"][l], p["rwkv_k_a"][l], p["rwkv_r_k"][l].reshape(-1),
            p["rwkv_ln_w"][l], p["rwkv_ln_b"][l],
            (p["rwkv_v0"][l - 1] if l > 0 else jnp.zeros((A_WIDTH,), F32)),
            mu[:A_WIDTH], mu[A_WIDTH:2 * A_WIDTH], mu[2 * A_WIDTH:aw]]
    rwkv_prm = jnp.pad(jnp.stack(rows), ((0, 16 - len(rows)), (0, 0)))
    mu2 = jnp.concatenate([_pad_cols(mu[None, aw + 128:], 256), mu[None, aw:aw + 128]], axis=1)
    v2 = (p["rwkv_v2"][l - 1] if l > 0 else jnp.zeros((A_VRES_LORA, A_WIDTH), F32))
    bcast = lambda t: jnp.repeat(t, C_HEAD_K)
    gdn_prm = jnp.pad(jnp.stack([bcast(p["gdn_A_log"][l]), bcast(p["gdn_dt_bias"][l]),
                                 jnp.tile(p["gdn_norm"][l], C_HEADS)]), ((0, SUBLANES - 3), (0, 0)))
    fc = p["ffn_conv"][l]
    return dict(
        v1p=(_pad_cols(p["rwkv_v1"][l - 1], LANES).astype(BF16) if l > 0 else None),
        rwkv_prm=rwkv_prm, mu2=mu2,
        w2p=_pad_rows(p["rwkv_w2"][l], 128, 0).astype(BF16),
        a2p=_pad_rows(p["rwkv_a2"][l], 128, A_DECAY_LORA).astype(BF16),
        g2p=_pad_rows(p["rwkv_g2"][l], 256, 0).astype(BF16),
        v2p=_pad_rows(v2, 128, 0).astype(BF16),
        gdn_conv=p["gdn_conv"][l].reshape(C_CONV, 3, C_KW).transpose(1, 0, 2),
        gdn_prm=gdn_prm,
        pa=p["proj_a"][l].astype(BF16), pb=p["proj_b"][l].astype(BF16), pc=p["proj_c"][l].astype(BF16),
        ffn_conv=fc.reshape(FFN_CONV, 2, D_FF // FFN_TF, FFN_TF).transpose(2, 0, 1, 3).reshape(
            D_FF // FFN_TF, FFN_CONV, 2 * FFN_TF),
    )


def _forward(x, p):
    batch, seq, d = x.shape
    m = batch * seq
    xf = x.reshape(m, d)
    tab = rope_tables(seq)
    v_first = None
    stack_rows = lambda w: w.astype(BF16).reshape(w.shape[0] * w.shape[1], w.shape[2])
    w_out_all, ffn_up_all, ffn_down_all = stack_rows(p["w_out"]), stack_rows(p["ffn_up"]), stack_rows(p["ffn_down"])
    h = rmsnorm(xf, p["attn_norm"][0], BF16)
    for l in range(DEPTH):
        lp = _layer_params(l, p)
        wt = prep_w_in(p["w_in"], l)
        za = matmul_nt(h, wt, WT_ZA, ZA_W, tn=512, name="in_proj_a")
        zb = matmul_nt(h, wt, WT_ZB, B_IN, tn=512, name="in_proj_b")
        zc = matmul_nt(h, wt, WT_ZC, ZC_W, tn=1408, tm=1024, name="in_proj_c")
        zg = matmul_nt(h, wt, WT_ZG, 3 * D_MODEL, tn=1024, name="in_proj_g")
        lv = matmul(h, lp["v1p"], tn=LANES, tm=2048, name="vres_lora") if l > 0 else None
        ya, v_l = rwkv_mix(za, lv, v_first, lp["rwkv_prm"], lp["mu2"], lp["w2p"], lp["a2p"], lp["g2p"], lp["v2p"],
                           batch=batch, seq=seq)
        if l == 0:
            v_first = v_l
        attn = [dilated_attention_group(zb, tab, gi, batch=batch, seq=seq) for gi in range(B_GROUPS)]
        yc = gated_deltanet(zc, lp["gdn_conv"], lp["gdn_prm"], batch=batch, seq=seq)
        merged = merge_mixers(ya, attn, yc, zg, lp["pa"], lp["pb"], lp["pc"])
        xf, h2 = matmul_res_norm(merged, w_out_all, l, xf, p["ffn_norm"][l], tm=512, norm_dtype=BF16,
                                 emit_sum=True, name="out_proj_norm")
        act = ffn_up_act(h2, ffn_up_all, l, lp["ffn_conv"], seq=seq)
        if l + 1 < DEPTH:
            xf, h = matmul_res_norm(act, ffn_down_all, l, xf, p["attn_norm"][l + 1], tm=256, norm_dtype=BF16,
                                    emit_sum=True, name="ffn_down_norm")
        else:
            out = matmul_res_norm(act, ffn_down_all, l, xf, p["final_norm"], tm=256, norm_dtype=F32,
                                  emit_sum=False, name="ffn_down_norm")
    return out.reshape(batch, seq, d)


def kernel(x, attn_norm, w_in, rwkv_mu, rwkv_w0, rwkv_w2, rwkv_a0, rwkv_a2, rwkv_g2, rwkv_k_k, rwkv_k_a, rwkv_r_k, rwkv_ln_w, rwkv_ln_b, rwkv_v0, rwkv_v1, rwkv_v2, gdn_conv, gdn_A_log, gdn_dt_bias, gdn_norm, proj_a, proj_b, proj_c, w_out, ffn_norm, ffn_up, ffn_conv, ffn_down, final_norm):
    params = dict(
        attn_norm=attn_norm, w_in=w_in, rwkv_mu=rwkv_mu, rwkv_w0=rwkv_w0, rwkv_w2=rwkv_w2, rwkv_a0=rwkv_a0,
        rwkv_a2=rwkv_a2, rwkv_g2=rwkv_g2, rwkv_k_k=rwkv_k_k, rwkv_k_a=rwkv_k_a, rwkv_r_k=rwkv_r_k,
        rwkv_ln_w=rwkv_ln_w, rwkv_ln_b=rwkv_ln_b, rwkv_v0=rwkv_v0, rwkv_v1=rwkv_v1, rwkv_v2=rwkv_v2,
        gdn_conv=gdn_conv, gdn_A_log=gdn_A_log, gdn_dt_bias=gdn_dt_bias, gdn_norm=gdn_norm, proj_a=proj_a,
        proj_b=proj_b, proj_c=proj_c, w_out=w_out, ffn_norm=ffn_norm, ffn_up=ffn_up, ffn_conv=ffn_conv,
        ffn_down=ffn_down, final_norm=final_norm)
    return _forward(x, params)
```

```python
import functools

import jax
import jax.numpy as jnp
from jax import lax
from jax.experimental import pallas as pl
from jax.experimental.pallas import tpu as pltpu

F32 = jnp.float32
BF16 = jnp.bfloat16

D_MODEL = 2048
DEPTH = 2
NORM_EPS = 1e-6

A_HEADS, A_HEAD = 16, 64
A_WIDTH = A_HEADS * A_HEAD
A_DECAY_LORA, A_ICLR_LORA, A_GATE_LORA, A_VRES_LORA = 64, 64, 160, 32
A_GN_EPS = 64e-5
A_IN = 3 * A_WIDTH + A_DECAY_LORA + A_ICLR_LORA + A_GATE_LORA

B_PAIRS = ((128, 1), (512, 4), (2048, 16))
B_GROUPS = 3
B_HEADS_PER_GROUP, B_HEAD = 4, 128
B_WIDTH = B_GROUPS * B_HEADS_PER_GROUP * B_HEAD
B_OUT = B_HEADS_PER_GROUP * B_HEAD
B_IN = 3 * B_WIDTH
ROPE_THETA = 500000.0
ROPE_DIM = B_HEAD // 4

C_HEADS, C_HEAD_K, C_HEAD_V = 8, 128, 128
C_KW = C_HEADS * C_HEAD_K
C_VW = C_HEADS * C_HEAD_V
C_CONV = 4
C_IN = 2 * C_KW + C_VW + 2 * C_HEADS + C_VW

D_FF = 5632
FFN_CONV = 3

LANES = 128
SUBLANES = 8
CHUNK = 64
FFN_TF = 1408
VMEM_LIMIT = 56 * 1024 * 1024

ZA_GD = 3 * A_WIDTH
ZA_WA = ZA_GD + 256
ZA_W = ZA_WA + 256
ZC_BA = 3 * C_KW + C_VW
ZC_W = ZC_BA + 128


def _cparams(sem):
    return pltpu.CompilerParams(dimension_semantics=sem, vmem_limit_bytes=VMEM_LIMIT)


def _dot(a, b):
    return jnp.dot(a.astype(BF16), b.astype(BF16), preferred_element_type=F32)


def _dot_nt(a, b):
    return lax.dot_general(a.astype(BF16), b.astype(BF16), (((1,), (1,)), ((), ())), preferred_element_type=F32)


def _dot_tn(a, b):
    return lax.dot_general(a.astype(BF16), b.astype(BF16), (((0,), (0,)), ((), ())), preferred_element_type=F32)


def _split(a):
    hi = a.astype(BF16)
    lo = (a - hi.astype(F32)).astype(BF16)
    return hi, lo


def _dot_sel_l(sel, a):
    hi, lo = _split(a)
    return (jnp.dot(sel, hi, preferred_element_type=F32) + jnp.dot(sel, lo, preferred_element_type=F32))


def _sigmoid(x):
    return 1.0 / (1.0 + jnp.exp(-x))


def _softplus(x):
    return jnp.maximum(x, 0.0) + jnp.log(1.0 + jnp.exp(-jnp.abs(x)))


def _iota2(shape, axis):
    return lax.broadcasted_iota(jnp.int32, shape, axis)


def _chunk_of(idx):
    return jnp.right_shift(idx, CHUNK.bit_length() - 1)


def _chunk_tri(n):
    ri, ci = _iota2((n, n), 0), _iota2((n, n), 1)
    return jnp.where((_chunk_of(ri) == _chunk_of(ci)) & (ri >= ci), 1.0, 0.0).astype(BF16)


def _neumann_inverse_many(n_mats, eye):
    ps = [eye + n for n in n_mats]
    qs = [_dot(n, n) for n in n_mats]
    levels = CHUNK.bit_length() - 2
    for lvl in range(levels):
        if lvl == levels - 1:
            ps = [p + _dot(p, q) for p, q in zip(ps, qs)]
        else:
            prods = [_dot(jnp.concatenate([q, p], axis=0), q) for p, q in zip(ps, qs)]
            ps = [p + pr[LANES:] for p, pr in zip(ps, prods)]
            qs = [pr[0:LANES] for pr in prods]
    return ps


def _neumann_inverse(n_mat, eye):
    return _neumann_inverse_many([n_mat], eye)[0]


def _rmsnorm_kernel(x_ref, g_ref, o_ref):
    x = x_ref[...]
    ms = jnp.mean(x * x, axis=-1, keepdims=True)
    o_ref[...] = ((x * lax.rsqrt(ms + NORM_EPS)) * g_ref[...]).astype(o_ref.dtype)


def rmsnorm(x, g, out_dtype, tm=512):
    m, d = x.shape
    return pl.pallas_call(
        _rmsnorm_kernel,
        grid=(m // tm,),
        in_specs=[pl.BlockSpec((tm, d), lambda i: (i, 0)), pl.BlockSpec((1, d), lambda i: (0, 0))],
        out_specs=pl.BlockSpec((tm, d), lambda i: (i, 0)),
        out_shape=jax.ShapeDtypeStruct((m, d), out_dtype),
        compiler_params=_cparams(("parallel",)),
        name="rmsnorm",
    )(x, g.reshape(1, d))


def _matmul_kernel(x_ref, w_ref, o_ref):
    o_ref[...] = jnp.dot(x_ref[...], w_ref[...], preferred_element_type=F32).astype(o_ref.dtype)


def _matmul_nt_kernel(x_ref, wt_ref, o_ref):
    o_ref[...] = lax.dot_general(x_ref[...], wt_ref[...], (((1,), (1,)), ((), ())), preferred_element_type=F32)


def matmul(x, w, tn, tm=1024, name="matmul"):
    m, k = x.shape
    n = w.shape[1]
    tm = min(tm, m)
    return pl.pallas_call(
        _matmul_kernel,
        grid=(m // tm, n // tn),
        in_specs=[pl.BlockSpec((tm, k), lambda i, j: (i, 0)), pl.BlockSpec((k, tn), lambda i, j: (0, j))],
        out_specs=pl.BlockSpec((tm, tn), lambda i, j: (i, j)),
        out_shape=jax.ShapeDtypeStruct((m, n), F32),
        compiler_params=_cparams(("parallel", "parallel")),
        name=name,
    )(x, w)


def _matmul_res_norm_kernel(x_ref, w_ref, r_ref, g_ref, *out_refs):
    y = r_ref[...] + jnp.dot(x_ref[...], w_ref[...], preferred_element_type=F32)
    ms = jnp.mean(y * y, axis=-1, keepdims=True)
    hn = (y * lax.rsqrt(ms + NORM_EPS)) * g_ref[...]
    if len(out_refs) == 2:
        out_refs[0][...] = y
    out_refs[-1][...] = hn.astype(out_refs[-1].dtype)


def matmul_res_norm(x, w, layer, residual, gain, *, tm, norm_dtype, emit_sum, name):
    m, k = x.shape
    n = w.shape[1]
    rows = lambda dt: (pl.BlockSpec((tm, n), lambda i: (i, 0)), jax.ShapeDtypeStruct((m, n), dt))
    outs = ([rows(F32)] if emit_sum else []) + [rows(norm_dtype)]
    res = pl.pallas_call(
        _matmul_res_norm_kernel,
        grid=(m // tm,),
        in_specs=[pl.BlockSpec((tm, k), lambda i: (i, 0)),
                  pl.BlockSpec((k, n), lambda i: (layer, 0), pipeline_mode=pl.Buffered(1)),
                  pl.BlockSpec((tm, n), lambda i: (i, 0)),
                  pl.BlockSpec((1, n), lambda i: (0, 0))],
        out_specs=[o[0] for o in outs],
        out_shape=[o[1] for o in outs],
        compiler_params=_cparams(("parallel",)),
        name=name,
    )(x, w, residual, gain.reshape(1, n))
    return tuple(res) if emit_sum else res[0]


def _rwkv_kernel(r_ref, k_ref, v_ref, gd_ref, wa_ref, lv_ref, vf_ref, p_ref, mu2_ref, w2_ref, a2_ref, g2_ref,
                 v2_ref, *out_and_scratch, tb, pp, has_vmix):
    *out_refs, s_ref, y_ref, br_ref, bk_ref, bv_ref, bgd_ref, bwa_ref = out_and_scratch
    i = pl.program_id(2)
    h = SUBLANES

    @pl.when(i == 0)
    def _():
        s_ref[...] = jnp.zeros_like(s_ref)
        for b in (br_ref, bk_ref, bv_ref, bgd_ref, bwa_ref):
            b[0:h, :] = jnp.zeros((h, b.shape[1]), F32)

    prm = p_ref[...]
    w0, a0, k_k, k_a, r_k, ln_w, ln_b, v0 = [prm[j:j + 1] for j in range(8)]
    mu_r, mu_k, mu_v = prm[8:9], prm[9:10], prm[10:11]
    mu2 = mu2_ref[...]
    mu_gd, mu_wa = mu2[:, 0:256], mu2[:, 256:384]

    def shifted_mix(x_ref, buf_ref, mu):
        x = x_ref[...]
        buf_ref[h:h + tb, :] = x
        xs = buf_ref[h - 1:h - 1 + tb, :]
        buf_ref[0:h, :] = x[tb - h:tb, :]
        return x + (xs - x) * mu

    r = shifted_mix(r_ref, br_ref, mu_r)
    k = shifted_mix(k_ref, bk_ref, mu_k)
    v = shifted_mix(v_ref, bv_ref, mu_v)
    gd = shifted_mix(gd_ref, bgd_ref, mu_gd)
    wa = shifted_mix(wa_ref, bwa_ref, mu_wa)

    lane = _iota2((1, LANES), 1)
    m0 = jnp.where(lane < A_HEAD, 1.0, 0.0)
    m1 = 1.0 - m0
    ri = _iota2((LANES, LANES), 0)
    ci = _iota2((LANES, LANES), 1)
    same = _chunk_of(ri) == _chunk_of(ci)
    strict = same & (ri > ci)
    incl = same & (ri >= ci)
    eye = jnp.where(ri == ci, 1.0, 0.0)
    tri = _chunk_tri(tb)
    pair_lanes = [slice(q * LANES, (q + 1) * LANES) for q in range(pp)]

    def head_sum(x):
        outs = []
        for ls in pair_lanes:
            s0 = jnp.sum(x[:, ls] * m0, axis=-1, keepdims=True)
            s1 = jnp.sum(x[:, ls] * m1, axis=-1, keepdims=True)
            outs.append(jnp.where(lane < A_HEAD, s0, s1))
        return jnp.concatenate(outs, axis=1)

    lw = -jnp.exp(-0.5) * _sigmoid(w0 + _dot(jnp.tanh(wa), w2_ref[...]))
    a = _sigmoid(a0 + _dot(wa, a2_ref[...]))
    g = _dot(_sigmoid(gd), g2_ref[...])
    kk = k * k_k
    kk = kk / jnp.maximum(jnp.sqrt(head_sum(kk * kk)), 1e-12)
    k = k * (1.0 + (a - 1.0) * k_a)
    if has_vmix:
        v_mix = _sigmoid(v0 + _dot(lv_ref[...], v2_ref[...]))
        v = v + (vf_ref[...] - v) * v_mix
    if len(out_refs) == 2:
        out_refs[1][...] = v

    cum = _dot_sel_l(tri, lw)
    e_pos = jnp.exp(cum)
    e_neg = jnp.exp(-cum)
    r_t = r * e_pos
    a_t = -kk * jnp.exp(cum - lw)
    b_t = (kk * a) * e_neg
    k_t = k * e_neg

    def stack_masked(x):
        return jnp.concatenate([x * m0, x * m1], axis=0)

    def stack_dup(x):
        return jnp.concatenate([x, x], axis=0)

    kb = kk * a
    nchunk = tb // CHUNK
    inst = [(c, q) for c in range(nchunk) for q in range(pp)]
    rows_of = lambda c: slice(c * CHUNK, (c + 1) * CHUNK)
    last_of = lambda c: slice((c + 1) * CHUNK - 1, (c + 1) * CHUNK)
    a_s = {cq: stack_masked(a_t[rows_of(cq[0]), pair_lanes[cq[1]]]) for cq in inst}
    r_s = {cq: stack_masked(r_t[rows_of(cq[0]), pair_lanes[cq[1]]]) for cq in inst}
    v_s = {cq: stack_masked(v[rows_of(cq[0]), pair_lanes[cq[1]]]) for cq in inst}
    gm = {}
    for c, q in inst:
        sl, ls = rows_of(c), pair_lanes[q]
        right = jnp.concatenate([stack_dup(b_t[sl, ls]), stack_dup(k_t[sl, ls])], axis=0)
        gm[c, q] = _dot_nt(jnp.concatenate([a_s[c, q], r_s[c, q]], axis=0), right)
    n_ab = [jnp.where(strict, gm[cq][0:LANES, 0:LANES], 0.0) for cq in inst]
    t_inv = dict(zip(inst, _neumann_inverse_many(n_ab, eye)))
    gv = {cq: _dot(jnp.where(strict, gm[cq][0:LANES, LANES:], 0.0), v_s[cq]) for cq in inst}
    tu = {cq: _dot(t_inv[cq], jnp.concatenate([a_s[cq], gv[cq]], axis=1)) for cq in inst}
    g_r = {cq: jnp.where(jnp.concatenate([incl, incl], axis=1), gm[cq][LANES:], 0.0) for cq in inst}

    states = [s_ref[q] for q in range(pp)]
    for c in range(nchunk):
        sl = rows_of(c)
        x0 = [_dot_nt(jnp.concatenate([tu[c, q][:, 0:LANES], r_s[c, q]], axis=0), states[q]) for q in range(pp)]
        uv = [jnp.concatenate([x0[q][0:LANES] + tu[c, q][:, LANES:], v_s[c, q]], axis=0) for q in range(pp)]
        for q, ls in enumerate(pair_lanes):
            cum_last = cum[last_of(c), ls]
            e_end = jnp.exp(cum_last - cum[sl, ls])
            bk_end = jnp.concatenate([stack_dup(kb[sl, ls] * e_end), stack_dup(k[sl, ls] * e_end)], axis=0)
            states[q] = jnp.where(same, states[q] * jnp.exp(cum_last) + _dot_tn(uv[q], bk_end), 0.0)
        for q, ls in enumerate(pair_lanes):
            y_s = x0[q][LANES:] + _dot(g_r[c, q], uv[q])
            y_ref[sl, ls] = y_s[0:CHUNK] + y_s[CHUNK:]
    for q in range(pp):
        s_ref[q] = states[q]

    y = y_ref[...]
    inv_n = 1.0 / A_HEAD
    mean = head_sum(y) * inv_n
    d = y - mean
    var = head_sum(d * d) * inv_n
    yn = d * lax.rsqrt(var + A_GN_EPS) * ln_w + ln_b
    bonus = head_sum(r * k * r_k) * v
    out_refs[0][...] = ((yn + bonus) * g).astype(out_refs[0].dtype)


def rwkv_mix(za, lv, v_first, prm, mu2, w2p, a2p, g2p, v2p, *, batch, seq, tb=256, pp=4):
    has_vmix = v_first is not None
    nt = seq // tb
    w = pp * LANES
    npair = A_WIDTH // w
    row = lambda b, p, i: b * nt + i
    if v_first is None:
        v_first = lv = za
    in_specs = [
        pl.BlockSpec((tb, w), lambda b, p, i: (row(b, p, i), p)),
        pl.BlockSpec((tb, w), lambda b, p, i: (row(b, p, i), npair + p)),
        pl.BlockSpec((tb, w), lambda b, p, i: (row(b, p, i), 2 * npair + p)),
        pl.BlockSpec((tb, 256), lambda b, p, i: (row(b, p, i), ZA_GD // 256)),
        pl.BlockSpec((tb, LANES), lambda b, p, i: (row(b, p, i), ZA_WA // LANES)),
        pl.BlockSpec((tb, LANES), lambda b, p, i: (row(b, p, i), 0)),
        pl.BlockSpec((tb, w), lambda b, p, i: (row(b, p, i), p)),
        pl.BlockSpec((16, w), lambda b, p, i: (0, p)),
        pl.BlockSpec((1, 384), lambda b, p, i: (0, 0)),
        pl.BlockSpec((LANES, w), lambda b, p, i: (0, p)),
        pl.BlockSpec((LANES, w), lambda b, p, i: (0, p)),
        pl.BlockSpec((256, w), lambda b, p, i: (0, p)),
        pl.BlockSpec((LANES, w), lambda b, p, i: (0, p)),
    ]
    out_spec = pl.BlockSpec((tb, w), lambda b, p, i: (row(b, p, i), p))
    out_dtypes = [BF16] if has_vmix else [BF16, F32]
    res = pl.pallas_call(
        functools.partial(_rwkv_kernel, tb=tb, pp=pp, has_vmix=has_vmix),
        grid=(batch, npair, nt),
        in_specs=in_specs,
        out_specs=[out_spec] * len(out_dtypes),
        out_shape=[jax.ShapeDtypeStruct((batch * seq, A_WIDTH), dt) for dt in out_dtypes],
        scratch_shapes=[
            pltpu.VMEM((pp, LANES, LANES), F32),
            pltpu.VMEM((tb, w), F32),
            pltpu.VMEM((tb + SUBLANES, w), F32),
            pltpu.VMEM((tb + SUBLANES, w), F32),
            pltpu.VMEM((tb + SUBLANES, w), F32),
            pltpu.VMEM((tb + SUBLANES, 256), F32),
            pltpu.VMEM((tb + SUBLANES, LANES), F32),
        ],
        compiler_params=_cparams(("parallel", "parallel", "arbitrary")),
        name="rwkv7",
    )(za, za, za, za, za, lv, v_first, prm, mu2, w2p, a2p, g2p, v2p)
    return (res[0], None) if has_vmix else (res[0], res[1])


def _attn_kernel(*refs, tq, dil, nb, has_prev):
    if has_prev:
        (q_ref, kc_ref, kp_ref, vc_ref, vp_ref, cc_ref, sc_ref, nc_ref, cp_ref, sp_ref, np_ref,
         o_ref, lse_ref) = refs
    else:
        q_ref, kc_ref, vc_ref, cc_ref, sc_ref, nc_ref, o_ref, lse_ref = refs
    i = pl.program_id(1)
    scale = B_HEAD ** -0.5

    def rows(j, r):
        start = j * tq * dil + r
        return pl.ds(start, tq, stride=dil) if dil > 1 else pl.ds(start, tq)

    def rope(x, tabs):
        half = ROPE_DIM // 2
        return (x * tabs[0] + pltpu.roll(x, half, axis=1) * tabs[1]
                + pltpu.roll(x, LANES - half, axis=1) * tabs[2])

    if has_prev:
        ri, ci = _iota2((tq, 2 * tq), 0), _iota2((tq, 2 * tq), 1)
        cur_ok = (ci >= tq) & ((ci - tq) <= ri)
        no_prev = jnp.where(i > 0, 0, 2 * tq)
        valid_first = ((ci < tq) & (ci >= ri + no_prev)) | cur_ok
        valid_inner = ((ci < tq) & (ci >= ri)) | cur_ok
    else:
        causal = _iota2((tq, tq), 1) <= _iota2((tq, tq), 0)

    inst = [(r, j) for r in range(dil) for j in range(nb)]
    qs, ks, vs = {}, {}, {}
    for r in range(dil):
        if has_prev:
            rp = rows(0, r)
            ks[r, -1] = rope(kp_ref[rp, :], (cp_ref[rp, :], sp_ref[rp, :], np_ref[rp, :]))
            vs[r, -1] = vp_ref[rp, :]
        for j in range(nb):
            rw = rows(j, r)
            tabs = (cc_ref[rw, :], sc_ref[rw, :], nc_ref[rw, :])
            qs[r, j] = rope(q_ref[rw, :], tabs) * scale
            ks[r, j] = rope(kc_ref[rw, :], tabs)
            vs[r, j] = vc_ref[rw, :]
    if has_prev:
        s = {(r, j): jnp.where(valid_first if j == 0 else valid_inner,
                               _dot_nt(qs[r, j], jnp.concatenate([ks[r, j - 1], ks[r, j]], axis=0)), -1e30)
             for r, j in inst}
    else:
        s = {rj: jnp.where(causal, _dot_nt(qs[rj], ks[rj]), -1e30) for rj in inst}
    m = {rj: jnp.max(s[rj], axis=-1, keepdims=True) for rj in inst}
    p = {rj: jnp.exp(s[rj] - m[rj]) for rj in inst}
    den = {rj: jnp.sum(p[rj], axis=-1, keepdims=True) for rj in inst}
    if has_prev:
        num = {(r, j): _dot(p[r, j], jnp.concatenate([vs[r, j - 1], vs[r, j]], axis=0)) for r, j in inst}
    else:
        num = {rj: _dot(p[rj], vs[rj]) for rj in inst}
    for r, j in inst:
        o_ref[rows(j, r), :] = num[r, j] / den[r, j]
        lse_ref[rows(j, r), :] = jnp.broadcast_to(m[r, j] + jnp.log(den[r, j]), (tq, B_HEAD))


def dilated_attention_group(zb, rope_tabs, gi, *, batch, seq, tq=128, blocks_per_step=16):
    win, dil = B_PAIRS[gi]
    assert win // dil == tq
    nq = seq // (dil * tq)
    has_prev = nq > 1
    nb = min(nq, max(1, blocks_per_step // dil))
    rows_step = nb * tq * dil
    nsteps = seq // rows_step
    prev_rows = tq * dil
    nprev = seq // prev_rows
    nh = B_HEADS_PER_GROUP
    cur = lambda col: pl.BlockSpec((rows_step, B_HEAD), lambda b, i, hh: (b * nsteps + i, col * nh + hh))
    prv = lambda col: pl.BlockSpec(
        (prev_rows, B_HEAD), lambda b, i, hh: (b * nprev + jnp.maximum(i * nb - 1, 0), col * nh + hh))
    tab_cur = pl.BlockSpec((rows_step, LANES), lambda b, i, hh: (i, 0))
    tab_prv = pl.BlockSpec((prev_rows, LANES), lambda b, i, hh: (jnp.maximum(i * nb - 1, 0), 0))
    qc, kc, vc = gi, B_GROUPS + gi, 2 * B_GROUPS + gi
    if has_prev:
        in_specs = [cur(qc), cur(kc), prv(kc), cur(vc), prv(vc)] + [tab_cur] * 3 + [tab_prv] * 3
        args = (zb, zb, zb, zb, zb) + tuple(rope_tabs) * 2
    else:
        in_specs = [cur(qc), cur(kc), cur(vc)] + [tab_cur] * 3
        args = (zb, zb, zb) + tuple(rope_tabs)
    out_spec = pl.BlockSpec((rows_step, B_HEAD), lambda b, i, hh: (b * nsteps + i, hh))
    out_sds = jax.ShapeDtypeStruct((batch * seq, B_OUT), F32)
    return pl.pallas_call(
        functools.partial(_attn_kernel, tq=tq, dil=dil, nb=nb, has_prev=has_prev),
        grid=(batch, nsteps, nh),
        in_specs=in_specs,
        out_specs=[out_spec, out_spec],
        out_shape=[out_sds, out_sds],
        compiler_params=_cparams(("parallel", "parallel", "arbitrary")),
        name=f"dilated_attn_g{gi}",
    )(*args)


def rope_tables(seq):
    half = ROPE_DIM // 2
    inv = ROPE_THETA ** (-jnp.arange(half, dtype=F32) / half)
    ang = jnp.arange(seq, dtype=F32)[:, None] * inv[None, :]
    cos, sin = jnp.cos(ang), jnp.sin(ang)
    z = jnp.zeros((seq, LANES - ROPE_DIM), F32)
    zh = jnp.zeros((seq, half), F32)
    c_tab = jnp.concatenate([cos, cos, jnp.ones_like(z)], axis=1)
    s_pos = jnp.concatenate([zh, sin, z], axis=1)
    s_neg = jnp.concatenate([-sin, zh, z], axis=1)
    return c_tab, s_pos, s_neg


def _gdn_kernel(q_ref, k_ref, v_ref, gate_ref, ba_ref, cw_ref, p_ref, out_ref, s_ref, o_ref, bq_ref, bk_ref,
                bv_ref, *, tb, pp):
    i = pl.program_id(2)
    h = SUBLANES
    hd = C_HEAD_K

    @pl.when(i == 0)
    def _():
        s_ref[...] = jnp.zeros_like(s_ref)
        for b in (bq_ref, bk_ref, bv_ref):
            b[0:h, :] = jnp.zeros((h, b.shape[1]), F32)

    cw = cw_ref[...]
    prm = p_ref[...]

    def conv_silu(x_ref, buf_ref, w):
        x = x_ref[...]
        buf_ref[h:h + tb, :] = x
        acc = x * w[C_CONV - 1:C_CONV]
        for j in range(C_CONV - 1):
            off = h - (C_CONV - 1) + j
            acc = acc + buf_ref[off:off + tb, :] * w[j:j + 1]
        buf_ref[0:h, :] = x[tb - h:tb, :]
        return acc * _sigmoid(acc)

    q = conv_silu(q_ref, bq_ref, cw[0])
    k = conv_silu(k_ref, bk_ref, cw[1])
    v = conv_silu(v_ref, bv_ref, cw[2])

    ri = _iota2((LANES, LANES), 0)
    ci = _iota2((LANES, LANES), 1)
    same = _chunk_of(ri) == _chunk_of(ci)
    strict = same & (ri > ci)
    incl = same & (ri >= ci)
    eye = jnp.where(ri == ci, 1.0, 0.0)
    tri = _chunk_tri(tb)

    ba = ba_ref[...]
    nh = 2 * pp
    head_lanes = [slice(hh * hd, (hh + 1) * hd) for hh in range(nh)]
    vh = [v[:, ls] for ls in head_lanes]
    q_ss = [jnp.sum(q[:, ls] * q[:, ls], axis=-1, keepdims=True) for ls in head_lanes]
    k_ss = [jnp.sum(k[:, ls] * k[:, ls], axis=-1, keepdims=True) for ls in head_lanes]
    b_raw = [ba[:, hh:hh + 1] for hh in range(nh)]
    a_raw = [ba[:, C_HEADS + hh:C_HEADS + hh + 1] for hh in range(nh)]
    qh = [q[:, ls] / jnp.maximum(jnp.sqrt(ss), 1e-12) * (hd ** -0.5) for ls, ss in zip(head_lanes, q_ss)]
    kh = [k[:, ls] / jnp.maximum(jnp.sqrt(ss), 1e-12) for ls, ss in zip(head_lanes, k_ss)]
    beta = [_sigmoid(b) for b in b_raw]
    glog = [-jnp.exp(prm[0:1, ls]) * _softplus(al + prm[1:2, ls]) for ls, al in zip(head_lanes, a_raw)]
    gam = [_dot_sel_l(tri, gl) for gl in glog]

    nchunk = tb // CHUNK
    inst = [(c, pr) for c in range(nchunk) for pr in range(pp)]
    rows_of = lambda c: slice(c * CHUNK, (c + 1) * CHUNK)

    def stack(xs, cp):
        c, pr = cp
        return jnp.concatenate([xs[2 * pr][rows_of(c)], xs[2 * pr + 1][rows_of(c)]], axis=0)

    k_s = {cp: stack(kh, cp) for cp in inst}
    q_s = {cp: stack(qh, cp) for cp in inst}
    beta_s = {cp: stack(beta, cp) for cp in inst}
    gam_s = {cp: stack(gam, cp) for cp in inst}
    kq = {cp: _dot_nt(jnp.concatenate([k_s[cp], q_s[cp]], axis=0), k_s[cp]) for cp in inst}
    dm = {cp: jnp.exp(jnp.where(incl, gam_s[cp] - gam_s[cp].T, -1e30)) for cp in inst}
    n_mats = [jnp.where(strict, -(beta_s[cp] * kq[cp][0:LANES] * dm[cp]), 0.0) for cp in inst]
    t_inv = dict(zip(inst, _neumann_inverse_many(n_mats, eye)))
    e_gam = {cp: jnp.exp(gam_s[cp]) for cp in inst}
    sol = {cp: _dot(t_inv[cp], jnp.concatenate([stack(vh, cp) * beta_s[cp],
                                                k_s[cp] * (beta_s[cp] * e_gam[cp])], axis=1)) for cp in inst}

    states = [s_ref[hh] for hh in range(nh)]
    for c in range(nchunk):
        sl = rows_of(c)
        ws = []
        for hh in range(nh):
            cp, hs = (c, hh // 2), slice((hh % 2) * CHUNK, (hh % 2 + 1) * CHUNK)
            qg = (q_s[cp] * e_gam[cp])[hs]
            ws.append(_dot(jnp.concatenate([sol[cp][hs, hd:], qg], axis=0), states[hh]))
        v_new = [sol[c, hh // 2][(hh % 2) * CHUNK:(hh % 2 + 1) * CHUNK, 0:hd] - ws[hh][0:CHUNK]
                 for hh in range(nh)]
        for hh in range(nh):
            g_h = gam[hh][sl]
            g_last = g_h[CHUNK - 1:CHUNK, :]
            states[hh] = states[hh] * jnp.exp(g_last) + _dot_tn(kh[hh][sl] * jnp.exp(g_last - g_h), v_new[hh])
        for pr in range(pp):
            h0 = 2 * pr
            attn = kq[c, pr][LANES:] * dm[c, pr]
            o_s = (jnp.concatenate([ws[h0][CHUNK:], ws[h0 + 1][CHUNK:]], axis=0)
                   + _dot(attn, jnp.concatenate([v_new[h0], v_new[h0 + 1]], axis=0)))
            o_ref[sl, h0 * hd:(h0 + 1) * hd] = o_s[0:CHUNK]
            o_ref[sl, (h0 + 1) * hd:(h0 + 2) * hd] = o_s[CHUNK:]
    for hh in range(nh):
        s_ref[hh] = states[hh]

    gate = gate_ref[...]
    o_h = [o_ref[:, ls] for ls in head_lanes]
    o_ms = [jnp.mean(o * o, axis=-1, keepdims=True) for o in o_h]
    for o, ms, ls in zip(o_h, o_ms, head_lanes):
        gt = gate[:, ls]
        out_ref[:, ls] = ((o * lax.rsqrt(ms + NORM_EPS) * prm[2:3, ls]) * (gt * _sigmoid(gt))).astype(out_ref.dtype)


def gated_deltanet(zc, conv_w, prm, *, batch, seq, tb=256, pp=4):
    assert 2 * pp == C_HEADS, "the kernel indexes the per-head beta/alpha columns statically"
    nt = seq // tb
    npair = C_HEADS // (2 * pp)
    wblk = 2 * pp * C_HEAD_K
    row = lambda b, p, i: b * nt + i
    in_specs = [
        pl.BlockSpec((tb, wblk), lambda b, p, i: (row(b, p, i), p)),
        pl.BlockSpec((tb, wblk), lambda b, p, i: (row(b, p, i), npair + p)),
        pl.BlockSpec((tb, wblk), lambda b, p, i: (row(b, p, i), 2 * npair + p)),
        pl.BlockSpec((tb, wblk), lambda b, p, i: (row(b, p, i), 3 * npair + p)),
        pl.BlockSpec((tb, LANES), lambda b, p, i: (row(b, p, i), ZC_BA // LANES)),
        pl.BlockSpec((3, C_CONV, wblk), lambda b, p, i: (0, 0, p)),
        pl.BlockSpec((SUBLANES, wblk), lambda b, p, i: (0, p)),
    ]
    return pl.pallas_call(
        functools.partial(_gdn_kernel, tb=tb, pp=pp),
        grid=(batch, npair, nt),
        in_specs=in_specs,
        out_specs=pl.BlockSpec((tb, wblk), lambda b, p, i: (row(b, p, i), p)),
        out_shape=jax.ShapeDtypeStruct((batch * seq, C_VW), BF16),
        scratch_shapes=[
            pltpu.VMEM((2 * pp, C_HEAD_K, C_HEAD_V), F32),
            pltpu.VMEM((tb, wblk), F32),
            pltpu.VMEM((tb + SUBLANES, wblk), F32),
            pltpu.VMEM((tb + SUBLANES, wblk), F32),
            pltpu.VMEM((tb + SUBLANES, wblk), F32),
        ],
        compiler_params=_cparams(("parallel", "parallel", "arbitrary")),
        name="gated_deltanet",
    )(zc, zc, zc, zc, zc, conv_w, prm)


def _merge_kernel(ya_ref, o0_ref, o1_ref, o2_ref, l0_ref, l1_ref, l2_ref, yc_ref, ga_ref, gb_ref, gc_ref,
                  pa_ref, pb_ref, pc_ref, out_ref):
    l0, l1, l2 = l0_ref[...], l1_ref[...], l2_ref[...]
    m = jnp.maximum(jnp.maximum(l0, l1), l2)
    w0, w1, w2 = jnp.exp(l0 - m), jnp.exp(l1 - m), jnp.exp(l2 - m)
    yb = (w0 * o0_ref[...] + w1 * o1_ref[...] + w2 * o2_ref[...]) / (w0 + w1 + w2)
    merged = (_sigmoid(ga_ref[...]) * _dot(ya_ref[...], pa_ref[...])
              + _sigmoid(gb_ref[...]) * _dot(yb, pb_ref[...])
              + _sigmoid(gc_ref[...]) * _dot(yc_ref[...], pc_ref[...]))
    out_ref[...] = merged.astype(out_ref.dtype)


def merge_mixers(ya, attn, yc, zg, pa, pb, pc, tm=256):
    m = ya.shape[0]
    d = D_MODEL
    rows = lambda w: pl.BlockSpec((tm, w), lambda i: (i, 0))
    const = lambda a: pl.BlockSpec(a.shape, lambda i: (0, 0), pipeline_mode=pl.Buffered(1))
    (o0, l0), (o1, l1), (o2, l2) = attn
    in_specs = ([rows(A_WIDTH)] + [rows(B_OUT)] * 6 + [rows(C_VW)]
                + [pl.BlockSpec((tm, d), lambda i, j=j: (i, j)) for j in range(3)]
                + [const(pa), const(pb), const(pc)])
    return pl.pallas_call(
        _merge_kernel,
        grid=(m // tm,),
        in_specs=in_specs,
        out_specs=pl.BlockSpec((tm, d), lambda i: (i, 0)),
        out_shape=jax.ShapeDtypeStruct((m, d), BF16),
        compiler_params=_cparams(("parallel",)),
        name="merge_mixers",
    )(ya, o0, o1, o2, l0, l1, l2, yc, zg, zg, zg, pa, pb, pc)


def _ffn_up_act_kernel(x_ref, wg_ref, wv_ref, cw_ref, o_ref, buf_ref, halo_ref, *, tm, tf, blocks_per_seq):
    i = pl.program_id(0)
    j = pl.program_id(1)
    h = SUBLANES
    first = (i % blocks_per_seq) == 0

    @pl.when(first)
    def _():
        buf_ref[0:h, :] = jnp.zeros((h, 2 * tf), F32)

    @pl.when(jnp.logical_not(first))
    def _():
        buf_ref[0:h, :] = halo_ref[j]

    cw = cw_ref[...]

    def project(s):
        cs = slice(s * LANES, (s + 1) * LANES)
        w = jnp.concatenate([wg_ref[:, cs], wv_ref[:, cs]], axis=1)
        return jnp.dot(x_ref[...], w, preferred_element_type=F32)

    def conv(u, cols):
        buf_ref[h:h + tm, cols] = u
        halo_ref[j, :, cols] = u[tm - h:tm, :]
        acc = u * cw[FFN_CONV - 1:FFN_CONV, cols]
        for t in range(FFN_CONV - 1):
            off = h - (FFN_CONV - 1) + t
            acc = acc + buf_ref[off:off + tm, cols] * cw[t:t + 1, cols]
        return acc

    def gate(s, u):
        cg = conv(u[:, 0:LANES], slice(s * LANES, (s + 1) * LANES))
        cv = conv(u[:, LANES:], slice(tf + s * LANES, tf + (s + 1) * LANES))
        o_ref[:, s * LANES:(s + 1) * LANES] = ((cg * _sigmoid(cg)) * cv).astype(o_ref.dtype)

    nsub = tf // LANES
    pending = project(0)
    for s in range(nsub):
        nxt = project(s + 1) if s + 1 < nsub else None
        gate(s, pending)
        pending = nxt


def ffn_up_act(h2, w_up, layer, conv_w, *, seq, tm=1024, tf=FFN_TF):
    m, d = h2.shape
    tm = min(tm, seq)
    nf = D_FF // tf
    return pl.pallas_call(
        functools.partial(_ffn_up_act_kernel, tm=tm, tf=tf, blocks_per_seq=seq // tm),
        grid=(m // tm, nf),
        in_specs=[
            pl.BlockSpec((tm, d), lambda i, j: (i, 0)),
            pl.BlockSpec((d, tf), lambda i, j: (layer, j)),
            pl.BlockSpec((d, tf), lambda i, j: (layer, nf + j)),
            pl.BlockSpec((None, FFN_CONV, 2 * tf), lambda i, j: (j, 0, 0)),
        ],
        out_specs=pl.BlockSpec((tm, tf), lambda i, j: (i, j)),
        out_shape=jax.ShapeDtypeStruct((m, D_FF), BF16),
        scratch_shapes=[
            pltpu.VMEM((tm + SUBLANES, 2 * tf), F32),
            pltpu.VMEM((nf, SUBLANES, 2 * tf), F32),
        ],
        compiler_params=_cparams(("arbitrary", "arbitrary")),
        name="ffn_up_act",
    )(h2, w_up, w_up, conv_w)


WT_ZA, WT_ZB, WT_ZC, WT_ZG = 0, ZA_W, ZA_W + B_IN, ZA_W + B_IN + ZC_W
WT_TILES_PER_STEP = 4
WT_ROWS = -(-(WT_ZG + 3 * D_MODEL) // (WT_TILES_PER_STEP * LANES)) * (WT_TILES_PER_STEP * LANES)


def _w_in_tile_table():
    aw = 3 * A_WIDTH
    lora = A_DECAY_LORA + A_ICLR_LORA
    c0 = A_IN + B_IN
    qkv = 2 * C_KW + C_VW
    runs = [(0, aw), (aw + lora, A_GATE_LORA), (None, ZA_WA - ZA_GD - A_GATE_LORA), (aw, lora),
            (None, ZA_W - ZA_WA - lora),
            (A_IN, B_IN),
            (c0, qkv), (c0 + qkv + 2 * C_HEADS, C_VW), (c0 + qkv, 2 * C_HEADS), (None, ZC_W - ZC_BA - 2 * C_HEADS),
            (c0 + C_IN, 3 * D_MODEL)]
    src, valid = [], []
    pending = 0
    for start, n in runs:
        if start is None:
            assert pending + n == LANES or pending == 0 and n % LANES == 0
            if pending == 0:
                src += [0] * (n // LANES)
                valid += [0] * (n // LANES)
            pending = 0
            continue
        assert pending == 0
        for off in range(0, n, LANES):
            src.append(start + off)
            valid.append(min(LANES, n - off))
        pending = n % LANES
    assert pending == 0 and len(src) == (WT_ZG + 3 * D_MODEL) // LANES
    tail = WT_ROWS // LANES - len(src)
    return src + [0] * tail, valid + [0] * tail


def _prep_w_in_kernel(src_ref, valid_ref, *refs):
    *w_refs, o_ref = refs
    t = pl.program_id(0)
    rows = _iota2(w_refs[0].shape, 0)
    for g, w_ref in enumerate(w_refs):
        tile = jnp.where(rows < valid_ref[t * len(w_refs) + g], w_ref[...], 0.0)
        o_ref[g * LANES:(g + 1) * LANES, :] = tile.astype(BF16)


def prep_w_in(w_in, l):
    depth, d, n_in = w_in.shape
    wt = jnp.swapaxes(w_in, 1, 2).reshape(depth * n_in, d)
    src, valid = _w_in_tile_table()
    src = jnp.asarray(src, jnp.int32) + l * n_in
    valid = jnp.asarray(valid, jnp.int32)
    g = WT_TILES_PER_STEP
    tile_spec = lambda k: pl.BlockSpec(
        (pl.Element(LANES), pl.Element(d)),
        lambda t, src, valid: (pl.multiple_of(src[t * g + k], 2 * SUBLANES), 0))
    return pl.pallas_call(
        _prep_w_in_kernel,
        grid_spec=pltpu.PrefetchScalarGridSpec(
            num_scalar_prefetch=2, grid=(WT_ROWS // (g * LANES),),
            in_specs=[tile_spec(k) for k in range(g)],
            out_specs=pl.BlockSpec((g * LANES, d), lambda t, src, valid: (t, 0))),
        out_shape=jax.ShapeDtypeStruct((WT_ROWS, d), BF16),
        compiler_params=_cparams(("parallel",)),
        name="prep_w_in",
    )(src, valid, *([wt] * g))


def matmul_nt(x, wt, row0, n, tn, tm=2048, name="matmul_nt"):
    m, k = x.shape
    return pl.pallas_call(
        _matmul_nt_kernel,
        grid=(m // tm, n // tn),
        in_specs=[pl.BlockSpec((tm, k), lambda i, j: (i, 0)),
                  pl.BlockSpec((pl.Element(tn), pl.Element(k)),
                               lambda i, j: (pl.multiple_of(row0 + j * tn, 2 * SUBLANES), 0))],
        out_specs=pl.BlockSpec((tm, tn), lambda i, j: (i, j)),
        out_shape=jax.ShapeDtypeStruct((m, n), F32),
        compiler_params=_cparams(("parallel", "parallel")),
        name=name,
    )(x, wt)


def _pad_cols(w, width):
    return jnp.pad(w, ((0, 0), (0, width - w.shape[1])))


def _pad_rows(w, rows, at=0):
    return jnp.pad(w, ((at, rows - at - w.shape[0]), (0, 0)))


def _layer_params(l, p):
    aw = 3 * A_WIDTH
    mu = p["rwkv_mu"][l]
    rows = [p["rwkv_w0"][l], p["rwkv_a0"][l], p["rwkv_k_k"][l], p["rwkv_k_a"][l], p["rwkv_r_k"][l].reshape(-1),
            p["rwkv_ln_w"][l], p["rwkv_ln_b"][l],
            (p["rwkv_v0"][l - 1] if l > 0 else jnp.zeros((A_WIDTH,), F32)),
            mu[:A_WIDTH], mu[A_WIDTH:2 * A_WIDTH], mu[2 * A_WIDTH:aw]]
    rwkv_prm = jnp.pad(jnp.stack(rows), ((0, 16 - len(rows)), (0, 0)))
    mu2 = jnp.concatenate([_pad_cols(mu[None, aw + 128:], 256), mu[None, aw:aw + 128]], axis=1)
    v2 = (p["rwkv_v2"][l - 1] if l > 0 else jnp.zeros((A_VRES_LORA, A_WIDTH), F32))
    bcast = lambda t: jnp.repeat(t, C_HEAD_K)
    gdn_prm = jnp.pad(jnp.stack([bcast(p["gdn_A_log"][l]), bcast(p["gdn_dt_bias"][l]),
                                 jnp.tile(p["gdn_norm"][l], C_HEADS)]), ((0, SUBLANES - 3), (0, 0)))
    fc = p["ffn_conv"][l]
    return dict(
        v1p=(_pad_cols(p["rwkv_v1"][l - 1], LANES).astype(BF16) if l > 0 else None),
        rwkv_prm=rwkv_prm, mu2=mu2,
        w2p=_pad_rows(p["rwkv_w2"][l], 128, 0).astype(BF16),
        a2p=_pad_rows(p["rwkv_a2"][l], 128, A_DECAY_LORA).astype(BF16),
        g2p=_pad_rows(p["rwkv_g2"][l], 256, 0).astype(BF16),
        v2p=_pad_rows(v2, 128, 0).astype(BF16),
        gdn_conv=p["gdn_conv"][l].reshape(C_CONV, 3, C_KW).transpose(1, 0, 2),
        gdn_prm=gdn_prm,
        pa=p["proj_a"][l].astype(BF16), pb=p["proj_b"][l].astype(BF16), pc=p["proj_c"][l].astype(BF16),
        ffn_conv=fc.reshape(FFN_CONV, 2, D_FF // FFN_TF, FFN_TF).transpose(2, 0, 1, 3).reshape(
            D_FF // FFN_TF, FFN_CONV, 2 * FFN_TF),
    )


def _forward(x, p):
    batch, seq, d = x.shape
    m = batch * seq
    xf = x.reshape(m, d)
    tab = rope_tables(seq)
    v_first = None
    stack_rows = lambda w: w.astype(BF16).reshape(w.shape[0] * w.shape[1], w.shape[2])
    w_out_all, ffn_up_all, ffn_down_all = stack_rows(p["w_out"]), stack_rows(p["ffn_up"]), stack_rows(p["ffn_down"])
    h = rmsnorm(xf, p["attn_norm"][0], BF16)
    for l in range(DEPTH):
        lp = _layer_params(l, p)
        wt = prep_w_in(p["w_in"], l)
        za = matmul_nt(h, wt, WT_ZA, ZA_W, tn=512, name="in_proj_a")
        zb = matmul_nt(h, wt, WT_ZB, B_IN, tn=512, name="in_proj_b")
        zc = matmul_nt(h, wt, WT_ZC, ZC_W, tn=1408, tm=1024, name="in_proj_c")
        zg = matmul_nt(h, wt, WT_ZG, 3 * D_MODEL, tn=1024, name="in_proj_g")
        lv = matmul(h, lp["v1p"], tn=LANES, tm=2048, name="vres_lora") if l > 0 else None
        ya, v_l = rwkv_mix(za, lv, v_first, lp["rwkv_prm"], lp["mu2"], lp["w2p"], lp["a2p"], lp["g2p"], lp["v2p"],
                           batch=batch, seq=seq)
        if l == 0:
            v_first = v_l
        attn = [dilated_attention_group(zb, tab, gi, batch=batch, seq=seq) for gi in range(B_GROUPS)]
        yc = gated_deltanet(zc, lp["gdn_conv"], lp["gdn_prm"], batch=batch, seq=seq)
        merged = merge_mixers(ya, attn, yc, zg, lp["pa"], lp["pb"], lp["pc"])
        xf, h2 = matmul_res_norm(merged, w_out_all, l, xf, p["ffn_norm"][l], tm=512, norm_dtype=BF16,
                                 emit_sum=True, name="out_proj_norm")
        act = ffn_up_act(h2, ffn_up_all, l, lp["ffn_conv"], seq=seq)
        if l + 1 < DEPTH:
            xf, h = matmul_res_norm(act, ffn_down_all, l, xf, p["attn_norm"][l + 1], tm=256, norm_dtype=BF16,
                                    emit_sum=True, name="ffn_down_norm")
        else:
            out = matmul_res_norm(act, ffn_down_all, l, xf, p["final_norm"], tm=256, norm_dtype=F32,
                                  emit_sum=False, name="ffn_down_norm")
    return out.reshape(batch, seq, d)


def kernel(x, attn_norm, w_in, rwkv_mu, rwkv_w0, rwkv_w2, rwkv_a0, rwkv_a2, rwkv_g2, rwkv_k_k, rwkv_k_a, rwkv_r_k, rwkv_ln_w, rwkv_ln_b, rwkv_v0, rwkv_v1, rwkv_v2, gdn_conv, gdn_A_log, gdn_dt_bias, gdn_norm, proj_a, proj_b, proj_c, w_out, ffn_norm, ffn_up, ffn_conv, ffn_down, final_norm):
    params = dict(
        attn_norm=attn_norm, w_in=w_in, rwkv_mu=rwkv_mu, rwkv_w0=rwkv_w0, rwkv_w2=rwkv_w2, rwkv_a0=rwkv_a0,
        rwkv_a2=rwkv_a2, rwkv_g2=rwkv_g2, rwkv_k_k=rwkv_k_k, rwkv_k_a=rwkv_k_a, rwkv_r_k=rwkv_r_k,
        rwkv_ln_w=rwkv_ln_w, rwkv_ln_b=rwkv_ln_b, rwkv_v0=rwkv_v0, rwkv_v1=rwkv_v1, rwkv_v2=rwkv_v2,
        gdn_conv=gdn_conv, gdn_A_log=gdn_A_log, gdn_dt_bias=gdn_dt_bias, gdn_norm=gdn_norm, proj_a=proj_a,
        proj_b=proj_b, proj_c=proj_c, w_out=w_out, ffn_norm=ffn_norm, ffn_up=ffn_up, ffn_conv=ffn_conv,
        ffn_down=ffn_down, final_norm=final_norm)
    return _forward(x, params)
```

```python
import functools

import jax
import jax.numpy as jnp
from jax import lax
from jax.experimental import pallas as pl
from jax.experimental.pallas import tpu as pltpu

F32 = jnp.float32
BF16 = jnp.bfloat16

D_MODEL = 2048
DEPTH = 2
NORM_EPS = 1e-6

A_HEADS, A_HEAD = 16, 64
A_WIDTH = A_HEADS * A_HEAD
A_DECAY_LORA, A_ICLR_LORA, A_GATE_LORA, A_VRES_LORA = 64, 64, 160, 32
A_GN_EPS = 64e-5
A_IN = 3 * A_WIDTH + A_DECAY_LORA + A_ICLR_LORA + A_GATE_LORA

B_PAIRS = ((128, 1), (512, 4), (2048, 16))
B_GROUPS = 3
B_HEADS_PER_GROUP, B_HEAD = 4, 128
B_WIDTH = B_GROUPS * B_HEADS_PER_GROUP * B_HEAD
B_OUT = B_HEADS_PER_GROUP * B_HEAD
B_IN = 3 * B_WIDTH
ROPE_THETA = 500000.0
ROPE_DIM = B_HEAD // 4

C_HEADS, C_HEAD_K, C_HEAD_V = 8, 128, 128
C_KW = C_HEADS * C_HEAD_K
C_VW = C_HEADS * C_HEAD_V
C_CONV = 4
C_IN = 2 * C_KW + C_VW + 2 * C_HEADS + C_VW

D_FF = 5632
FFN_CONV = 3

LANES = 128
SUBLANES = 8
CHUNK = 64
FFN_TF = 1408
VMEM_LIMIT = 56 * 1024 * 1024

ZA_GD = 3 * A_WIDTH
ZA_WA = ZA_GD + 256
ZA_W = ZA_WA + 256
ZC_BA = 3 * C_KW + C_VW
ZC_W = ZC_BA + 128


def _cparams(sem):
    return pltpu.CompilerParams(dimension_semantics=sem, vmem_limit_bytes=VMEM_LIMIT)


def _dot(a, b):
    return jnp.dot(a.astype(BF16), b.astype(BF16), preferred_element_type=F32)


def _dot_nt(a, b):
    return lax.dot_general(a.astype(BF16), b.astype(BF16), (((1,), (1,)), ((), ())), preferred_element_type=F32)


def _dot_tn(a, b):
    return lax.dot_general(a.astype(BF16), b.astype(BF16), (((0,), (0,)), ((), ())), preferred_element_type=F32)


def _split(a):
    hi = a.astype(BF16)
    lo = (a - hi.astype(F32)).astype(BF16)
    return hi, lo


def _dot_sel_l(sel, a):
    hi, lo = _split(a)
    return (jnp.dot(sel, hi, preferred_element_type=F32) + jnp.dot(sel, lo, preferred_element_type=F32))


def _sigmoid(x):
    return 1.0 / (1.0 + jnp.exp(-x))


def _softplus(x):
    return jnp.maximum(x, 0.0) + jnp.log(1.0 + jnp.exp(-jnp.abs(x)))


def _iota2(shape, axis):
    return lax.broadcasted_iota(jnp.int32, shape, axis)


def _chunk_of(idx):
    return jnp.right_shift(idx, CHUNK.bit_length() - 1)


def _chunk_tri(n):
    ri, ci = _iota2((n, n), 0), _iota2((n, n), 1)
    return jnp.where((_chunk_of(ri) == _chunk_of(ci)) & (ri >= ci), 1.0, 0.0).astype(BF16)


def _neumann_inverse_many(n_mats, eye):
    ps = [eye + n for n in n_mats]
    qs = [_dot(n, n) for n in n_mats]
    levels = CHUNK.bit_length() - 2
    for lvl in range(levels):
        if lvl == levels - 1:
            ps = [p + _dot(p, q) for p, q in zip(ps, qs)]
        else:
            prods = [_dot(jnp.concatenate([q, p], axis=0), q) for p, q in zip(ps, qs)]
            ps = [p + pr[LANES:] for p, pr in zip(ps, prods)]
            qs = [pr[0:LANES] for pr in prods]
    return ps


def _neumann_inverse(n_mat, eye):
    return _neumann_inverse_many([n_mat], eye)[0]


def _rmsnorm_kernel(x_ref, g_ref, o_ref):
    x = x_ref[...]
    ms = jnp.mean(x * x, axis=-1, keepdims=True)
    o_ref[...] = ((x * lax.rsqrt(ms + NORM_EPS)) * g_ref[...]).astype(o_ref.dtype)


def rmsnorm(x, g, out_dtype, tm=512):
    m, d = x.shape
    return pl.pallas_call(
        _rmsnorm_kernel,
        grid=(m // tm,),
        in_specs=[pl.BlockSpec((tm, d), lambda i: (i, 0)), pl.BlockSpec((1, d), lambda i: (0, 0))],
        out_specs=pl.BlockSpec((tm, d), lambda i: (i, 0)),
        out_shape=jax.ShapeDtypeStruct((m, d), out_dtype),
        compiler_params=_cparams(("parallel",)),
        name="rmsnorm",
    )(x, g.reshape(1, d))


def _matmul_kernel(x_ref, w_ref, o_ref):
    o_ref[...] = jnp.dot(x_ref[...], w_ref[...], preferred_element_type=F32).astype(o_ref.dtype)


def _matmul_nt_kernel(x_ref, wt_ref, o_ref):
    o_ref[...] = lax.dot_general(x_ref[...], wt_ref[...], (((1,), (1,)), ((), ())), preferred_element_type=F32)


def matmul(x, w, tn, tm=1024, name="matmul"):
    m, k = x.shape
    n = w.shape[1]
    tm = min(tm, m)
    return pl.pallas_call(
        _matmul_kernel,
        grid=(m // tm, n // tn),
        in_specs=[pl.BlockSpec((tm, k), lambda i, j: (i, 0)), pl.BlockSpec((k, tn), lambda i, j: (0, j))],
        out_specs=pl.BlockSpec((tm, tn), lambda i, j: (i, j)),
        out_shape=jax.ShapeDtypeStruct((m, n), F32),
        compiler_params=_cparams(("parallel", "parallel")),
        name=name,
    )(x, w)


def _matmul_res_norm_kernel(x_ref, w_ref, r_ref, g_ref, *out_refs):
    y = r_ref[...] + jnp.dot(x_ref[...], w_ref[...], preferred_element_type=F32)
    ms = jnp.mean(y * y, axis=-1, keepdims=True)
    hn = (y * lax.rsqrt(ms + NORM_EPS)) * g_ref[...]
    if len(out_refs) == 2:
        out_refs[0][...] = y
    out_refs[-1][...] = hn.astype(out_refs[-1].dtype)


def matmul_res_norm(x, w, layer, residual, gain, *, tm, norm_dtype, emit_sum, name):
    m, k = x.shape
    n = w.shape[1]
    rows = lambda dt: (pl.BlockSpec((tm, n), lambda i: (i, 0)), jax.ShapeDtypeStruct((m, n), dt))
    outs = ([rows(F32)] if emit_sum else []) + [rows(norm_dtype)]
    res = pl.pallas_call(
        _matmul_res_norm_kernel,
        grid=(m // tm,),
        in_specs=[pl.BlockSpec((tm, k), lambda i: (i, 0)),
                  pl.BlockSpec((k, n), lambda i: (layer, 0), pipeline_mode=pl.Buffered(1)),
                  pl.BlockSpec((tm, n), lambda i: (i, 0)),
                  pl.BlockSpec((1, n), lambda i: (0, 0))],
        out_specs=[o[0] for o in outs],
        out_shape=[o[1] for o in outs],
        compiler_params=_cparams(("parallel",)),
        name=name,
    )(x, w, residual, gain.reshape(1, n))
    return tuple(res) if emit_sum else res[0]


def _rwkv_kernel(r_ref, k_ref, v_ref, gd_ref, wa_ref, lv_ref, vf_ref, p_ref, mu2_ref, w2_ref, a2_ref, g2_ref,
                 v2_ref, *out_and_scratch, tb, pp, has_vmix):
    *out_refs, s_ref, y_ref, br_ref, bk_ref, bv_ref, bgd_ref, bwa_ref = out_and_scratch
    i = pl.program_id(2)
    h = SUBLANES

    @pl.when(i == 0)
    def _():
        s_ref[...] = jnp.zeros_like(s_ref)
        for b in (br_ref, bk_ref, bv_ref, bgd_ref, bwa_ref):
            b[0:h, :] = jnp.zeros((h, b.shape[1]), F32)

    prm = p_ref[...]
    w0, a0, k_k, k_a, r_k, ln_w, ln_b, v0 = [prm[j:j + 1] for j in range(8)]
    mu_r, mu_k, mu_v = prm[8:9], prm[9:10], prm[10:11]
    mu2 = mu2_ref[...]
    mu_gd, mu_wa = mu2[:, 0:256], mu2[:, 256:384]

    def shifted_mix(x_ref, buf_ref, mu):
        x = x_ref[...]
        buf_ref[h:h + tb, :] = x
        xs = buf_ref[h - 1:h - 1 + tb, :]
        buf_ref[0:h, :] = x[tb - h:tb, :]
        return x + (xs - x) * mu

    r = shifted_mix(r_ref, br_ref, mu_r)
    k = shifted_mix(k_ref, bk_ref, mu_k)
    v = shifted_mix(v_ref, bv_ref, mu_v)
    gd = shifted_mix(gd_ref, bgd_ref, mu_gd)
    wa = shifted_mix(wa_ref, bwa_ref, mu_wa)

    lane = _iota2((1, LANES), 1)
    m0 = jnp.where(lane < A_HEAD, 1.0, 0.0)
    m1 = 1.0 - m0
    ri = _iota2((LANES, LANES), 0)
    ci = _iota2((LANES, LANES), 1)
    same = _chunk_of(ri) == _chunk_of(ci)
    strict = same & (ri > ci)
    incl = same & (ri >= ci)
    eye = jnp.where(ri == ci, 1.0, 0.0)
    tri = _chunk_tri(tb)
    pair_lanes = [slice(q * LANES, (q + 1) * LANES) for q in range(pp)]

    def head_sum(x):
        outs = []
        for ls in pair_lanes:
            s0 = jnp.sum(x[:, ls] * m0, axis=-1, keepdims=True)
            s1 = jnp.sum(x[:, ls] * m1, axis=-1, keepdims=True)
            outs.append(jnp.where(lane < A_HEAD, s0, s1))
        return jnp.concatenate(outs, axis=1)

    lw = -jnp.exp(-0.5) * _sigmoid(w0 + _dot(jnp.tanh(wa), w2_ref[...]))
    a = _sigmoid(a0 + _dot(wa, a2_ref[...]))
    g = _dot(_sigmoid(gd), g2_ref[...])
    kk = k * k_k
    kk = kk / jnp.maximum(jnp.sqrt(head_sum(kk * kk)), 1e-12)
    k = k * (1.0 + (a - 1.0) * k_a)
    if has_vmix:
        v_mix = _sigmoid(v0 + _dot(lv_ref[...], v2_ref[...]))
        v = v + (vf_ref[...] - v) * v_mix
    if len(out_refs) == 2:
        out_refs[1][...] = v

    cum = _dot_sel_l(tri, lw)
    e_pos = jnp.exp(cum)
    e_neg = jnp.exp(-cum)
    r_t = r * e_pos
    a_t = -kk * jnp.exp(cum - lw)
    b_t = (kk * a) * e_neg
    k_t = k * e_neg

    def stack_masked(x):
        return jnp.concatenate([x * m0, x * m1], axis=0)

    def stack_dup(x):
        return jnp.concatenate([x, x], axis=0)

    kb = kk * a
    nchunk = tb // CHUNK
    inst = [(c, q) for c in range(nchunk) for q in range(pp)]
    rows_of = lambda c: slice(c * CHUNK, (c + 1) * CHUNK)
    last_of = lambda c: slice((c + 1) * CHUNK - 1, (c + 1) * CHUNK)
    a_s = {cq: stack_masked(a_t[rows_of(cq[0]), pair_lanes[cq[1]]]) for cq in inst}
    r_s = {cq: stack_masked(r_t[rows_of(cq[0]), pair_lanes[cq[1]]]) for cq in inst}
    v_s = {cq: stack_masked(v[rows_of(cq[0]), pair_lanes[cq[1]]]) for cq in inst}
    gm = {}
    for c, q in inst:
        sl, ls = rows_of(c), pair_lanes[q]
        right = jnp.concatenate([stack_dup(b_t[sl, ls]), stack_dup(k_t[sl, ls])], axis=0)
        gm[c, q] = _dot_nt(jnp.concatenate([a_s[c, q], r_s[c, q]], axis=0), right)
    n_ab = [jnp.where(strict, gm[cq][0:LANES, 0:LANES], 0.0) for cq in inst]
    t_inv = dict(zip(inst, _neumann_inverse_many(n_ab, eye)))
    gv = {cq: _dot(jnp.where(strict, gm[cq][0:LANES, LANES:], 0.0), v_s[cq]) for cq in inst}
    tu = {cq: _dot(t_inv[cq], jnp.concatenate([a_s[cq], gv[cq]], axis=1)) for cq in inst}
    g_r = {cq: jnp.where(jnp.concatenate([incl, incl], axis=1), gm[cq][LANES:], 0.0) for cq in inst}

    states = [s_ref[q] for q in range(pp)]
    for c in range(nchunk):
        sl = rows_of(c)
        x0 = [_dot_nt(jnp.concatenate([tu[c, q][:, 0:LANES], r_s[c, q]], axis=0), states[q]) for q in range(pp)]
        uv = [jnp.concatenate([x0[q][0:LANES] + tu[c, q][:, LANES:], v_s[c, q]], axis=0) for q in range(pp)]
        for q, ls in enumerate(pair_lanes):
            cum_last = cum[last_of(c), ls]
            e_end = jnp.exp(cum_last - cum[sl, ls])
            bk_end = jnp.concatenate([stack_dup(kb[sl, ls] * e_end), stack_dup(k[sl, ls] * e_end)], axis=0)
            states[q] = jnp.where(same, states[q] * jnp.exp(cum_last) + _dot_tn(uv[q], bk_end), 0.0)
        for q, ls in enumerate(pair_lanes):
            y_s = x0[q][LANES:] + _dot(g_r[c, q], uv[q])
            y_ref[sl, ls] = y_s[0:CHUNK] + y_s[CHUNK:]
    for q in range(pp):
        s_ref[q] = states[q]

    y = y_ref[...]
    inv_n = 1.0 / A_HEAD
    mean = head_sum(y) * inv_n
    d = y - mean
    var = head_sum(d * d) * inv_n
    yn = d * lax.rsqrt(var + A_GN_EPS) * ln_w + ln_b
    bonus = head_sum(r * k * r_k) * v
    out_refs[0][...] = ((yn + bonus) * g).astype(out_refs[0].dtype)


def rwkv_mix(za, lv, v_first, prm, mu2, w2p, a2p, g2p, v2p, *, batch, seq, tb=256, pp=4):
    has_vmix = v_first is not None
    nt = seq // tb
    w = pp * LANES
    npair = A_WIDTH // w
    row = lambda b, p, i: b * nt + i
    if v_first is None:
        v_first = lv = za
    in_specs = [
        pl.BlockSpec((tb, w), lambda b, p, i: (row(b, p, i), p)),
        pl.BlockSpec((tb, w), lambda b, p, i: (row(b, p, i), npair + p)),
        pl.BlockSpec((tb, w), lambda b, p, i: (row(b, p, i), 2 * npair + p)),
        pl.BlockSpec((tb, 256), lambda b, p, i: (row(b, p, i), ZA_GD // 256)),
        pl.BlockSpec((tb, LANES), lambda b, p, i: (row(b, p, i), ZA_WA // LANES)),
        pl.BlockSpec((tb, LANES), lambda b, p, i: (row(b, p, i), 0)),
        pl.BlockSpec((tb, w), lambda b, p, i: (row(b, p, i), p)),
        pl.BlockSpec((16, w), lambda b, p, i: (0, p)),
        pl.BlockSpec((1, 384), lambda b, p, i: (0, 0)),
        pl.BlockSpec((LANES, w), lambda b, p, i: (0, p)),
        pl.BlockSpec((LANES, w), lambda b, p, i: (0, p)),
        pl.BlockSpec((256, w), lambda b, p, i: (0, p)),
        pl.BlockSpec((LANES, w), lambda b, p, i: (0, p)),
    ]
    out_spec = pl.BlockSpec((tb, w), lambda b, p, i: (row(b, p, i), p))
    out_dtypes = [BF16] if has_vmix else [BF16, F32]
    res = pl.pallas_call(
        functools.partial(_rwkv_kernel, tb=tb, pp=pp, has_vmix=has_vmix),
        grid=(batch, npair, nt),
        in_specs=in_specs,
        out_specs=[out_spec] * len(out_dtypes),
        out_shape=[jax.ShapeDtypeStruct((batch * seq, A_WIDTH), dt) for dt in out_dtypes],
        scratch_shapes=[
            pltpu.VMEM((pp, LANES, LANES), F32),
            pltpu.VMEM((tb, w), F32),
            pltpu.VMEM((tb + SUBLANES, w), F32),
            pltpu.VMEM((tb + SUBLANES, w), F32),
            pltpu.VMEM((tb + SUBLANES, w), F32),
            pltpu.VMEM((tb + SUBLANES, 256), F32),
            pltpu.VMEM((tb + SUBLANES, LANES), F32),
        ],
        compiler_params=_cparams(("parallel", "parallel", "arbitrary")),
        name="rwkv7",
    )(za, za, za, za, za, lv, v_first, prm, mu2, w2p, a2p, g2p, v2p)
    return (res[0], None) if has_vmix else (res[0], res[1])


def _attn_kernel(*refs, tq, dil, nb, has_prev):
    if has_prev:
        (q_ref, kc_ref, kp_ref, vc_ref, vp_ref, cc_ref, sc_ref, nc_ref, cp_ref, sp_ref, np_ref,
         o_ref, lse_ref) = refs
    else:
        q_ref, kc_ref, vc_ref, cc_ref, sc_ref, nc_ref, o_ref, lse_ref = refs
    i = pl.program_id(1)
    scale = B_HEAD ** -0.5

    def rows(j, r):
        start = j * tq * dil + r
        return pl.ds(start, tq, stride=dil) if dil > 1 else pl.ds(start, tq)

    def rope(x, tabs):
        half = ROPE_DIM // 2
        return (x * tabs[0] + pltpu.roll(x, half, axis=1) * tabs[1]
                + pltpu.roll(x, LANES - half, axis=1) * tabs[2])

    if has_prev:
        ri, ci = _iota2((tq, 2 * tq), 0), _iota2((tq, 2 * tq), 1)
        cur_ok = (ci >= tq) & ((ci - tq) <= ri)
        no_prev = jnp.where(i > 0, 0, 2 * tq)
        valid_first = ((ci < tq) & (ci >= ri + no_prev)) | cur_ok
        valid_inner = ((ci < tq) & (ci >= ri)) | cur_ok
    else:
        causal = _iota2((tq, tq), 1) <= _iota2((tq, tq), 0)

    inst = [(r, j) for r in range(dil) for j in range(nb)]
    qs, ks, vs = {}, {}, {}
    for r in range(dil):
        if has_prev:
            rp = rows(0, r)
            ks[r, -1] = rope(kp_ref[rp, :], (cp_ref[rp, :], sp_ref[rp, :], np_ref[rp, :]))
            vs[r, -1] = vp_ref[rp, :]
        for j in range(nb):
            rw = rows(j, r)
            tabs = (cc_ref[rw, :], sc_ref[rw, :], nc_ref[rw, :])
            qs[r, j] = rope(q_ref[rw, :], tabs) * scale
            ks[r, j] = rope(kc_ref[rw, :], tabs)
            vs[r, j] = vc_ref[rw, :]
    if has_prev:
        s = {(r, j): jnp.where(valid_first if j == 0 else valid_inner,
                               _dot_nt(qs[r, j], jnp.concatenate([ks[r, j - 1], ks[r, j]], axis=0)), -1e30)
             for r, j in inst}
    else:
        s = {rj: jnp.where(causal, _dot_nt(qs[rj], ks[rj]), -1e30) for rj in inst}
    m = {rj: jnp.max(s[rj], axis=-1, keepdims=True) for rj in inst}
    p = {rj: jnp.exp(s[rj] - m[rj]) for rj in inst}
    den = {rj: jnp.sum(p[rj], axis=-1, keepdims=True) for rj in inst}
    if has_prev:
        num = {(r, j): _dot(p[r, j], jnp.concatenate([vs[r, j - 1], vs[r, j]], axis=0)) for r, j in inst}
    else:
        num = {rj: _dot(p[rj], vs[rj]) for rj in inst}
    for r, j in inst:
        o_ref[rows(j, r), :] = num[r, j] / den[r, j]
        lse_ref[rows(j, r), :] = jnp.broadcast_to(m[r, j] + jnp.log(den[r, j]), (tq, B_HEAD))


def dilated_attention_group(zb, rope_tabs, gi, *, batch, seq, tq=128, blocks_per_step=16):
    win, dil = B_PAIRS[gi]
    assert win // dil == tq
    nq = seq // (dil * tq)
    has_prev = nq > 1
    nb = min(nq, max(1, blocks_per_step // dil))
    rows_step = nb * tq * dil
    nsteps = seq // rows_step
    prev_rows = tq * dil
    nprev = seq // prev_rows
    nh = B_HEADS_PER_GROUP
    cur = lambda col: pl.BlockSpec((rows_step, B_HEAD), lambda b, i, hh: (b * nsteps + i, col * nh + hh))
    prv = lambda col: pl.BlockSpec(
        (prev_rows, B_HEAD), lambda b, i, hh: (b * nprev + jnp.maximum(i * nb - 1, 0), col * nh + hh))
    tab_cur = pl.BlockSpec((rows_step, LANES), lambda b, i, hh: (i, 0))
    tab_prv = pl.BlockSpec((prev_rows, LANES), lambda b, i, hh: (jnp.maximum(i * nb - 1, 0), 0))
    qc, kc, vc = gi, B_GROUPS + gi, 2 * B_GROUPS + gi
    if has_prev:
        in_specs = [cur(qc), cur(kc), prv(kc), cur(vc), prv(vc)] + [tab_cur] * 3 + [tab_prv] * 3
        args = (zb, zb, zb, zb, zb) + tuple(rope_tabs) * 2
    else:
        in_specs = [cur(qc), cur(kc), cur(vc)] + [tab_cur] * 3
        args = (zb, zb, zb) + tuple(rope_tabs)
    out_spec = pl.BlockSpec((rows_step, B_HEAD), lambda b, i, hh: (b * nsteps + i, hh))
    out_sds = jax.ShapeDtypeStruct((batch * seq, B_OUT), F32)
    return pl.pallas_call(
        functools.partial(_attn_kernel, tq=tq, dil=dil, nb=nb, has_prev=has_prev),
        grid=(batch, nsteps, nh),
        in_specs=in_specs,
        out_specs=[out_spec, out_spec],
        out_shape=[out_sds, out_sds],
        compiler_params=_cparams(("parallel", "parallel", "arbitrary")),
        name=f"dilated_attn_g{gi}",
    )(*args)


def rope_tables(seq):
    half = ROPE_DIM // 2
    inv = ROPE_THETA ** (-jnp.arange(half, dtype=F32) / half)
    ang = jnp.arange(seq, dtype=F32)[:, None] * inv[None, :]
    cos, sin = jnp.cos(ang), jnp.sin(ang)
    z = jnp.zeros((seq, LANES - ROPE_DIM), F32)
    zh = jnp.zeros((seq, half), F32)
    c_tab = jnp.concatenate([cos, cos, jnp.ones_like(z)], axis=1)
    s_pos = jnp.concatenate([zh, sin, z], axis=1)
    s_neg = jnp.concatenate([-sin, zh, z], axis=1)
    return c_tab, s_pos, s_neg


def _gdn_kernel(q_ref, k_ref, v_ref, gate_ref, ba_ref, cw_ref, p_ref, out_ref, s_ref, o_ref, bq_ref, bk_ref,
                bv_ref, *, tb, pp):
    i = pl.program_id(2)
    h = SUBLANES
    hd = C_HEAD_K

    @pl.when(i == 0)
    def _():
        s_ref[...] = jnp.zeros_like(s_ref)
        for b in (bq_ref, bk_ref, bv_ref):
            b[0:h, :] = jnp.zeros((h, b.shape[1]), F32)

    cw = cw_ref[...]
    prm = p_ref[...]

    def conv_silu(x_ref, buf_ref, w):
        x = x_ref[...]
        buf_ref[h:h + tb, :] = x
        acc = x * w[C_CONV - 1:C_CONV]
        for j in range(C_CONV - 1):
            off = h - (C_CONV - 1) + j
            acc = acc + buf_ref[off:off + tb, :] * w[j:j + 1]
        buf_ref[0:h, :] = x[tb - h:tb, :]
        return acc * _sigmoid(acc)

    q = conv_silu(q_ref, bq_ref, cw[0])
    k = conv_silu(k_ref, bk_ref, cw[1])
    v = conv_silu(v_ref, bv_ref, cw[2])

    ri = _iota2((LANES, LANES), 0)
    ci = _iota2((LANES, LANES), 1)
    same = _chunk_of(ri) == _chunk_of(ci)
    strict = same & (ri > ci)
    incl = same & (ri >= ci)
    eye = jnp.where(ri == ci, 1.0, 0.0)
    tri = _chunk_tri(tb)

    ba = ba_ref[...]
    nh = 2 * pp
    head_lanes = [slice(hh * hd, (hh + 1) * hd) for hh in range(nh)]
    vh = [v[:, ls] for ls in head_lanes]
    q_ss = [jnp.sum(q[:, ls] * q[:, ls], axis=-1, keepdims=True) for ls in head_lanes]
    k_ss = [jnp.sum(k[:, ls] * k[:, ls], axis=-1, keepdims=True) for ls in head_lanes]
    b_raw = [ba[:, hh:hh + 1] for hh in range(nh)]
    a_raw = [ba[:, C_HEADS + hh:C_HEADS + hh + 1] for hh in range(nh)]
    qh = [q[:, ls] / jnp.maximum(jnp.sqrt(ss), 1e-12) * (hd ** -0.5) for ls, ss in zip(head_lanes, q_ss)]
    kh = [k[:, ls] / jnp.maximum(jnp.sqrt(ss), 1e-12) for ls, ss in zip(head_lanes, k_ss)]
    beta = [_sigmoid(b) for b in b_raw]
    glog = [-jnp.exp(prm[0:1, ls]) * _softplus(al + prm[1:2, ls]) for ls, al in zip(head_lanes, a_raw)]
    gam = [_dot_sel_l(tri, gl) for gl in glog]

    nchunk = tb // CHUNK
    inst = [(c, pr) for c in range(nchunk) for pr in range(pp)]
    rows_of = lambda c: slice(c * CHUNK, (c + 1) * CHUNK)

    def stack(xs, cp):
        c, pr = cp
        return jnp.concatenate([xs[2 * pr][rows_of(c)], xs[2 * pr + 1][rows_of(c)]], axis=0)

    k_s = {cp: stack(kh, cp) for cp in inst}
    q_s = {cp: stack(qh, cp) for cp in inst}
    beta_s = {cp: stack(beta, cp) for cp in inst}
    gam_s = {cp: stack(gam, cp) for cp in inst}
    kq = {cp: _dot_nt(jnp.concatenate([k_s[cp], q_s[cp]], axis=0), k_s[cp]) for cp in inst}
    dm = {cp: jnp.exp(jnp.where(incl, gam_s[cp] - gam_s[cp].T, -1e30)) for cp in inst}
    n_mats = [jnp.where(strict, -(beta_s[cp] * kq[cp][0:LANES] * dm[cp]), 0.0) for cp in inst]
    t_inv = dict(zip(inst, _neumann_inverse_many(n_mats, eye)))
    e_gam = {cp: jnp.exp(gam_s[cp]) for cp in inst}
    sol = {cp: _dot(t_inv[cp], jnp.concatenate([stack(vh, cp) * beta_s[cp],
                                                k_s[cp] * (beta_s[cp] * e_gam[cp])], axis=1)) for cp in inst}

    states = [s_ref[hh] for hh in range(nh)]
    for c in range(nchunk):
        sl = rows_of(c)
        ws = []
        for hh in range(nh):
            cp, hs = (c, hh // 2), slice((hh % 2) * CHUNK, (hh % 2 + 1) * CHUNK)
            qg = (q_s[cp] * e_gam[cp])[hs]
            ws.append(_dot(jnp.concatenate([sol[cp][hs, hd:], qg], axis=0), states[hh]))
        v_new = [sol[c, hh // 2][(hh % 2) * CHUNK:(hh % 2 + 1) * CHUNK, 0:hd] - ws[hh][0:CHUNK]
                 for hh in range(nh)]
        for hh in range(nh):
            g_h = gam[hh][sl]
            g_last = g_h[CHUNK - 1:CHUNK, :]
            states[hh] = states[hh] * jnp.exp(g_last) + _dot_tn(kh[hh][sl] * jnp.exp(g_last - g_h), v_new[hh])
        for pr in range(pp):
            h0 = 2 * pr
            attn = kq[c, pr][LANES:] * dm[c, pr]
            o_s = (jnp.concatenate([ws[h0][CHUNK:], ws[h0 + 1][CHUNK:]], axis=0)
                   + _dot(attn, jnp.concatenate([v_new[h0], v_new[h0 + 1]], axis=0)))
            o_ref[sl, h0 * hd:(h0 + 1) * hd] = o_s[0:CHUNK]
            o_ref[sl, (h0 + 1) * hd:(h0 + 2) * hd] = o_s[CHUNK:]
    for hh in range(nh):
        s_ref[hh] = states[hh]

    gate = gate_ref[...]
    o_h = [o_ref[:, ls] for ls in head_lanes]
    o_ms = [jnp.mean(o * o, axis=-1, keepdims=True) for o in o_h]
    for o, ms, ls in zip(o_h, o_ms, head_lanes):
        gt = gate[:, ls]
        out_ref[:, ls] = ((o * lax.rsqrt(ms + NORM_EPS) * prm[2:3, ls]) * (gt * _sigmoid(gt))).astype(out_ref.dtype)


def gated_deltanet(zc, conv_w, prm, *, batch, seq, tb=256, pp=4):
    assert 2 * pp == C_HEADS, "the kernel indexes the per-head beta/alpha columns statically"
    nt = seq // tb
    npair = C_HEADS // (2 * pp)
    wblk = 2 * pp * C_HEAD_K
    row = lambda b, p, i: b * nt + i
    in_specs = [
        pl.BlockSpec((tb, wblk), lambda b, p, i: (row(b, p, i), p)),
        pl.BlockSpec((tb, wblk), lambda b, p, i: (row(b, p, i), npair + p)),
        pl.BlockSpec((tb, wblk), lambda b, p, i: (row(b, p, i), 2 * npair + p)),
        pl.BlockSpec((tb, wblk), lambda b, p, i: (row(b, p, i), 3 * npair + p)),
        pl.BlockSpec((tb, LANES), lambda b, p, i: (row(b, p, i), ZC_BA // LANES)),
        pl.BlockSpec((3, C_CONV, wblk), lambda b, p, i: (0, 0, p)),
        pl.BlockSpec((SUBLANES, wblk), lambda b, p, i: (0, p)),
    ]
    return pl.pallas_call(
        functools.partial(_gdn_kernel, tb=tb, pp=pp),
        grid=(batch, npair, nt),
        in_specs=in_specs,
        out_specs=pl.BlockSpec((tb, wblk), lambda b, p, i: (row(b, p, i), p)),
        out_shape=jax.ShapeDtypeStruct((batch * seq, C_VW), BF16),
        scratch_shapes=[
            pltpu.VMEM((2 * pp, C_HEAD_K, C_HEAD_V), F32),
            pltpu.VMEM((tb, wblk), F32),
            pltpu.VMEM((tb + SUBLANES, wblk), F32),
            pltpu.VMEM((tb + SUBLANES, wblk), F32),
            pltpu.VMEM((tb + SUBLANES, wblk), F32),
        ],
        compiler_params=_cparams(("parallel", "parallel", "arbitrary")),
        name="gated_deltanet",
    )(zc, zc, zc, zc, zc, conv_w, prm)


def _merge_out_kernel(ya_ref, o0_ref, o1_ref, o2_ref, l0_ref, l1_ref, l2_ref, yc_ref, ga_ref, gb_ref, gc_ref,
                      pa_ref, pb_ref, pc_ref, wo_ref, x_ref, g_ref, y_ref, h_ref):
    l0, l1, l2 = l0_ref[...], l1_ref[...], l2_ref[...]
    m = jnp.maximum(jnp.maximum(l0, l1), l2)
    w0, w1, w2 = jnp.exp(l0 - m), jnp.exp(l1 - m), jnp.exp(l2 - m)
    yb = (w0 * o0_ref[...] + w1 * o1_ref[...] + w2 * o2_ref[...]) / (w0 + w1 + w2)
    merged = (_sigmoid(ga_ref[...]) * _dot(ya_ref[...], pa_ref[...])
              + _sigmoid(gb_ref[...]) * _dot(yb, pb_ref[...])
              + _sigmoid(gc_ref[...]) * _dot(yc_ref[...], pc_ref[...]))
    y = x_ref[...] + _dot(merged, wo_ref[...])
    ms = jnp.mean(y * y, axis=-1, keepdims=True)
    y_ref[...] = y
    h_ref[...] = ((y * lax.rsqrt(ms + NORM_EPS)) * g_ref[...]).astype(h_ref.dtype)


def merge_out_proj(ya, attn, yc, zg, pa, pb, pc, w_out, layer, x, gain, tm=256):
    m = ya.shape[0]
    d = D_MODEL
    rows = lambda w: pl.BlockSpec((tm, w), lambda i: (i, 0))
    const = lambda a: pl.BlockSpec(a.shape, lambda i: (0, 0), pipeline_mode=pl.Buffered(1))
    (o0, l0), (o1, l1), (o2, l2) = attn
    in_specs = ([rows(A_WIDTH)] + [rows(B_OUT)] * 6 + [rows(C_VW)]
                + [pl.BlockSpec((tm, d), lambda i, j=j: (i, j)) for j in range(3)]
                + [const(pa), const(pb), const(pc),
                   pl.BlockSpec((d, d), lambda i: (layer, 0), pipeline_mode=pl.Buffered(1)),
                   rows(d), pl.BlockSpec((1, d), lambda i: (0, 0))])
    return pl.pallas_call(
        _merge_out_kernel,
        grid=(m // tm,),
        in_specs=in_specs,
        out_specs=[rows(d), rows(d)],
        out_shape=[jax.ShapeDtypeStruct((m, d), F32), jax.ShapeDtypeStruct((m, d), BF16)],
        compiler_params=_cparams(("parallel",)),
        name="merge_out_proj",
    )(ya, o0, o1, o2, l0, l1, l2, yc, zg, zg, zg, pa, pb, pc, w_out, x, gain.reshape(1, d))


def _ffn_up_act_kernel(x_ref, wg_ref, wv_ref, cw_ref, o_ref, buf_ref, halo_ref, *, tm, tf, blocks_per_seq):
    i = pl.program_id(0)
    j = pl.program_id(1)
    h = SUBLANES
    first = (i % blocks_per_seq) == 0

    @pl.when(first)
    def _():
        buf_ref[0:h, :] = jnp.zeros((h, 2 * tf), F32)

    @pl.when(jnp.logical_not(first))
    def _():
        buf_ref[0:h, :] = halo_ref[j]

    cw = cw_ref[...]

    def project(s):
        cs = slice(s * LANES, (s + 1) * LANES)
        w = jnp.concatenate([wg_ref[:, cs], wv_ref[:, cs]], axis=1)
        return jnp.dot(x_ref[...], w, preferred_element_type=F32)

    def conv(u, cols):
        buf_ref[h:h + tm, cols] = u
        halo_ref[j, :, cols] = u[tm - h:tm, :]
        acc = u * cw[FFN_CONV - 1:FFN_CONV, cols]
        for t in range(FFN_CONV - 1):
            off = h - (FFN_CONV - 1) + t
            acc = acc + buf_ref[off:off + tm, cols] * cw[t:t + 1, cols]
        return acc

    def gate(s, u):
        cg = conv(u[:, 0:LANES], slice(s * LANES, (s + 1) * LANES))
        cv = conv(u[:, LANES:], slice(tf + s * LANES, tf + (s + 1) * LANES))
        o_ref[:, s * LANES:(s + 1) * LANES] = ((cg * _sigmoid(cg)) * cv).astype(o_ref.dtype)

    nsub = tf // LANES
    pending = project(0)
    for s in range(nsub):
        nxt = project(s + 1) if s + 1 < nsub else None
        gate(s, pending)
        pending = nxt


def ffn_up_act(h2, w_up, layer, conv_w, *, seq, tm=1024, tf=FFN_TF):
    m, d = h2.shape
    tm = min(tm, seq)
    nf = D_FF // tf
    return pl.pallas_call(
        functools.partial(_ffn_up_act_kernel, tm=tm, tf=tf, blocks_per_seq=seq // tm),
        grid=(m // tm, nf),
        in_specs=[
            pl.BlockSpec((tm, d), lambda i, j: (i, 0)),
            pl.BlockSpec((d, tf), lambda i, j: (layer, j)),
            pl.BlockSpec((d, tf), lambda i, j: (layer, nf + j)),
            pl.BlockSpec((None, FFN_CONV, 2 * tf), lambda i, j: (j, 0, 0)),
        ],
        out_specs=pl.BlockSpec((tm, tf), lambda i, j: (i, j)),
        out_shape=jax.ShapeDtypeStruct((m, D_FF), BF16),
        scratch_shapes=[
            pltpu.VMEM((tm + SUBLANES, 2 * tf), F32),
            pltpu.VMEM((nf, SUBLANES, 2 * tf), F32),
        ],
        compiler_params=_cparams(("arbitrary", "arbitrary")),
        name="ffn_up_act",
    )(h2, w_up, w_up, conv_w)


WT_ZA, WT_ZB, WT_ZC, WT_ZG = 0, ZA_W, ZA_W + B_IN, ZA_W + B_IN + ZC_W
WT_TILES_PER_STEP = 4
WT_ROWS = -(-(WT_ZG + 3 * D_MODEL) // (WT_TILES_PER_STEP * LANES)) * (WT_TILES_PER_STEP * LANES)


def _w_in_tile_table():
    aw = 3 * A_WIDTH
    lora = A_DECAY_LORA + A_ICLR_LORA
    c0 = A_IN + B_IN
    qkv = 2 * C_KW + C_VW
    runs = [(0, aw), (aw + lora, A_GATE_LORA), (None, ZA_WA - ZA_GD - A_GATE_LORA), (aw, lora),
            (None, ZA_W - ZA_WA - lora),
            (A_IN, B_IN),
            (c0, qkv), (c0 + qkv + 2 * C_HEADS, C_VW), (c0 + qkv, 2 * C_HEADS), (None, ZC_W - ZC_BA - 2 * C_HEADS),
            (c0 + C_IN, 3 * D_MODEL)]
    src, valid = [], []
    pending = 0
    for start, n in runs:
        if start is None:
            assert pending + n == LANES or pending == 0 and n % LANES == 0
            if pending == 0:
                src += [0] * (n // LANES)
                valid += [0] * (n // LANES)
            pending = 0
            continue
        assert pending == 0
        for off in range(0, n, LANES):
            src.append(start + off)
            valid.append(min(LANES, n - off))
        pending = n % LANES
    assert pending == 0 and len(src) == (WT_ZG + 3 * D_MODEL) // LANES
    tail = WT_ROWS // LANES - len(src)
    return src + [0] * tail, valid + [0] * tail


def _prep_w_in_kernel(src_ref, valid_ref, *refs):
    *w_refs, o_ref = refs
    t = pl.program_id(0)
    rows = _iota2(w_refs[0].shape, 0)
    for g, w_ref in enumerate(w_refs):
        tile = jnp.where(rows < valid_ref[t * len(w_refs) + g], w_ref[...], 0.0)
        o_ref[g * LANES:(g + 1) * LANES, :] = tile.astype(BF16)


def prep_w_in(w_in, l):
    depth, d, n_in = w_in.shape
    wt = jnp.swapaxes(w_in, 1, 2).reshape(depth * n_in, d)
    src, valid = _w_in_tile_table()
    src = jnp.asarray(src, jnp.int32) + l * n_in
    valid = jnp.asarray(valid, jnp.int32)
    g = WT_TILES_PER_STEP
    tile_spec = lambda k: pl.BlockSpec(
        (pl.Element(LANES), pl.Element(d)),
        lambda t, src, valid: (pl.multiple_of(src[t * g + k], 2 * SUBLANES), 0))
    return pl.pallas_call(
        _prep_w_in_kernel,
        grid_spec=pltpu.PrefetchScalarGridSpec(
            num_scalar_prefetch=2, grid=(WT_ROWS // (g * LANES),),
            in_specs=[tile_spec(k) for k in range(g)],
            out_specs=pl.BlockSpec((g * LANES, d), lambda t, src, valid: (t, 0))),
        out_shape=jax.ShapeDtypeStruct((WT_ROWS, d), BF16),
        compiler_params=_cparams(("parallel",)),
        name="prep_w_in",
    )(src, valid, *([wt] * g))


def matmul_nt(x, wt, row0, n, tn, tm=2048, name="matmul_nt"):
    m, k = x.shape
    return pl.pallas_call(
        _matmul_nt_kernel,
        grid=(m // tm, n // tn),
        in_specs=[pl.BlockSpec((tm, k), lambda i, j: (i, 0)),
                  pl.BlockSpec((pl.Element(tn), pl.Element(k)),
                               lambda i, j: (pl.multiple_of(row0 + j * tn, 2 * SUBLANES), 0))],
        out_specs=pl.BlockSpec((tm, tn), lambda i, j: (i, j)),
        out_shape=jax.ShapeDtypeStruct((m, n), F32),
        compiler_params=_cparams(("parallel", "parallel")),
        name=name,
    )(x, wt)


def _pad_cols(w, width):
    return jnp.pad(w, ((0, 0), (0, width - w.shape[1])))


def _pad_rows(w, rows, at=0):
    return jnp.pad(w, ((at, rows - at - w.shape[0]), (0, 0)))


def _layer_params(l, p):
    aw = 3 * A_WIDTH
    mu = p["rwkv_mu"][l]
    rows = [p["rwkv_w0"][l], p["rwkv_a0"][l], p["rwkv_k_k"][l], p["rwkv_k_a"][l], p["rwkv_r_k"][l].reshape(-1),
            p["rwkv_ln_w"][l], p["rwkv_ln_b"][l],
            (p["rwkv_v0"][l - 1] if l > 0 else jnp.zeros((A_WIDTH,), F32)),
            mu[:A_WIDTH], mu[A_WIDTH:2 * A_WIDTH], mu[2 * A_WIDTH:aw]]
    rwkv_prm = jnp.pad(jnp.stack(rows), ((0, 16 - len(rows)), (0, 0)))
    mu2 = jnp.concatenate([_pad_cols(mu[None, aw + 128:], 256), mu[None, aw:aw + 128]], axis=1)
    v2 = (p["rwkv_v2"][l - 1] if l > 0 else jnp.zeros((A_VRES_LORA, A_WIDTH), F32))
    bcast = lambda t: jnp.repeat(t, C_HEAD_K)
    gdn_prm = jnp.pad(jnp.stack([bcast(p["gdn_A_log"][l]), bcast(p["gdn_dt_bias"][l]),
                                 jnp.tile(p["gdn_norm"][l], C_HEADS)]), ((0, SUBLANES - 3), (0, 0)))
    fc = p["ffn_conv"][l]
    return dict(
        v1p=(_pad_cols(p["rwkv_v1"][l - 1], LANES).astype(BF16) if l > 0 else None),
        rwkv_prm=rwkv_prm, mu2=mu2,
        w2p=_pad_rows(p["rwkv_w2"][l], 128, 0).astype(BF16),
        a2p=_pad_rows(p["rwkv_a2"][l], 128, A_DECAY_LORA).astype(BF16),
        g2p=_pad_rows(p["rwkv_g2"][l], 256, 0).astype(BF16),
        v2p=_pad_rows(v2, 128, 0).astype(BF16),
        gdn_conv=p["gdn_conv"][l].reshape(C_CONV, 3, C_KW).transpose(1, 0, 2),
        gdn_prm=gdn_prm,
        pa=p["proj_a"][l].astype(BF16), pb=p["proj_b"][l].astype(BF16), pc=p["proj_c"][l].astype(BF16),
        ffn_conv=fc.reshape(FFN_CONV, 2, D_FF // FFN_TF, FFN_TF).transpose(2, 0, 1, 3).reshape(
            D_FF // FFN_TF, FFN_CONV, 2 * FFN_TF),
    )


def _forward(x, p):
    batch, seq, d = x.shape
    m = batch * seq
    xf = x.reshape(m, d)
    tab = rope_tables(seq)
    v_first = None
    stack_rows = lambda w: w.astype(BF16).reshape(w.shape[0] * w.shape[1], w.shape[2])
    w_out_all, ffn_up_all, ffn_down_all = stack_rows(p["w_out"]), stack_rows(p["ffn_up"]), stack_rows(p["ffn_down"])
    h = rmsnorm(xf, p["attn_norm"][0], BF16)
    for l in range(DEPTH):
        lp = _layer_params(l, p)
        wt = prep_w_in(p["w_in"], l)
        za = matmul_nt(h, wt, WT_ZA, ZA_W, tn=512, name="in_proj_a")
        zb = matmul_nt(h, wt, WT_ZB, B_IN, tn=512, name="in_proj_b")
        zc = matmul_nt(h, wt, WT_ZC, ZC_W, tn=1408, tm=1024, name="in_proj_c")
        zg = matmul_nt(h, wt, WT_ZG, 3 * D_MODEL, tn=1024, name="in_proj_g")
        lv = matmul(h, lp["v1p"], tn=LANES, tm=2048, name="vres_lora") if l > 0 else None
        ya, v_l = rwkv_mix(za, lv, v_first, lp["rwkv_prm"], lp["mu2"], lp["w2p"], lp["a2p"], lp["g2p"], lp["v2p"],
                           batch=batch, seq=seq)
        if l == 0:
            v_first = v_l
        attn = [dilated_attention_group(zb, tab, gi, batch=batch, seq=seq) for gi in range(B_GROUPS)]
        yc = gated_deltanet(zc, lp["gdn_conv"], lp["gdn_prm"], batch=batch, seq=seq)
        xf, h2 = merge_out_proj(ya, attn, yc, zg, lp["pa"], lp["pb"], lp["pc"], w_out_all, l, xf, p["ffn_norm"][l])
        act = ffn_up_act(h2, ffn_up_all, l, lp["ffn_conv"], seq=seq)
        if l + 1 < DEPTH:
            xf, h = matmul_res_norm(act, ffn_down_all, l, xf, p["attn_norm"][l + 1], tm=256, norm_dtype=BF16,
                                    emit_sum=True, name="ffn_down_norm")
        else:
            out = matmul_res_norm(act, ffn_down_all, l, xf, p["final_norm"], tm=256, norm_dtype=F32,
                                  emit_sum=False, name="ffn_down_norm")
    return out.reshape(batch, seq, d)


def kernel(x, attn_norm, w_in, rwkv_mu, rwkv_w0, rwkv_w2, rwkv_a0, rwkv_a2, rwkv_g2, rwkv_k_k, rwkv_k_a, rwkv_r_k, rwkv_ln_w, rwkv_ln_b, rwkv_v0, rwkv_v1, rwkv_v2, gdn_conv, gdn_A_log, gdn_dt_bias, gdn_norm, proj_a, proj_b, proj_c, w_out, ffn_norm, ffn_up, ffn_conv, ffn_down, final_norm):
    params = dict(
        attn_norm=attn_norm, w_in=w_in, rwkv_mu=rwkv_mu, rwkv_w0=rwkv_w0, rwkv_w2=rwkv_w2, rwkv_a0=rwkv_a0,
        rwkv_a2=rwkv_a2, rwkv_g2=rwkv_g2, rwkv_k_k=rwkv_k_k, rwkv_k_a=rwkv_k_a, rwkv_r_k=rwkv_r_k,
        rwkv_ln_w=rwkv_ln_w, rwkv_ln_b=rwkv_ln_b, rwkv_v0=rwkv_v0, rwkv_v1=rwkv_v1, rwkv_v2=rwkv_v2,
        gdn_conv=gdn_conv, gdn_A_log=gdn_A_log, gdn_dt_bias=gdn_dt_bias, gdn_norm=gdn_norm, proj_a=proj_a,
        proj_b=proj_b, proj_c=proj_c, w_out=w_out, ffn_norm=ffn_norm, ffn_up=ffn_up, ffn_conv=ffn_conv,
        ffn_down=ffn_down, final_norm=final_norm)
    return _forward(x, params)
```

```python
import functools

import jax
import jax.numpy as jnp
from jax import lax
from jax.experimental import pallas as pl
from jax.experimental.pallas import tpu as pltpu

F32 = jnp.float32
BF16 = jnp.bfloat16

D_MODEL = 2048
DEPTH = 2
NORM_EPS = 1e-6

A_HEADS, A_HEAD = 16, 64
A_WIDTH = A_HEADS * A_HEAD
A_DECAY_LORA, A_ICLR_LORA, A_GATE_LORA, A_VRES_LORA = 64, 64, 160, 32
A_GN_EPS = 64e-5
A_IN = 3 * A_WIDTH + A_DECAY_LORA + A_ICLR_LORA + A_GATE_LORA

B_PAIRS = ((128, 1), (512, 4), (2048, 16))
B_GROUPS = 3
B_HEADS_PER_GROUP, B_HEAD = 4, 128
B_WIDTH = B_GROUPS * B_HEADS_PER_GROUP * B_HEAD
B_OUT = B_HEADS_PER_GROUP * B_HEAD
B_IN = 3 * B_WIDTH
ROPE_THETA = 500000.0
ROPE_DIM = B_HEAD // 4

C_HEADS, C_HEAD_K, C_HEAD_V = 8, 128, 128
C_KW = C_HEADS * C_HEAD_K
C_VW = C_HEADS * C_HEAD_V
C_CONV = 4
C_IN = 2 * C_KW + C_VW + 2 * C_HEADS + C_VW

D_FF = 5632
FFN_CONV = 3

LANES = 128
SUBLANES = 8
CHUNK = 64
FFN_TF = 1408
VMEM_LIMIT = 56 * 1024 * 1024

ZA_GD = 3 * A_WIDTH
ZA_WA = ZA_GD + 256
ZA_LV = ZA_WA + 128
ZA_W = ZA_LV + 128
ZC_BA = 3 * C_KW + C_VW
ZC_W = ZC_BA + 128


def _cparams(sem):
    return pltpu.CompilerParams(dimension_semantics=sem, vmem_limit_bytes=VMEM_LIMIT)


def _dot(a, b):
    return jnp.dot(a.astype(BF16), b.astype(BF16), preferred_element_type=F32)


def _dot_nt(a, b):
    return lax.dot_general(a.astype(BF16), b.astype(BF16), (((1,), (1,)), ((), ())), preferred_element_type=F32)


def _dot_tn(a, b):
    return lax.dot_general(a.astype(BF16), b.astype(BF16), (((0,), (0,)), ((), ())), preferred_element_type=F32)


def _split(a):
    hi = a.astype(BF16)
    lo = (a - hi.astype(F32)).astype(BF16)
    return hi, lo


def _dot_sel_l(sel, a):
    hi, lo = _split(a)
    return (jnp.dot(sel, hi, preferred_element_type=F32) + jnp.dot(sel, lo, preferred_element_type=F32))


def _sigmoid(x):
    return 1.0 / (1.0 + jnp.exp(-x))


def _softplus(x):
    return jnp.maximum(x, 0.0) + jnp.log(1.0 + jnp.exp(-jnp.abs(x)))


def _iota2(shape, axis):
    return lax.broadcasted_iota(jnp.int32, shape, axis)


def _chunk_of(idx):
    return jnp.right_shift(idx, CHUNK.bit_length() - 1)


def _chunk_tri(n):
    ri, ci = _iota2((n, n), 0), _iota2((n, n), 1)
    return jnp.where((_chunk_of(ri) == _chunk_of(ci)) & (ri >= ci), 1.0, 0.0).astype(BF16)


def _neumann_inverse_many(n_mats, eye):
    ps = [eye + n for n in n_mats]
    qs = [_dot(n, n) for n in n_mats]
    levels = CHUNK.bit_length() - 2
    for lvl in range(levels):
        if lvl == levels - 1:
            ps = [p + _dot(p, q) for p, q in zip(ps, qs)]
        else:
            prods = [_dot(jnp.concatenate([q, p], axis=0), q) for p, q in zip(ps, qs)]
            ps = [p + pr[LANES:] for p, pr in zip(ps, prods)]
            qs = [pr[0:LANES] for pr in prods]
    return ps


def _neumann_inverse(n_mat, eye):
    return _neumann_inverse_many([n_mat], eye)[0]


def _rmsnorm_kernel(x_ref, g_ref, o_ref):
    x = x_ref[...]
    ms = jnp.mean(x * x, axis=-1, keepdims=True)
    o_ref[...] = ((x * lax.rsqrt(ms + NORM_EPS)) * g_ref[...]).astype(o_ref.dtype)


def rmsnorm(x, g, out_dtype, tm=512):
    m, d = x.shape
    return pl.pallas_call(
        _rmsnorm_kernel,
        grid=(m // tm,),
        in_specs=[pl.BlockSpec((tm, d), lambda i: (i, 0)), pl.BlockSpec((1, d), lambda i: (0, 0))],
        out_specs=pl.BlockSpec((tm, d), lambda i: (i, 0)),
        out_shape=jax.ShapeDtypeStruct((m, d), out_dtype),
        compiler_params=_cparams(("parallel",)),
        name="rmsnorm",
    )(x, g.reshape(1, d))


def _matmul_nt_kernel(x_ref, wt_ref, o_ref):
    o_ref[...] = lax.dot_general(x_ref[...], wt_ref[...], (((1,), (1,)), ((), ())), preferred_element_type=F32)


def _matmul_res_norm_kernel(x_ref, w_ref, r_ref, g_ref, *out_refs):
    y = r_ref[...] + jnp.dot(x_ref[...], w_ref[...], preferred_element_type=F32)
    ms = jnp.mean(y * y, axis=-1, keepdims=True)
    hn = (y * lax.rsqrt(ms + NORM_EPS)) * g_ref[...]
    if len(out_refs) == 2:
        out_refs[0][...] = y
    out_refs[-1][...] = hn.astype(out_refs[-1].dtype)


def matmul_res_norm(x, w, layer, residual, gain, *, tm, norm_dtype, emit_sum, name):
    m, k = x.shape
    n = w.shape[1]
    rows = lambda dt: (pl.BlockSpec((tm, n), lambda i: (i, 0)), jax.ShapeDtypeStruct((m, n), dt))
    outs = ([rows(F32)] if emit_sum else []) + [rows(norm_dtype)]
    res = pl.pallas_call(
        _matmul_res_norm_kernel,
        grid=(m // tm,),
        in_specs=[pl.BlockSpec((tm, k), lambda i: (i, 0)),
                  pl.BlockSpec((k, n), lambda i: (layer, 0), pipeline_mode=pl.Buffered(1)),
                  pl.BlockSpec((tm, n), lambda i: (i, 0)),
                  pl.BlockSpec((1, n), lambda i: (0, 0))],
        out_specs=[o[0] for o in outs],
        out_shape=[o[1] for o in outs],
        compiler_params=_cparams(("parallel",)),
        name=name,
    )(x, w, residual, gain.reshape(1, n))
    return tuple(res) if emit_sum else res[0]


def _rwkv_kernel(r_ref, k_ref, v_ref, gd_ref, wa_ref, lv_ref, vf_ref, p_ref, mu2_ref, w2_ref, a2_ref, g2_ref,
                 v2_ref, *out_and_scratch, tb, pp, has_vmix):
    *out_refs, s_ref, y_ref, br_ref, bk_ref, bv_ref, bgd_ref, bwa_ref = out_and_scratch
    i = pl.program_id(2)
    h = SUBLANES

    @pl.when(i == 0)
    def _():
        s_ref[...] = jnp.zeros_like(s_ref)
        for b in (br_ref, bk_ref, bv_ref, bgd_ref, bwa_ref):
            b[0:h, :] = jnp.zeros((h, b.shape[1]), F32)

    prm = p_ref[...]
    w0, a0, k_k, k_a, r_k, ln_w, ln_b, v0 = [prm[j:j + 1] for j in range(8)]
    mu_r, mu_k, mu_v = prm[8:9], prm[9:10], prm[10:11]
    mu2 = mu2_ref[...]
    mu_gd, mu_wa = mu2[:, 0:256], mu2[:, 256:384]

    def shifted_mix(x_ref, buf_ref, mu):
        x = x_ref[...]
        buf_ref[h:h + tb, :] = x
        xs = buf_ref[h - 1:h - 1 + tb, :]
        buf_ref[0:h, :] = x[tb - h:tb, :]
        return x + (xs - x) * mu

    r = shifted_mix(r_ref, br_ref, mu_r)
    k = shifted_mix(k_ref, bk_ref, mu_k)
    v = shifted_mix(v_ref, bv_ref, mu_v)
    gd = shifted_mix(gd_ref, bgd_ref, mu_gd)
    wa = shifted_mix(wa_ref, bwa_ref, mu_wa)

    lane = _iota2((1, LANES), 1)
    m0 = jnp.where(lane < A_HEAD, 1.0, 0.0)
    m1 = 1.0 - m0
    ri = _iota2((LANES, LANES), 0)
    ci = _iota2((LANES, LANES), 1)
    same = _chunk_of(ri) == _chunk_of(ci)
    strict = same & (ri > ci)
    incl = same & (ri >= ci)
    eye = jnp.where(ri == ci, 1.0, 0.0)
    tri = _chunk_tri(tb)
    pair_lanes = [slice(q * LANES, (q + 1) * LANES) for q in range(pp)]

    def head_sum(x):
        outs = []
        for ls in pair_lanes:
            s0 = jnp.sum(x[:, ls] * m0, axis=-1, keepdims=True)
            s1 = jnp.sum(x[:, ls] * m1, axis=-1, keepdims=True)
            outs.append(jnp.where(lane < A_HEAD, s0, s1))
        return jnp.concatenate(outs, axis=1)

    lw = -jnp.exp(-0.5) * _sigmoid(w0 + _dot(jnp.tanh(wa), w2_ref[...]))
    a = _sigmoid(a0 + _dot(wa, a2_ref[...]))
    g = _dot(_sigmoid(gd), g2_ref[...])
    kk = k * k_k
    kk = kk / jnp.maximum(jnp.sqrt(head_sum(kk * kk)), 1e-12)
    k = k * (1.0 + (a - 1.0) * k_a)
    if has_vmix:
        v_mix = _sigmoid(v0 + _dot(lv_ref[...], v2_ref[...]))
        v = v + (vf_ref[...] - v) * v_mix
    if len(out_refs) == 2:
        out_refs[1][...] = v

    cum = _dot_sel_l(tri, lw)
    e_pos = jnp.exp(cum)
    e_neg = jnp.exp(-cum)
    r_t = r * e_pos
    a_t = -kk * jnp.exp(cum - lw)
    b_t = (kk * a) * e_neg
    k_t = k * e_neg

    def stack_masked(x):
        return jnp.concatenate([x * m0, x * m1], axis=0)

    def stack_dup(x):
        return jnp.concatenate([x, x], axis=0)

    kb = kk * a
    nchunk = tb // CHUNK
    inst = [(c, q) for c in range(nchunk) for q in range(pp)]
    rows_of = lambda c: slice(c * CHUNK, (c + 1) * CHUNK)
    last_of = lambda c: slice((c + 1) * CHUNK - 1, (c + 1) * CHUNK)
    a_s = {cq: stack_masked(a_t[rows_of(cq[0]), pair_lanes[cq[1]]]) for cq in inst}
    r_s = {cq: stack_masked(r_t[rows_of(cq[0]), pair_lanes[cq[1]]]) for cq in inst}
    v_s = {cq: stack_masked(v[rows_of(cq[0]), pair_lanes[cq[1]]]) for cq in inst}
    gm = {}
    for c, q in inst:
        sl, ls = rows_of(c), pair_lanes[q]
        right = jnp.concatenate([stack_dup(b_t[sl, ls]), stack_dup(k_t[sl, ls])], axis=0)
        gm[c, q] = _dot_nt(jnp.concatenate([a_s[c, q], r_s[c, q]], axis=0), right)
    n_ab = [jnp.where(strict, gm[cq][0:LANES, 0:LANES], 0.0) for cq in inst]
    t_inv = dict(zip(inst, _neumann_inverse_many(n_ab, eye)))
    gv = {cq: _dot(jnp.where(strict, gm[cq][0:LANES, LANES:], 0.0), v_s[cq]) for cq in inst}
    tu = {cq: _dot(t_inv[cq], jnp.concatenate([a_s[cq], gv[cq]], axis=1)) for cq in inst}
    g_r = {cq: jnp.where(jnp.concatenate([incl, incl], axis=1), gm[cq][LANES:], 0.0) for cq in inst}

    states = [s_ref[q] for q in range(pp)]
    for c in range(nchunk):
        sl = rows_of(c)
        x0 = [_dot_nt(jnp.concatenate([tu[c, q][:, 0:LANES], r_s[c, q]], axis=0), states[q]) for q in range(pp)]
        uv = [jnp.concatenate([x0[q][0:LANES] + tu[c, q][:, LANES:], v_s[c, q]], axis=0) for q in range(pp)]
        for q, ls in enumerate(pair_lanes):
            cum_last = cum[last_of(c), ls]
            e_end = jnp.exp(cum_last - cum[sl, ls])
            bk_end = jnp.concatenate([stack_dup(kb[sl, ls] * e_end), stack_dup(k[sl, ls] * e_end)], axis=0)
            states[q] = jnp.where(same, states[q] * jnp.exp(cum_last) + _dot_tn(uv[q], bk_end), 0.0)
        for q, ls in enumerate(pair_lanes):
            y_s = x0[q][LANES:] + _dot(g_r[c, q], uv[q])
            y_ref[sl, ls] = y_s[0:CHUNK] + y_s[CHUNK:]
    for q in range(pp):
        s_ref[q] = states[q]

    y = y_ref[...]
    inv_n = 1.0 / A_HEAD
    mean = head_sum(y) * inv_n
    d = y - mean
    var = head_sum(d * d) * inv_n
    yn = d * lax.rsqrt(var + A_GN_EPS) * ln_w + ln_b
    bonus = head_sum(r * k * r_k) * v
    out_refs[0][...] = ((yn + bonus) * g).astype(out_refs[0].dtype)


def rwkv_mix(za, v_first, prm, mu2, w2p, a2p, g2p, v2p, *, batch, seq, tb=256, pp=4):
    has_vmix = v_first is not None
    nt = seq // tb
    w = pp * LANES
    npair = A_WIDTH // w
    row = lambda b, p, i: b * nt + i
    if v_first is None:
        v_first = za
    in_specs = [
        pl.BlockSpec((tb, w), lambda b, p, i: (row(b, p, i), p)),
        pl.BlockSpec((tb, w), lambda b, p, i: (row(b, p, i), npair + p)),
        pl.BlockSpec((tb, w), lambda b, p, i: (row(b, p, i), 2 * npair + p)),
        pl.BlockSpec((tb, 256), lambda b, p, i: (row(b, p, i), ZA_GD // 256)),
        pl.BlockSpec((tb, LANES), lambda b, p, i: (row(b, p, i), ZA_WA // LANES)),
        pl.BlockSpec((tb, LANES), lambda b, p, i: (row(b, p, i), ZA_LV // LANES)),
        pl.BlockSpec((tb, w), lambda b, p, i: (row(b, p, i), p)),
        pl.BlockSpec((16, w), lambda b, p, i: (0, p)),
        pl.BlockSpec((1, 384), lambda b, p, i: (0, 0)),
        pl.BlockSpec((LANES, w), lambda b, p, i: (0, p)),
        pl.BlockSpec((LANES, w), lambda b, p, i: (0, p)),
        pl.BlockSpec((256, w), lambda b, p, i: (0, p)),
        pl.BlockSpec((LANES, w), lambda b, p, i: (0, p)),
    ]
    out_spec = pl.BlockSpec((tb, w), lambda b, p, i: (row(b, p, i), p))
    out_dtypes = [BF16] if has_vmix else [BF16, F32]
    res = pl.pallas_call(
        functools.partial(_rwkv_kernel, tb=tb, pp=pp, has_vmix=has_vmix),
        grid=(batch, npair, nt),
        in_specs=in_specs,
        out_specs=[out_spec] * len(out_dtypes),
        out_shape=[jax.ShapeDtypeStruct((batch * seq, A_WIDTH), dt) for dt in out_dtypes],
        scratch_shapes=[
            pltpu.VMEM((pp, LANES, LANES), F32),
            pltpu.VMEM((tb, w), F32),
            pltpu.VMEM((tb + SUBLANES, w), F32),
            pltpu.VMEM((tb + SUBLANES, w), F32),
            pltpu.VMEM((tb + SUBLANES, w), F32),
            pltpu.VMEM((tb + SUBLANES, 256), F32),
            pltpu.VMEM((tb + SUBLANES, LANES), F32),
        ],
        compiler_params=_cparams(("parallel", "parallel", "arbitrary")),
        name="rwkv7",
    )(za, za, za, za, za, za, v_first, prm, mu2, w2p, a2p, g2p, v2p)
    return (res[0], None) if has_vmix else (res[0], res[1])


def _attn_kernel(*refs, tq, dil, nb, has_prev):
    if has_prev:
        (q_ref, kc_ref, kp_ref, vc_ref, vp_ref, cc_ref, sc_ref, nc_ref, cp_ref, sp_ref, np_ref,
         o_ref, lse_ref) = refs
    else:
        q_ref, kc_ref, vc_ref, cc_ref, sc_ref, nc_ref, o_ref, lse_ref = refs
    i = pl.program_id(1)
    scale = B_HEAD ** -0.5

    def rows(j, r):
        start = j * tq * dil + r
        return pl.ds(start, tq, stride=dil) if dil > 1 else pl.ds(start, tq)

    def rope(x, tabs):
        half = ROPE_DIM // 2
        return (x * tabs[0] + pltpu.roll(x, half, axis=1) * tabs[1]
                + pltpu.roll(x, LANES - half, axis=1) * tabs[2])

    if has_prev:
        ri, ci = _iota2((tq, 2 * tq), 0), _iota2((tq, 2 * tq), 1)
        cur_ok = (ci >= tq) & ((ci - tq) <= ri)
        no_prev = jnp.where(i > 0, 0, 2 * tq)
        valid_first = ((ci < tq) & (ci >= ri + no_prev)) | cur_ok
        valid_inner = ((ci < tq) & (ci >= ri)) | cur_ok
    else:
        causal = _iota2((tq, tq), 1) <= _iota2((tq, tq), 0)

    inst = [(r, j) for r in range(dil) for j in range(nb)]
    qs, ks, vs = {}, {}, {}
    for r in range(dil):
        if has_prev:
            rp = rows(0, r)
            ks[r, -1] = rope(kp_ref[rp, :], (cp_ref[rp, :], sp_ref[rp, :], np_ref[rp, :]))
            vs[r, -1] = vp_ref[rp, :]
        for j in range(nb):
            rw = rows(j, r)
            tabs = (cc_ref[rw, :], sc_ref[rw, :], nc_ref[rw, :])
            qs[r, j] = rope(q_ref[rw, :], tabs) * scale
            ks[r, j] = rope(kc_ref[rw, :], tabs)
            vs[r, j] = vc_ref[rw, :]
    if has_prev:
        s = {(r, j): jnp.where(valid_first if j == 0 else valid_inner,
                               _dot_nt(qs[r, j], jnp.concatenate([ks[r, j - 1], ks[r, j]], axis=0)), -1e30)
             for r, j in inst}
    else:
        s = {rj: jnp.where(causal, _dot_nt(qs[rj], ks[rj]), -1e30) for rj in inst}
    m = {rj: jnp.max(s[rj], axis=-1, keepdims=True) for rj in inst}
    p = {rj: jnp.exp(s[rj] - m[rj]) for rj in inst}
    den = {rj: jnp.sum(p[rj], axis=-1, keepdims=True) for rj in inst}
    if has_prev:
        num = {(r, j): _dot(p[r, j], jnp.concatenate([vs[r, j - 1], vs[r, j]], axis=0)) for r, j in inst}
    else:
        num = {rj: _dot(p[rj], vs[rj]) for rj in inst}
    for r, j in inst:
        o_ref[rows(j, r), :] = num[r, j] / den[r, j]
        lse_ref[rows(j, r), :] = jnp.broadcast_to(m[r, j] + jnp.log(den[r, j]), (tq, B_HEAD))


def dilated_attention_group(zb, rope_tabs, gi, *, batch, seq, col0=0, tq=128, blocks_per_step=16):
    win, dil = B_PAIRS[gi]
    assert win // dil == tq
    nq = seq // (dil * tq)
    has_prev = nq > 1
    nb = min(nq, max(1, blocks_per_step // dil))
    rows_step = nb * tq * dil
    nsteps = seq // rows_step
    prev_rows = tq * dil
    nprev = seq // prev_rows
    nh = B_HEADS_PER_GROUP
    cur = lambda col: pl.BlockSpec((rows_step, B_HEAD), lambda b, i, hh: (b * nsteps + i, col0 + col * nh + hh))
    prv = lambda col: pl.BlockSpec(
        (prev_rows, B_HEAD), lambda b, i, hh: (b * nprev + jnp.maximum(i * nb - 1, 0), col0 + col * nh + hh))
    tab_cur = pl.BlockSpec((rows_step, LANES), lambda b, i, hh: (i, 0))
    tab_prv = pl.BlockSpec((prev_rows, LANES), lambda b, i, hh: (jnp.maximum(i * nb - 1, 0), 0))
    qc, kc, vc = gi, B_GROUPS + gi, 2 * B_GROUPS + gi
    if has_prev:
        in_specs = [cur(qc), cur(kc), prv(kc), cur(vc), prv(vc)] + [tab_cur] * 3 + [tab_prv] * 3
        args = (zb, zb, zb, zb, zb) + tuple(rope_tabs) * 2
    else:
        in_specs = [cur(qc), cur(kc), cur(vc)] + [tab_cur] * 3
        args = (zb, zb, zb) + tuple(rope_tabs)
    out_spec = pl.BlockSpec((rows_step, B_HEAD), lambda b, i, hh: (b * nsteps + i, hh))
    out_sds = jax.ShapeDtypeStruct((batch * seq, B_OUT), F32)
    return pl.pallas_call(
        functools.partial(_attn_kernel, tq=tq, dil=dil, nb=nb, has_prev=has_prev),
        grid=(batch, nsteps, nh),
        in_specs=in_specs,
        out_specs=[out_spec, out_spec],
        out_shape=[out_sds, out_sds],
        compiler_params=_cparams(("parallel", "parallel", "arbitrary")),
        name=f"dilated_attn_g{gi}",
    )(*args)


def rope_tables(seq):
    half = ROPE_DIM // 2
    inv = ROPE_THETA ** (-jnp.arange(half, dtype=F32) / half)
    ang = jnp.arange(seq, dtype=F32)[:, None] * inv[None, :]
    cos, sin = jnp.cos(ang), jnp.sin(ang)
    z = jnp.zeros((seq, LANES - ROPE_DIM), F32)
    zh = jnp.zeros((seq, half), F32)
    c_tab = jnp.concatenate([cos, cos, jnp.ones_like(z)], axis=1)
    s_pos = jnp.concatenate([zh, sin, z], axis=1)
    s_neg = jnp.concatenate([-sin, zh, z], axis=1)
    return c_tab, s_pos, s_neg


def _gdn_kernel(q_ref, k_ref, v_ref, gate_ref, ba_ref, cw_ref, p_ref, out_ref, s_ref, o_ref, bq_ref, bk_ref,
                bv_ref, *, tb, pp):
    i = pl.program_id(2)
    h = SUBLANES
    hd = C_HEAD_K

    @pl.when(i == 0)
    def _():
        s_ref[...] = jnp.zeros_like(s_ref)
        for b in (bq_ref, bk_ref, bv_ref):
            b[0:h, :] = jnp.zeros((h, b.shape[1]), F32)

    cw = cw_ref[...]
    prm = p_ref[...]

    def conv_silu(x_ref, buf_ref, w):
        x = x_ref[...]
        buf_ref[h:h + tb, :] = x
        acc = x * w[C_CONV - 1:C_CONV]
        for j in range(C_CONV - 1):
            off = h - (C_CONV - 1) + j
            acc = acc + buf_ref[off:off + tb, :] * w[j:j + 1]
        buf_ref[0:h, :] = x[tb - h:tb, :]
        return acc * _sigmoid(acc)

    q = conv_silu(q_ref, bq_ref, cw[0])
    k = conv_silu(k_ref, bk_ref, cw[1])
    v = conv_silu(v_ref, bv_ref, cw[2])

    ri = _iota2((LANES, LANES), 0)
    ci = _iota2((LANES, LANES), 1)
    same = _chunk_of(ri) == _chunk_of(ci)
    strict = same & (ri > ci)
    incl = same & (ri >= ci)
    eye = jnp.where(ri == ci, 1.0, 0.0)
    tri = _chunk_tri(tb)

    ba = ba_ref[...]
    nh = 2 * pp
    head_lanes = [slice(hh * hd, (hh + 1) * hd) for hh in range(nh)]
    vh = [v[:, ls] for ls in head_lanes]
    q_ss = [jnp.sum(q[:, ls] * q[:, ls], axis=-1, keepdims=True) for ls in head_lanes]
    k_ss = [jnp.sum(k[:, ls] * k[:, ls], axis=-1, keepdims=True) for ls in head_lanes]
    b_raw = [ba[:, hh:hh + 1] for hh in range(nh)]
    a_raw = [ba[:, C_HEADS + hh:C_HEADS + hh + 1] for hh in range(nh)]
    qh = [q[:, ls] / jnp.maximum(jnp.sqrt(ss), 1e-12) * (hd ** -0.5) for ls, ss in zip(head_lanes, q_ss)]
    kh = [k[:, ls] / jnp.maximum(jnp.sqrt(ss), 1e-12) for ls, ss in zip(head_lanes, k_ss)]
    beta = [_sigmoid(b) for b in b_raw]
    glog = [-jnp.exp(prm[0:1, ls]) * _softplus(al + prm[1:2, ls]) for ls, al in zip(head_lanes, a_raw)]
    gam = [_dot_sel_l(tri, gl) for gl in glog]

    nchunk = tb // CHUNK
    inst = [(c, pr) for c in range(nchunk) for pr in range(pp)]
    rows_of = lambda c: slice(c * CHUNK, (c + 1) * CHUNK)

    def stack(xs, cp):
        c, pr = cp
        return jnp.concatenate([xs[2 * pr][rows_of(c)], xs[2 * pr + 1][rows_of(c)]], axis=0)

    k_s = {cp: stack(kh, cp) for cp in inst}
    q_s = {cp: stack(qh, cp) for cp in inst}
    beta_s = {cp: stack(beta, cp) for cp in inst}
    gam_s = {cp: stack(gam, cp) for cp in inst}
    kq = {cp: _dot_nt(jnp.concatenate([k_s[cp], q_s[cp]], axis=0), k_s[cp]) for cp in inst}
    dm = {cp: jnp.exp(jnp.where(incl, gam_s[cp] - gam_s[cp].T, -1e30)) for cp in inst}
    n_mats = [jnp.where(strict, -(beta_s[cp] * kq[cp][0:LANES] * dm[cp]), 0.0) for cp in inst]
    t_inv = dict(zip(inst, _neumann_inverse_many(n_mats, eye)))
    e_gam = {cp: jnp.exp(gam_s[cp]) for cp in inst}
    sol = {cp: _dot(t_inv[cp], jnp.concatenate([stack(vh, cp) * beta_s[cp],
                                                k_s[cp] * (beta_s[cp] * e_gam[cp])], axis=1)) for cp in inst}

    states = [s_ref[hh] for hh in range(nh)]
    for c in range(nchunk):
        sl = rows_of(c)
        ws = []
        for hh in range(nh):
            cp, hs = (c, hh // 2), slice((hh % 2) * CHUNK, (hh % 2 + 1) * CHUNK)
            qg = (q_s[cp] * e_gam[cp])[hs]
            ws.append(_dot(jnp.concatenate([sol[cp][hs, hd:], qg], axis=0), states[hh]))
        v_new = [sol[c, hh // 2][(hh % 2) * CHUNK:(hh % 2 + 1) * CHUNK, 0:hd] - ws[hh][0:CHUNK]
                 for hh in range(nh)]
        for hh in range(nh):
            g_h = gam[hh][sl]
            g_last = g_h[CHUNK - 1:CHUNK, :]
            states[hh] = states[hh] * jnp.exp(g_last) + _dot_tn(kh[hh][sl] * jnp.exp(g_last - g_h), v_new[hh])
        for pr in range(pp):
            h0 = 2 * pr
            attn = kq[c, pr][LANES:] * dm[c, pr]
            o_s = (jnp.concatenate([ws[h0][CHUNK:], ws[h0 + 1][CHUNK:]], axis=0)
                   + _dot(attn, jnp.concatenate([v_new[h0], v_new[h0 + 1]], axis=0)))
            o_ref[sl, h0 * hd:(h0 + 1) * hd] = o_s[0:CHUNK]
            o_ref[sl, (h0 + 1) * hd:(h0 + 2) * hd] = o_s[CHUNK:]
    for hh in range(nh):
        s_ref[hh] = states[hh]

    gate = gate_ref[...]
    o_h = [o_ref[:, ls] for ls in head_lanes]
    o_ms = [jnp.mean(o * o, axis=-1, keepdims=True) for o in o_h]
    for o, ms, ls in zip(o_h, o_ms, head_lanes):
        gt = gate[:, ls]
        out_ref[:, ls] = ((o * lax.rsqrt(ms + NORM_EPS) * prm[2:3, ls]) * (gt * _sigmoid(gt))).astype(out_ref.dtype)


def gated_deltanet(zc, conv_w, prm, *, batch, seq, tb=256, pp=4):
    assert 2 * pp == C_HEADS, "the kernel indexes the per-head beta/alpha columns statically"
    nt = seq // tb
    npair = C_HEADS // (2 * pp)
    wblk = 2 * pp * C_HEAD_K
    row = lambda b, p, i: b * nt + i
    in_specs = [
        pl.BlockSpec((tb, wblk), lambda b, p, i: (row(b, p, i), p)),
        pl.BlockSpec((tb, wblk), lambda b, p, i: (row(b, p, i), npair + p)),
        pl.BlockSpec((tb, wblk), lambda b, p, i: (row(b, p, i), 2 * npair + p)),
        pl.BlockSpec((tb, wblk), lambda b, p, i: (row(b, p, i), 3 * npair + p)),
        pl.BlockSpec((tb, LANES), lambda b, p, i: (row(b, p, i), ZC_BA // LANES)),
        pl.BlockSpec((3, C_CONV, wblk), lambda b, p, i: (0, 0, p)),
        pl.BlockSpec((SUBLANES, wblk), lambda b, p, i: (0, p)),
    ]
    return pl.pallas_call(
        functools.partial(_gdn_kernel, tb=tb, pp=pp),
        grid=(batch, npair, nt),
        in_specs=in_specs,
        out_specs=pl.BlockSpec((tb, wblk), lambda b, p, i: (row(b, p, i), p)),
        out_shape=jax.ShapeDtypeStruct((batch * seq, C_VW), BF16),
        scratch_shapes=[
            pltpu.VMEM((2 * pp, C_HEAD_K, C_HEAD_V), F32),
            pltpu.VMEM((tb, wblk), F32),
            pltpu.VMEM((tb + SUBLANES, wblk), F32),
            pltpu.VMEM((tb + SUBLANES, wblk), F32),
            pltpu.VMEM((tb + SUBLANES, wblk), F32),
        ],
        compiler_params=_cparams(("parallel", "parallel", "arbitrary")),
        name="gated_deltanet",
    )(zc, zc, zc, zc, zc, conv_w, prm)


def _merge_out_kernel(ya_ref, o0_ref, o1_ref, o2_ref, l0_ref, l1_ref, l2_ref, yc_ref, ga_ref, gb_ref, gc_ref,
                      pa_ref, pb_ref, pc_ref, wo_ref, x_ref, g_ref, y_ref, h_ref):
    l0, l1, l2 = l0_ref[...], l1_ref[...], l2_ref[...]
    m = jnp.maximum(jnp.maximum(l0, l1), l2)
    w0, w1, w2 = jnp.exp(l0 - m), jnp.exp(l1 - m), jnp.exp(l2 - m)
    yb = (w0 * o0_ref[...] + w1 * o1_ref[...] + w2 * o2_ref[...]) / (w0 + w1 + w2)
    merged = (_sigmoid(ga_ref[...]) * _dot(ya_ref[...], pa_ref[...])
              + _sigmoid(gb_ref[...]) * _dot(yb, pb_ref[...])
              + _sigmoid(gc_ref[...]) * _dot(yc_ref[...], pc_ref[...]))
    y = x_ref[...] + _dot(merged, wo_ref[...])
    ms = jnp.mean(y * y, axis=-1, keepdims=True)
    y_ref[...] = y
    h_ref[...] = ((y * lax.rsqrt(ms + NORM_EPS)) * g_ref[...]).astype(h_ref.dtype)


def merge_out_proj(ya, attn, yc, zg, pa, pb, pc, w_out, layer, x, gain, tm=256):
    m = ya.shape[0]
    d = D_MODEL
    rows = lambda w: pl.BlockSpec((tm, w), lambda i: (i, 0))
    const = lambda a: pl.BlockSpec(a.shape, lambda i: (0, 0), pipeline_mode=pl.Buffered(1))
    (o0, l0), (o1, l1), (o2, l2) = attn
    in_specs = ([rows(A_WIDTH)] + [rows(B_OUT)] * 6 + [rows(C_VW)]
                + [pl.BlockSpec((tm, d), lambda i, j=j: (i, j)) for j in range(3)]
                + [const(pa), const(pb), const(pc),
                   pl.BlockSpec((d, d), lambda i: (layer, 0), pipeline_mode=pl.Buffered(1)),
                   rows(d), pl.BlockSpec((1, d), lambda i: (0, 0))])
    return pl.pallas_call(
        _merge_out_kernel,
        grid=(m // tm,),
        in_specs=in_specs,
        out_specs=[rows(d), rows(d)],
        out_shape=[jax.ShapeDtypeStruct((m, d), F32), jax.ShapeDtypeStruct((m, d), BF16)],
        compiler_params=_cparams(("parallel",)),
        name="merge_out_proj",
    )(ya, o0, o1, o2, l0, l1, l2, yc, zg, zg, zg, pa, pb, pc, w_out, x, gain.reshape(1, d))


def _ffn_up_act_kernel(x_ref, wg_ref, wv_ref, cw_ref, o_ref, buf_ref, halo_ref, *, tm, tf, blocks_per_seq):
    i = pl.program_id(0)
    j = pl.program_id(1)
    h = SUBLANES
    first = (i % blocks_per_seq) == 0

    @pl.when(first)
    def _():
        buf_ref[0:h, :] = jnp.zeros((h, 2 * tf), F32)

    @pl.when(jnp.logical_not(first))
    def _():
        buf_ref[0:h, :] = halo_ref[j]

    cw = cw_ref[...]

    def project(s):
        cs = slice(s * LANES, (s + 1) * LANES)
        w = jnp.concatenate([wg_ref[:, cs], wv_ref[:, cs]], axis=1)
        return jnp.dot(x_ref[...], w, preferred_element_type=F32)

    def conv(u, cols):
        buf_ref[h:h + tm, cols] = u
        halo_ref[j, :, cols] = u[tm - h:tm, :]
        acc = u * cw[FFN_CONV - 1:FFN_CONV, cols]
        for t in range(FFN_CONV - 1):
            off = h - (FFN_CONV - 1) + t
            acc = acc + buf_ref[off:off + tm, cols] * cw[t:t + 1, cols]
        return acc

    def gate(s, u):
        cg = conv(u[:, 0:LANES], slice(s * LANES, (s + 1) * LANES))
        cv = conv(u[:, LANES:], slice(tf + s * LANES, tf + (s + 1) * LANES))
        o_ref[:, s * LANES:(s + 1) * LANES] = ((cg * _sigmoid(cg)) * cv).astype(o_ref.dtype)

    nsub = tf // LANES
    pending = project(0)
    for s in range(nsub):
        nxt = project(s + 1) if s + 1 < nsub else None
        gate(s, pending)
        pending = nxt


def ffn_up_act(h2, w_up, layer, conv_w, *, seq, tm=1024, tf=FFN_TF):
    m, d = h2.shape
    tm = min(tm, seq)
    nf = D_FF // tf
    return pl.pallas_call(
        functools.partial(_ffn_up_act_kernel, tm=tm, tf=tf, blocks_per_seq=seq // tm),
        grid=(m // tm, nf),
        in_specs=[
            pl.BlockSpec((tm, d), lambda i, j: (i, 0)),
            pl.BlockSpec((d, tf), lambda i, j: (layer, j)),
            pl.BlockSpec((d, tf), lambda i, j: (layer, nf + j)),
            pl.BlockSpec((None, FFN_CONV, 2 * tf), lambda i, j: (j, 0, 0)),
        ],
        out_specs=pl.BlockSpec((tm, tf), lambda i, j: (i, j)),
        out_shape=jax.ShapeDtypeStruct((m, D_FF), BF16),
        scratch_shapes=[
            pltpu.VMEM((tm + SUBLANES, 2 * tf), F32),
            pltpu.VMEM((nf, SUBLANES, 2 * tf), F32),
        ],
        compiler_params=_cparams(("arbitrary", "arbitrary")),
        name="ffn_up_act",
    )(h2, w_up, w_up, conv_w)


WT_ZA, WT_ZB, WT_ZC, WT_ZG = 0, ZA_W, ZA_W + B_IN, ZA_W + B_IN + ZC_W
WT_TILES_PER_STEP = 4
WT_ROWS = -(-(WT_ZG + 3 * D_MODEL) // (WT_TILES_PER_STEP * LANES)) * (WT_TILES_PER_STEP * LANES)


def _w_in_tile_table():
    aw = 3 * A_WIDTH
    lora = A_DECAY_LORA + A_ICLR_LORA
    c0 = A_IN + B_IN
    qkv = 2 * C_KW + C_VW
    runs = [(0, aw), (aw + lora, A_GATE_LORA), (None, ZA_WA - ZA_GD - A_GATE_LORA), (aw, lora),
            (None, ZA_W - ZA_WA - lora),
            (A_IN, B_IN),
            (c0, qkv), (c0 + qkv + 2 * C_HEADS, C_VW), (c0 + qkv, 2 * C_HEADS), (None, ZC_W - ZC_BA - 2 * C_HEADS),
            (c0 + C_IN, 3 * D_MODEL)]
    src, valid = [], []
    pending = 0
    for start, n in runs:
        if start is None:
            assert pending + n == LANES or pending == 0 and n % LANES == 0
            if pending == 0:
                src += [0] * (n // LANES)
                valid += [0] * (n // LANES)
            pending = 0
            continue
        assert pending == 0
        for off in range(0, n, LANES):
            src.append(start + off)
            valid.append(min(LANES, n - off))
        pending = n % LANES
    assert pending == 0 and len(src) == (WT_ZG + 3 * D_MODEL) // LANES
    tail = WT_ROWS // LANES - len(src)
    return src + [0] * tail, valid + [0] * tail


def _prep_w_in_kernel(src_ref, valid_ref, *refs):
    *w_refs, lora_ref, o_ref = refs
    g_n = len(w_refs)
    t = pl.program_id(0)
    rows = _iota2(w_refs[0].shape, 0)
    lv_tile = (WT_ZA + ZA_LV) // LANES
    for g, w_ref in enumerate(w_refs):
        tile = jnp.where(rows < valid_ref[t * g_n + g], w_ref[...], 0.0)
        if g == lv_tile % g_n:
            tile = jnp.where(t == lv_tile // g_n, lora_ref[...], tile)
        o_ref[g * LANES:(g + 1) * LANES, :] = tile.astype(BF16)


def prep_w_in(w_in, l, lora_t):
    depth, d, n_in = w_in.shape
    wt = jnp.swapaxes(w_in, 1, 2).reshape(depth * n_in, d)
    src, valid = _w_in_tile_table()
    src = jnp.asarray(src, jnp.int32) + l * n_in
    valid = jnp.asarray(valid, jnp.int32)
    g = WT_TILES_PER_STEP
    tile_spec = lambda k: pl.BlockSpec(
        (pl.Element(LANES), pl.Element(d)),
        lambda t, src, valid: (pl.multiple_of(src[t * g + k], 2 * SUBLANES), 0))
    return pl.pallas_call(
        _prep_w_in_kernel,
        grid_spec=pltpu.PrefetchScalarGridSpec(
            num_scalar_prefetch=2, grid=(WT_ROWS // (g * LANES),),
            in_specs=[tile_spec(k) for k in range(g)] + [pl.BlockSpec((LANES, d), lambda t, src, valid: (0, 0))],
            out_specs=pl.BlockSpec((g * LANES, d), lambda t, src, valid: (t, 0))),
        out_shape=jax.ShapeDtypeStruct((WT_ROWS, d), BF16),
        compiler_params=_cparams(("parallel",)),
        name="prep_w_in",
    )(src, valid, *([wt] * g), lora_t)


def matmul_nt(x, wt, row0, n, tn, tm=2048, name="matmul_nt"):
    m, k = x.shape
    return pl.pallas_call(
        _matmul_nt_kernel,
        grid=(m // tm, n // tn),
        in_specs=[pl.BlockSpec((tm, k), lambda i, j: (i, 0)),
                  pl.BlockSpec((pl.Element(tn), pl.Element(k)),
                               lambda i, j: (pl.multiple_of(row0 + j * tn, 2 * SUBLANES), 0))],
        out_specs=pl.BlockSpec((tm, tn), lambda i, j: (i, j)),
        out_shape=jax.ShapeDtypeStruct((m, n), F32),
        compiler_params=_cparams(("parallel", "parallel")),
        name=name,
    )(x, wt)


def _pad_cols(w, width):
    return jnp.pad(w, ((0, 0), (0, width - w.shape[1])))


def _pad_rows(w, rows, at=0):
    return jnp.pad(w, ((at, rows - at - w.shape[0]), (0, 0)))


def _layer_params(l, p):
    aw = 3 * A_WIDTH
    mu = p["rwkv_mu"][l]
    rows = [p["rwkv_w0"][l], p["rwkv_a0"][l], p["rwkv_k_k"][l], p["rwkv_k_a"][l], p["rwkv_r_k"][l].reshape(-1),
            p["rwkv_ln_w"][l], p["rwkv_ln_b"][l],
            (p["rwkv_v0"][l - 1] if l > 0 else jnp.zeros((A_WIDTH,), F32)),
            mu[:A_WIDTH], mu[A_WIDTH:2 * A_WIDTH], mu[2 * A_WIDTH:aw]]
    rwkv_prm = jnp.pad(jnp.stack(rows), ((0, 16 - len(rows)), (0, 0)))
    mu2 = jnp.concatenate([_pad_cols(mu[None, aw + 128:], 256), mu[None, aw:aw + 128]], axis=1)
    v2 = (p["rwkv_v2"][l - 1] if l > 0 else jnp.zeros((A_VRES_LORA, A_WIDTH), F32))
    bcast = lambda t: jnp.repeat(t, C_HEAD_K)
    gdn_prm = jnp.pad(jnp.stack([bcast(p["gdn_A_log"][l]), bcast(p["gdn_dt_bias"][l]),
                                 jnp.tile(p["gdn_norm"][l], C_HEADS)]), ((0, SUBLANES - 3), (0, 0)))
    fc = p["ffn_conv"][l]
    return dict(
        lora_t=(_pad_rows(p["rwkv_v1"][l - 1].T, LANES) if l > 0 else jnp.zeros((LANES, D_MODEL), F32)),
        rwkv_prm=rwkv_prm, mu2=mu2,
        w2p=_pad_rows(p["rwkv_w2"][l], 128, 0).astype(BF16),
        a2p=_pad_rows(p["rwkv_a2"][l], 128, A_DECAY_LORA).astype(BF16),
        g2p=_pad_rows(p["rwkv_g2"][l], 256, 0).astype(BF16),
        v2p=_pad_rows(v2, 128, 0).astype(BF16),
        gdn_conv=p["gdn_conv"][l].reshape(C_CONV, 3, C_KW).transpose(1, 0, 2),
        gdn_prm=gdn_prm,
        pa=p["proj_a"][l].astype(BF16), pb=p["proj_b"][l].astype(BF16), pc=p["proj_c"][l].astype(BF16),
        ffn_conv=fc.reshape(FFN_CONV, 2, D_FF // FFN_TF, FFN_TF).transpose(2, 0, 1, 3).reshape(
            D_FF // FFN_TF, FFN_CONV, 2 * FFN_TF),
    )


def _forward(x, p):
    batch, seq, d = x.shape
    m = batch * seq
    xf = x.reshape(m, d)
    tab = rope_tables(seq)
    v_first = None
    stack_rows = lambda w: w.astype(BF16).reshape(w.shape[0] * w.shape[1], w.shape[2])
    w_out_all, ffn_up_all, ffn_down_all = stack_rows(p["w_out"]), stack_rows(p["ffn_up"]), stack_rows(p["ffn_down"])
    h = rmsnorm(xf, p["attn_norm"][0], BF16)
    for l in range(DEPTH):
        lp = _layer_params(l, p)
        wt = prep_w_in(p["w_in"], l, lp["lora_t"])
        zab = matmul_nt(h, wt, WT_ZA, ZA_W + B_IN, tn=1024, name="in_proj_ab")
        zc = matmul_nt(h, wt, WT_ZC, ZC_W, tn=1408, tm=1024, name="in_proj_c")
        zg = matmul_nt(h, wt, WT_ZG, 3 * D_MODEL, tn=1024, name="in_proj_g")
        ya, v_l = rwkv_mix(zab, v_first, lp["rwkv_prm"], lp["mu2"], lp["w2p"], lp["a2p"], lp["g2p"], lp["v2p"],
                           batch=batch, seq=seq)
        if l == 0:
            v_first = v_l
        attn = [dilated_attention_group(zab, tab, gi, batch=batch, seq=seq, col0=ZA_W // B_HEAD)
                for gi in range(B_GROUPS)]
        yc = gated_deltanet(zc, lp["gdn_conv"], lp["gdn_prm"], batch=batch, seq=seq)
        xf, h2 = merge_out_proj(ya, attn, yc, zg, lp["pa"], lp["pb"], lp["pc"], w_out_all, l, xf, p["ffn_norm"][l])
        act = ffn_up_act(h2, ffn_up_all, l, lp["ffn_conv"], seq=seq)
        if l + 1 < DEPTH:
            xf, h = matmul_res_norm(act, ffn_down_all, l, xf, p["attn_norm"][l + 1], tm=256, norm_dtype=BF16,
                                    emit_sum=True, name="ffn_down_norm")
        else:
            out = matmul_res_norm(act, ffn_down_all, l, xf, p["final_norm"], tm=256, norm_dtype=F32,
                                  emit_sum=False, name="ffn_down_norm")
    return out.reshape(batch, seq, d)


def kernel(x, attn_norm, w_in, rwkv_mu, rwkv_w0, rwkv_w2, rwkv_a0, rwkv_a2, rwkv_g2, rwkv_k_k, rwkv_k_a, rwkv_r_k, rwkv_ln_w, rwkv_ln_b, rwkv_v0, rwkv_v1, rwkv_v2, gdn_conv, gdn_A_log, gdn_dt_bias, gdn_norm, proj_a, proj_b, proj_c, w_out, ffn_norm, ffn_up, ffn_conv, ffn_down, final_norm):
    params = dict(
        attn_norm=attn_norm, w_in=w_in, rwkv_mu=rwkv_mu, rwkv_w0=rwkv_w0, rwkv_w2=rwkv_w2, rwkv_a0=rwkv_a0,
        rwkv_a2=rwkv_a2, rwkv_g2=rwkv_g2, rwkv_k_k=rwkv_k_k, rwkv_k_a=rwkv_k_a, rwkv_r_k=rwkv_r_k,
        rwkv_ln_w=rwkv_ln_w, rwkv_ln_b=rwkv_ln_b, rwkv_v0=rwkv_v0, rwkv_v1=rwkv_v1, rwkv_v2=rwkv_v2,
        gdn_conv=gdn_conv, gdn_A_log=gdn_A_log, gdn_dt_bias=gdn_dt_bias, gdn_norm=gdn_norm, proj_a=proj_a,
        proj_b=proj_b, proj_c=proj_c, w_out=w_out, ffn_norm=ffn_norm, ffn_up=ffn_up, ffn_conv=ffn_conv,
        ffn_down=ffn_down, final_norm=final_norm)
    return _forward(x, params)
```

```python
import functools

import jax
import jax.numpy as jnp
from jax import lax
from jax.experimental import pallas as pl
from jax.experimental.pallas import tpu as pltpu

F32 = jnp.float32
BF16 = jnp.bfloat16

D_MODEL = 2048
DEPTH = 2
NORM_EPS = 1e-6

A_HEADS, A_HEAD = 16, 64
A_WIDTH = A_HEADS * A_HEAD
A_DECAY_LORA, A_ICLR_LORA, A_GATE_LORA, A_VRES_LORA = 64, 64, 160, 32
A_GN_EPS = 64e-5
A_IN = 3 * A_WIDTH + A_DECAY_LORA + A_ICLR_LORA + A_GATE_LORA

B_PAIRS = ((128, 1), (512, 4), (2048, 16))
B_GROUPS = 3
B_HEADS_PER_GROUP, B_HEAD = 4, 128
B_WIDTH = B_GROUPS * B_HEADS_PER_GROUP * B_HEAD
B_OUT = B_HEADS_PER_GROUP * B_HEAD
B_IN = 3 * B_WIDTH
ROPE_THETA = 500000.0
ROPE_DIM = B_HEAD // 4

C_HEADS, C_HEAD_K, C_HEAD_V = 8, 128, 128
C_KW = C_HEADS * C_HEAD_K
C_VW = C_HEADS * C_HEAD_V
C_CONV = 4
C_IN = 2 * C_KW + C_VW + 2 * C_HEADS + C_VW

D_FF = 5632
FFN_CONV = 3

LANES = 128
SUBLANES = 8
CHUNK = 64
FFN_TF = 1408
VMEM_LIMIT = 56 * 1024 * 1024

ZA_GD = 3 * A_WIDTH
ZA_WA = ZA_GD + 256
ZA_LV = ZA_WA + 128
ZA_W = ZA_LV + 128
ZC_BA = 3 * C_KW + C_VW
ZC_W = ZC_BA + 128


def _cparams(sem):
    return pltpu.CompilerParams(dimension_semantics=sem, vmem_limit_bytes=VMEM_LIMIT)


def _dot(a, b):
    return jnp.dot(a.astype(BF16), b.astype(BF16), preferred_element_type=F32)


def _dot_nt(a, b):
    return lax.dot_general(a.astype(BF16), b.astype(BF16), (((1,), (1,)), ((), ())), preferred_element_type=F32)


def _dot_tn(a, b):
    return lax.dot_general(a.astype(BF16), b.astype(BF16), (((0,), (0,)), ((), ())), preferred_element_type=F32)


def _split(a):
    hi = a.astype(BF16)
    lo = (a - hi.astype(F32)).astype(BF16)
    return hi, lo


def _dot_sel_l(sel, a):
    hi, lo = _split(a)
    return (jnp.dot(sel, hi, preferred_element_type=F32) + jnp.dot(sel, lo, preferred_element_type=F32))


def _sigmoid(x):
    return 1.0 / (1.0 + jnp.exp(-x))


def _softplus(x):
    return jnp.maximum(x, 0.0) + jnp.log(1.0 + jnp.exp(-jnp.abs(x)))


def _iota2(shape, axis):
    return lax.broadcasted_iota(jnp.int32, shape, axis)


def _chunk_of(idx):
    return jnp.right_shift(idx, CHUNK.bit_length() - 1)


def _chunk_tri(n):
    ri, ci = _iota2((n, n), 0), _iota2((n, n), 1)
    return jnp.where((_chunk_of(ri) == _chunk_of(ci)) & (ri >= ci), 1.0, 0.0).astype(BF16)


def _neumann_inverse_many(n_mats, eye):
    ps = [eye + n for n in n_mats]
    qs = [_dot(n, n) for n in n_mats]
    levels = CHUNK.bit_length() - 2
    for lvl in range(levels):
        if lvl == levels - 1:
            ps = [p + _dot(p, q) for p, q in zip(ps, qs)]
        else:
            prods = [_dot(jnp.concatenate([q, p], axis=0), q) for p, q in zip(ps, qs)]
            ps = [p + pr[LANES:] for p, pr in zip(ps, prods)]
            qs = [pr[0:LANES] for pr in prods]
    return ps


def _neumann_inverse(n_mat, eye):
    return _neumann_inverse_many([n_mat], eye)[0]


def _rmsnorm_kernel(x_ref, g_ref, o_ref):
    x = x_ref[...]
    ms = jnp.mean(x * x, axis=-1, keepdims=True)
    o_ref[...] = ((x * lax.rsqrt(ms + NORM_EPS)) * g_ref[...]).astype(o_ref.dtype)


def rmsnorm(x, g, out_dtype, tm=512):
    m, d = x.shape
    return pl.pallas_call(
        _rmsnorm_kernel,
        grid=(m // tm,),
        in_specs=[pl.BlockSpec((tm, d), lambda i: (i, 0)), pl.BlockSpec((1, d), lambda i: (0, 0))],
        out_specs=pl.BlockSpec((tm, d), lambda i: (i, 0)),
        out_shape=jax.ShapeDtypeStruct((m, d), out_dtype),
        compiler_params=_cparams(("parallel",)),
        name="rmsnorm",
    )(x, g.reshape(1, d))


def _matmul_nt_kernel(x_ref, wt_ref, o_ref):
    o_ref[...] = lax.dot_general(x_ref[...], wt_ref[...], (((1,), (1,)), ((), ())), preferred_element_type=F32)


def _matmul_res_norm_kernel(x_ref, w_ref, r_ref, g_ref, *out_refs):
    y = r_ref[...] + jnp.dot(x_ref[...], w_ref[...], preferred_element_type=F32)
    ms = jnp.mean(y * y, axis=-1, keepdims=True)
    hn = (y * lax.rsqrt(ms + NORM_EPS)) * g_ref[...]
    if len(out_refs) == 2:
        out_refs[0][...] = y
    out_refs[-1][...] = hn.astype(out_refs[-1].dtype)


def matmul_res_norm(x, w, layer, residual, gain, *, tm, norm_dtype, emit_sum, name):
    m, k = x.shape
    n = w.shape[1]
    rows = lambda dt: (pl.BlockSpec((tm, n), lambda i: (i, 0)), jax.ShapeDtypeStruct((m, n), dt))
    outs = ([rows(F32)] if emit_sum else []) + [rows(norm_dtype)]
    res = pl.pallas_call(
        _matmul_res_norm_kernel,
        grid=(m // tm,),
        in_specs=[pl.BlockSpec((tm, k), lambda i: (i, 0)),
                  pl.BlockSpec((k, n), lambda i: (layer, 0), pipeline_mode=pl.Buffered(1)),
                  pl.BlockSpec((tm, n), lambda i: (i, 0)),
                  pl.BlockSpec((1, n), lambda i: (0, 0))],
        out_specs=[o[0] for o in outs],
        out_shape=[o[1] for o in outs],
        compiler_params=_cparams(("parallel",)),
        name=name,
    )(x, w, residual, gain.reshape(1, n))
    return tuple(res) if emit_sum else res[0]


def _rwkv_kernel(r_ref, k_ref, v_ref, gd_ref, wa_ref, lv_ref, vf_ref, p_ref, mu2_ref, w2_ref, a2_ref, g2_ref,
                 v2_ref, *out_and_scratch, tb, pp, has_vmix):
    *out_refs, s_ref, y_ref, br_ref, bk_ref, bv_ref, bgd_ref, bwa_ref = out_and_scratch
    i = pl.program_id(2)
    h = SUBLANES

    @pl.when(i == 0)
    def _():
        s_ref[...] = jnp.zeros_like(s_ref)
        for b in (br_ref, bk_ref, bv_ref, bgd_ref, bwa_ref):
            b[0:h, :] = jnp.zeros((h, b.shape[1]), F32)

    prm = p_ref[...]
    w0, a0, k_k, k_a, r_k, ln_w, ln_b, v0 = [prm[j:j + 1] for j in range(8)]
    mu_r, mu_k, mu_v = prm[8:9], prm[9:10], prm[10:11]
    mu2 = mu2_ref[...]
    mu_gd, mu_wa = mu2[:, 0:256], mu2[:, 256:384]

    def shifted_mix(x_ref, buf_ref, mu):
        x = x_ref[...]
        buf_ref[h:h + tb, :] = x
        xs = buf_ref[h - 1:h - 1 + tb, :]
        buf_ref[0:h, :] = x[tb - h:tb, :]
        return x + (xs - x) * mu

    r = shifted_mix(r_ref, br_ref, mu_r)
    k = shifted_mix(k_ref, bk_ref, mu_k)
    v = shifted_mix(v_ref, bv_ref, mu_v)
    gd = shifted_mix(gd_ref, bgd_ref, mu_gd)
    wa = shifted_mix(wa_ref, bwa_ref, mu_wa)

    lane = _iota2((1, LANES), 1)
    m0 = jnp.where(lane < A_HEAD, 1.0, 0.0)
    m1 = 1.0 - m0
    ri = _iota2((LANES, LANES), 0)
    ci = _iota2((LANES, LANES), 1)
    same = _chunk_of(ri) == _chunk_of(ci)
    strict = same & (ri > ci)
    incl = same & (ri >= ci)
    eye = jnp.where(ri == ci, 1.0, 0.0)
    tri = _chunk_tri(tb)
    pair_lanes = [slice(q * LANES, (q + 1) * LANES) for q in range(pp)]

    def head_sum(x):
        outs = []
        for ls in pair_lanes:
            s0 = jnp.sum(x[:, ls] * m0, axis=-1, keepdims=True)
            s1 = jnp.sum(x[:, ls] * m1, axis=-1, keepdims=True)
            outs.append(jnp.where(lane < A_HEAD, s0, s1))
        return jnp.concatenate(outs, axis=1)

    lw = -jnp.exp(-0.5) * _sigmoid(w0 + _dot(jnp.tanh(wa), w2_ref[...]))
    a = _sigmoid(a0 + _dot(wa, a2_ref[...]))
    g = _dot(_sigmoid(gd), g2_ref[...])
    kk = k * k_k
    kk = kk / jnp.maximum(jnp.sqrt(head_sum(kk * kk)), 1e-12)
    k = k * (1.0 + (a - 1.0) * k_a)
    if has_vmix:
        v_mix = _sigmoid(v0 + _dot(lv_ref[...], v2_ref[...]))
        v = v + (vf_ref[...] - v) * v_mix
    if len(out_refs) == 2:
        out_refs[1][...] = v

    cum = _dot_sel_l(tri, lw)
    e_pos = jnp.exp(cum)
    e_neg = jnp.exp(-cum)
    r_t = r * e_pos
    a_t = -kk * jnp.exp(cum - lw)
    b_t = (kk * a) * e_neg
    k_t = k * e_neg

    def stack_masked(x):
        return jnp.concatenate([x * m0, x * m1], axis=0)

    def stack_dup(x):
        return jnp.concatenate([x, x], axis=0)

    kb = kk * a
    nchunk = tb // CHUNK
    inst = [(c, q) for c in range(nchunk) for q in range(pp)]
    rows_of = lambda c: slice(c * CHUNK, (c + 1) * CHUNK)
    last_of = lambda c: slice((c + 1) * CHUNK - 1, (c + 1) * CHUNK)
    a_s = {cq: stack_masked(a_t[rows_of(cq[0]), pair_lanes[cq[1]]]) for cq in inst}
    r_s = {cq: stack_masked(r_t[rows_of(cq[0]), pair_lanes[cq[1]]]) for cq in inst}
    v_s = {cq: stack_masked(v[rows_of(cq[0]), pair_lanes[cq[1]]]) for cq in inst}
    gm = {}
    for c, q in inst:
        sl, ls = rows_of(c), pair_lanes[q]
        right = jnp.concatenate([stack_dup(b_t[sl, ls]), stack_dup(k_t[sl, ls])], axis=0)
        gm[c, q] = _dot_nt(jnp.concatenate([a_s[c, q], r_s[c, q]], axis=0), right)
    n_ab = [jnp.where(strict, gm[cq][0:LANES, 0:LANES], 0.0) for cq in inst]
    t_inv = dict(zip(inst, _neumann_inverse_many(n_ab, eye)))
    gv = {cq: _dot(jnp.where(strict, gm[cq][0:LANES, LANES:], 0.0), v_s[cq]) for cq in inst}
    tu = {cq: _dot(t_inv[cq], jnp.concatenate([a_s[cq], gv[cq]], axis=1)) for cq in inst}
    g_r = {cq: jnp.where(jnp.concatenate([incl, incl], axis=1), gm[cq][LANES:], 0.0) for cq in inst}

    states = [s_ref[q] for q in range(pp)]
    for c in range(nchunk):
        sl = rows_of(c)
        x0 = [_dot_nt(jnp.concatenate([tu[c, q][:, 0:LANES], r_s[c, q]], axis=0), states[q]) for q in range(pp)]
        uv = [jnp.concatenate([x0[q][0:LANES] + tu[c, q][:, LANES:], v_s[c, q]], axis=0) for q in range(pp)]
        for q, ls in enumerate(pair_lanes):
            cum_last = cum[last_of(c), ls]
            e_end = jnp.exp(cum_last - cum[sl, ls])
            bk_end = jnp.concatenate([stack_dup(kb[sl, ls] * e_end), stack_dup(k[sl, ls] * e_end)], axis=0)
            states[q] = jnp.where(same, states[q] * jnp.exp(cum_last) + _dot_tn(uv[q], bk_end), 0.0)
        for q, ls in enumerate(pair_lanes):
            y_s = x0[q][LANES:] + _dot(g_r[c, q], uv[q])
            y_ref[sl, ls] = y_s[0:CHUNK] + y_s[CHUNK:]
    for q in range(pp):
        s_ref[q] = states[q]

    y = y_ref[...]
    inv_n = 1.0 / A_HEAD
    mean = head_sum(y) * inv_n
    d = y - mean
    var = head_sum(d * d) * inv_n
    yn = d * lax.rsqrt(var + A_GN_EPS) * ln_w + ln_b
    bonus = head_sum(r * k * r_k) * v
    out_refs[0][...] = ((yn + bonus) * g).astype(out_refs[0].dtype)


def rwkv_mix(za, v_first, prm, mu2, w2p, a2p, g2p, v2p, *, batch, seq, tb=256, pp=4):
    has_vmix = v_first is not None
    nt = seq // tb
    w = pp * LANES
    npair = A_WIDTH // w
    row = lambda b, p, i: b * nt + i
    if v_first is None:
        v_first = za
    in_specs = [
        pl.BlockSpec((tb, w), lambda b, p, i: (row(b, p, i), p)),
        pl.BlockSpec((tb, w), lambda b, p, i: (row(b, p, i), npair + p)),
        pl.BlockSpec((tb, w), lambda b, p, i: (row(b, p, i), 2 * npair + p)),
        pl.BlockSpec((tb, 256), lambda b, p, i: (row(b, p, i), ZA_GD // 256)),
        pl.BlockSpec((tb, LANES), lambda b, p, i: (row(b, p, i), ZA_WA // LANES)),
        pl.BlockSpec((tb, LANES), lambda b, p, i: (row(b, p, i), ZA_LV // LANES)),
        pl.BlockSpec((tb, w), lambda b, p, i: (row(b, p, i), p)),
        pl.BlockSpec((16, w), lambda b, p, i: (0, p)),
        pl.BlockSpec((1, 384), lambda b, p, i: (0, 0)),
        pl.BlockSpec((LANES, w), lambda b, p, i: (0, p)),
        pl.BlockSpec((LANES, w), lambda b, p, i: (0, p)),
        pl.BlockSpec((256, w), lambda b, p, i: (0, p)),
        pl.BlockSpec((LANES, w), lambda b, p, i: (0, p)),
    ]
    out_spec = pl.BlockSpec((tb, w), lambda b, p, i: (row(b, p, i), p))
    out_dtypes = [BF16] if has_vmix else [BF16, F32]
    res = pl.pallas_call(
        functools.partial(_rwkv_kernel, tb=tb, pp=pp, has_vmix=has_vmix),
        grid=(batch, npair, nt),
        in_specs=in_specs,
        out_specs=[out_spec] * len(out_dtypes),
        out_shape=[jax.ShapeDtypeStruct((batch * seq, A_WIDTH), dt) for dt in out_dtypes],
        scratch_shapes=[
            pltpu.VMEM((pp, LANES, LANES), F32),
            pltpu.VMEM((tb, w), F32),
            pltpu.VMEM((tb + SUBLANES, w), F32),
            pltpu.VMEM((tb + SUBLANES, w), F32),
            pltpu.VMEM((tb + SUBLANES, w), F32),
            pltpu.VMEM((tb + SUBLANES, 256), F32),
            pltpu.VMEM((tb + SUBLANES, LANES), F32),
        ],
        compiler_params=_cparams(("parallel", "parallel", "arbitrary")),
        name="rwkv7",
    )(za, za, za, za, za, za, v_first, prm, mu2, w2p, a2p, g2p, v2p)
    return (res[0], None) if has_vmix else (res[0], res[1])


def _attn_kernel(*refs, tq, dil, nb, has_prev):
    if has_prev:
        (q_ref, kc_ref, kp_ref, vc_ref, vp_ref, cc_ref, sc_ref, nc_ref, cp_ref, sp_ref, np_ref,
         o_ref, lse_ref) = refs
    else:
        q_ref, kc_ref, vc_ref, cc_ref, sc_ref, nc_ref, o_ref, lse_ref = refs
    i = pl.program_id(1)
    scale = B_HEAD ** -0.5

    def rows(j, r):
        start = j * tq * dil + r
        return pl.ds(start, tq, stride=dil) if dil > 1 else pl.ds(start, tq)

    def rope(x, tabs):
        half = ROPE_DIM // 2
        return (x * tabs[0] + pltpu.roll(x, half, axis=1) * tabs[1]
                + pltpu.roll(x, LANES - half, axis=1) * tabs[2])

    if has_prev:
        ri, ci = _iota2((tq, 2 * tq), 0), _iota2((tq, 2 * tq), 1)
        cur_ok = (ci >= tq) & ((ci - tq) <= ri)
        no_prev = jnp.where(i > 0, 0, 2 * tq)
        valid_first = ((ci < tq) & (ci >= ri + no_prev)) | cur_ok
        valid_inner = ((ci < tq) & (ci >= ri)) | cur_ok
    else:
        causal = _iota2((tq, tq), 1) <= _iota2((tq, tq), 0)

    inst = [(r, j) for r in range(dil) for j in range(nb)]
    qs, ks, vs = {}, {}, {}
    for r in range(dil):
        if has_prev:
            rp = rows(0, r)
            ks[r, -1] = rope(kp_ref[rp, :], (cp_ref[rp, :], sp_ref[rp, :], np_ref[rp, :]))
            vs[r, -1] = vp_ref[rp, :]
        for j in range(nb):
            rw = rows(j, r)
            tabs = (cc_ref[rw, :], sc_ref[rw, :], nc_ref[rw, :])
            qs[r, j] = rope(q_ref[rw, :], tabs) * scale
            ks[r, j] = rope(kc_ref[rw, :], tabs)
            vs[r, j] = vc_ref[rw, :]
    if has_prev:
        s = {(r, j): jnp.where(valid_first if j == 0 else valid_inner,
                               _dot_nt(qs[r, j], jnp.concatenate([ks[r, j - 1], ks[r, j]], axis=0)), -1e30)
             for r, j in inst}
    else:
        s = {rj: jnp.where(causal, _dot_nt(qs[rj], ks[rj]), -1e30) for rj in inst}
    m = {rj: jnp.max(s[rj], axis=-1, keepdims=True) for rj in inst}
    p = {rj: jnp.exp(s[rj] - m[rj]) for rj in inst}
    den = {rj: jnp.sum(p[rj], axis=-1, keepdims=True) for rj in inst}
    if has_prev:
        num = {(r, j): _dot(p[r, j], jnp.concatenate([vs[r, j - 1], vs[r, j]], axis=0)) for r, j in inst}
    else:
        num = {rj: _dot(p[rj], vs[rj]) for rj in inst}
    for r, j in inst:
        o_ref[rows(j, r), :] = num[r, j] / den[r, j]
        lse_ref[rows(j, r), :] = jnp.broadcast_to(m[r, j] + jnp.log(den[r, j]), (tq, B_HEAD))


def dilated_attention_group(zb, rope_tabs, gi, *, batch, seq, col0=0, tq=128, blocks_per_step=16):
    win, dil = B_PAIRS[gi]
    assert win // dil == tq
    nq = seq // (dil * tq)
    has_prev = nq > 1
    nb = min(nq, max(1, blocks_per_step // dil))
    rows_step = nb * tq * dil
    nsteps = seq // rows_step
    prev_rows = tq * dil
    nprev = seq // prev_rows
    nh = B_HEADS_PER_GROUP
    cur = lambda col: pl.BlockSpec((rows_step, B_HEAD), lambda b, i, hh: (b * nsteps + i, col0 + col * nh + hh))
    prv = lambda col: pl.BlockSpec(
        (prev_rows, B_HEAD), lambda b, i, hh: (b * nprev + jnp.maximum(i * nb - 1, 0), col0 + col * nh + hh))
    tab_cur = pl.BlockSpec((rows_step, LANES), lambda b, i, hh: (i, 0))
    tab_prv = pl.BlockSpec((prev_rows, LANES), lambda b, i, hh: (jnp.maximum(i * nb - 1, 0), 0))
    qc, kc, vc = gi, B_GROUPS + gi, 2 * B_GROUPS + gi
    if has_prev:
        in_specs = [cur(qc), cur(kc), prv(kc), cur(vc), prv(vc)] + [tab_cur] * 3 + [tab_prv] * 3
        args = (zb, zb, zb, zb, zb) + tuple(rope_tabs) * 2
    else:
        in_specs = [cur(qc), cur(kc), cur(vc)] + [tab_cur] * 3
        args = (zb, zb, zb) + tuple(rope_tabs)
    out_spec = pl.BlockSpec((rows_step, B_HEAD), lambda b, i, hh: (b * nsteps + i, hh))
    out_sds = jax.ShapeDtypeStruct((batch * seq, B_OUT), F32)
    return pl.pallas_call(
        functools.partial(_attn_kernel, tq=tq, dil=dil, nb=nb, has_prev=has_prev),
        grid=(batch, nsteps, nh),
        in_specs=in_specs,
        out_specs=[out_spec, out_spec],
        out_shape=[out_sds, out_sds],
        compiler_params=_cparams(("parallel", "parallel", "arbitrary")),
        name=f"dilated_attn_g{gi}",
    )(*args)


def rope_tables(seq):
    half = ROPE_DIM // 2
    inv = ROPE_THETA ** (-jnp.arange(half, dtype=F32) / half)
    ang = jnp.arange(seq, dtype=F32)[:, None] * inv[None, :]
    cos, sin = jnp.cos(ang), jnp.sin(ang)
    z = jnp.zeros((seq, LANES - ROPE_DIM), F32)
    zh = jnp.zeros((seq, half), F32)
    c_tab = jnp.concatenate([cos, cos, jnp.ones_like(z)], axis=1)
    s_pos = jnp.concatenate([zh, sin, z], axis=1)
    s_neg = jnp.concatenate([-sin, zh, z], axis=1)
    return c_tab, s_pos, s_neg


def _gdn_kernel(q_ref, k_ref, v_ref, gate_ref, ba_ref, cw_ref, p_ref, out_ref, s_ref, o_ref, bq_ref, bk_ref,
                bv_ref, *, tb, pp):
    i = pl.program_id(2)
    h = SUBLANES
    hd = C_HEAD_K

    @pl.when(i == 0)
    def _():
        s_ref[...] = jnp.zeros_like(s_ref)
        for b in (bq_ref, bk_ref, bv_ref):
            b[0:h, :] = jnp.zeros((h, b.shape[1]), F32)

    cw = cw_ref[...]
    prm = p_ref[...]

    def conv_silu(x_ref, buf_ref, w):
        x = x_ref[...]
        buf_ref[h:h + tb, :] = x
        acc = x * w[C_CONV - 1:C_CONV]
        for j in range(C_CONV - 1):
            off = h - (C_CONV - 1) + j
            acc = acc + buf_ref[off:off + tb, :] * w[j:j + 1]
        buf_ref[0:h, :] = x[tb - h:tb, :]
        return acc * _sigmoid(acc)

    q = conv_silu(q_ref, bq_ref, cw[0])
    k = conv_silu(k_ref, bk_ref, cw[1])
    v = conv_silu(v_ref, bv_ref, cw[2])

    ri = _iota2((LANES, LANES), 0)
    ci = _iota2((LANES, LANES), 1)
    same = _chunk_of(ri) == _chunk_of(ci)
    strict = same & (ri > ci)
    incl = same & (ri >= ci)
    eye = jnp.where(ri == ci, 1.0, 0.0)
    tri = _chunk_tri(tb)

    ba = ba_ref[...]
    nh = 2 * pp
    head_lanes = [slice(hh * hd, (hh + 1) * hd) for hh in range(nh)]
    vh = [v[:, ls] for ls in head_lanes]
    q_ss = [jnp.sum(q[:, ls] * q[:, ls], axis=-1, keepdims=True) for ls in head_lanes]
    k_ss = [jnp.sum(k[:, ls] * k[:, ls], axis=-1, keepdims=True) for ls in head_lanes]
    b_raw = [ba[:, hh:hh + 1] for hh in range(nh)]
    a_raw = [ba[:, C_HEADS + hh:C_HEADS + hh + 1] for hh in range(nh)]
    qh = [q[:, ls] / jnp.maximum(jnp.sqrt(ss), 1e-12) * (hd ** -0.5) for ls, ss in zip(head_lanes, q_ss)]
    kh = [k[:, ls] / jnp.maximum(jnp.sqrt(ss), 1e-12) for ls, ss in zip(head_lanes, k_ss)]
    beta = [_sigmoid(b) for b in b_raw]
    glog = [-jnp.exp(prm[0:1, ls]) * _softplus(al + prm[1:2, ls]) for ls, al in zip(head_lanes, a_raw)]
    gam = [_dot_sel_l(tri, gl) for gl in glog]

    nchunk = tb // CHUNK
    inst = [(c, pr) for c in range(nchunk) for pr in range(pp)]
    rows_of = lambda c: slice(c * CHUNK, (c + 1) * CHUNK)

    def stack(xs, cp):
        c, pr = cp
        return jnp.concatenate([xs[2 * pr][rows_of(c)], xs[2 * pr + 1][rows_of(c)]], axis=0)

    k_s = {cp: stack(kh, cp) for cp in inst}
    q_s = {cp: stack(qh, cp) for cp in inst}
    beta_s = {cp: stack(beta, cp) for cp in inst}
    gam_s = {cp: stack(gam, cp) for cp in inst}
    kq = {cp: _dot_nt(jnp.concatenate([k_s[cp], q_s[cp]], axis=0), k_s[cp]) for cp in inst}
    dm = {cp: jnp.exp(jnp.where(incl, gam_s[cp] - gam_s[cp].T, -1e30)) for cp in inst}
    n_mats = [jnp.where(strict, -(beta_s[cp] * kq[cp][0:LANES] * dm[cp]), 0.0) for cp in inst]
    t_inv = dict(zip(inst, _neumann_inverse_many(n_mats, eye)))
    e_gam = {cp: jnp.exp(gam_s[cp]) for cp in inst}
    sol = {cp: _dot(t_inv[cp], jnp.concatenate([stack(vh, cp) * beta_s[cp],
                                                k_s[cp] * (beta_s[cp] * e_gam[cp])], axis=1)) for cp in inst}

    states = [s_ref[hh] for hh in range(nh)]
    for c in range(nchunk):
        sl = rows_of(c)
        ws = []
        for hh in range(nh):
            cp, hs = (c, hh // 2), slice((hh % 2) * CHUNK, (hh % 2 + 1) * CHUNK)
            qg = (q_s[cp] * e_gam[cp])[hs]
            ws.append(_dot(jnp.concatenate([sol[cp][hs, hd:], qg], axis=0), states[hh]))
        v_new = [sol[c, hh // 2][(hh % 2) * CHUNK:(hh % 2 + 1) * CHUNK, 0:hd] - ws[hh][0:CHUNK]
                 for hh in range(nh)]
        for hh in range(nh):
            g_h = gam[hh][sl]
            g_last = g_h[CHUNK - 1:CHUNK, :]
            states[hh] = states[hh] * jnp.exp(g_last) + _dot_tn(kh[hh][sl] * jnp.exp(g_last - g_h), v_new[hh])
        for pr in range(pp):
            h0 = 2 * pr
            attn = kq[c, pr][LANES:] * dm[c, pr]
            o_s = (jnp.concatenate([ws[h0][CHUNK:], ws[h0 + 1][CHUNK:]], axis=0)
                   + _dot(attn, jnp.concatenate([v_new[h0], v_new[h0 + 1]], axis=0)))
            o_ref[sl, h0 * hd:(h0 + 1) * hd] = o_s[0:CHUNK]
            o_ref[sl, (h0 + 1) * hd:(h0 + 2) * hd] = o_s[CHUNK:]
    for hh in range(nh):
        s_ref[hh] = states[hh]

    gate = gate_ref[...]
    o_h = [o_ref[:, ls] for ls in head_lanes]
    o_ms = [jnp.mean(o * o, axis=-1, keepdims=True) for o in o_h]
    for o, ms, ls in zip(o_h, o_ms, head_lanes):
        gt = gate[:, ls]
        out_ref[:, ls] = ((o * lax.rsqrt(ms + NORM_EPS) * prm[2:3, ls]) * (gt * _sigmoid(gt))).astype(out_ref.dtype)


def gated_deltanet(zc, conv_w, prm, *, batch, seq, tb=256, pp=4):
    assert 2 * pp == C_HEADS, "the kernel indexes the per-head beta/alpha columns statically"
    nt = seq // tb
    npair = C_HEADS // (2 * pp)
    wblk = 2 * pp * C_HEAD_K
    row = lambda b, p, i: b * nt + i
    in_specs = [
        pl.BlockSpec((tb, wblk), lambda b, p, i: (row(b, p, i), p)),
        pl.BlockSpec((tb, wblk), lambda b, p, i: (row(b, p, i), npair + p)),
        pl.BlockSpec((tb, wblk), lambda b, p, i: (row(b, p, i), 2 * npair + p)),
        pl.BlockSpec((tb, wblk), lambda b, p, i: (row(b, p, i), 3 * npair + p)),
        pl.BlockSpec((tb, LANES), lambda b, p, i: (row(b, p, i), ZC_BA // LANES)),
        pl.BlockSpec((3, C_CONV, wblk), lambda b, p, i: (0, 0, p)),
        pl.BlockSpec((SUBLANES, wblk), lambda b, p, i: (0, p)),
    ]
    return pl.pallas_call(
        functools.partial(_gdn_kernel, tb=tb, pp=pp),
        grid=(batch, npair, nt),
        in_specs=in_specs,
        out_specs=pl.BlockSpec((tb, wblk), lambda b, p, i: (row(b, p, i), p)),
        out_shape=jax.ShapeDtypeStruct((batch * seq, C_VW), BF16),
        scratch_shapes=[
            pltpu.VMEM((2 * pp, C_HEAD_K, C_HEAD_V), F32),
            pltpu.VMEM((tb, wblk), F32),
            pltpu.VMEM((tb + SUBLANES, wblk), F32),
            pltpu.VMEM((tb + SUBLANES, wblk), F32),
            pltpu.VMEM((tb + SUBLANES, wblk), F32),
        ],
        compiler_params=_cparams(("parallel", "parallel", "arbitrary")),
        name="gated_deltanet",
    )(zc, zc, zc, zc, zc, conv_w, prm)


def _merge_out_kernel(ya_ref, o0_ref, o1_ref, o2_ref, l0_ref, l1_ref, l2_ref, yc_ref, ga_ref, gb_ref, gc_ref,
                      pa_ref, pb_ref, pc_ref, wo_ref, x_ref, g_ref, y_ref, h_ref):
    l0, l1, l2 = l0_ref[...], l1_ref[...], l2_ref[...]
    m = jnp.maximum(jnp.maximum(l0, l1), l2)
    w0, w1, w2 = jnp.exp(l0 - m), jnp.exp(l1 - m), jnp.exp(l2 - m)
    yb = (w0 * o0_ref[...] + w1 * o1_ref[...] + w2 * o2_ref[...]) / (w0 + w1 + w2)
    merged = (_sigmoid(ga_ref[...]) * _dot(ya_ref[...], pa_ref[...])
              + _sigmoid(gb_ref[...]) * _dot(yb, pb_ref[...])
              + _sigmoid(gc_ref[...]) * _dot(yc_ref[...], pc_ref[...]))
    y = x_ref[...] + _dot(merged, wo_ref[...])
    ms = jnp.mean(y * y, axis=-1, keepdims=True)
    y_ref[...] = y
    h_ref[...] = ((y * lax.rsqrt(ms + NORM_EPS)) * g_ref[...]).astype(h_ref.dtype)


def merge_out_proj(ya, attn, yc, zg, pa, pb, pc, w_out, layer, x, gain, tm=256):
    m = ya.shape[0]
    d = D_MODEL
    rows = lambda w: pl.BlockSpec((tm, w), lambda i: (i, 0))
    const = lambda a: pl.BlockSpec(a.shape, lambda i: (0, 0), pipeline_mode=pl.Buffered(1))
    (o0, l0), (o1, l1), (o2, l2) = attn
    in_specs = ([rows(A_WIDTH)] + [rows(B_OUT)] * 6 + [rows(C_VW)]
                + [pl.BlockSpec((tm, d), lambda i, j=j: (i, j)) for j in range(3)]
                + [const(pa), const(pb), const(pc),
                   pl.BlockSpec((d, d), lambda i: (layer, 0), pipeline_mode=pl.Buffered(1)),
                   rows(d), pl.BlockSpec((1, d), lambda i: (0, 0))])
    return pl.pallas_call(
        _merge_out_kernel,
        grid=(m // tm,),
        in_specs=in_specs,
        out_specs=[rows(d), rows(d)],
        out_shape=[jax.ShapeDtypeStruct((m, d), F32), jax.ShapeDtypeStruct((m, d), BF16)],
        compiler_params=_cparams(("parallel",)),
        name="merge_out_proj",
    )(ya, o0, o1, o2, l0, l1, l2, yc, zg, zg, zg, pa, pb, pc, w_out, x, gain.reshape(1, d))


def _ffn_up_act_kernel(x_ref, wg_ref, wv_ref, cw_ref, o_ref, buf_ref, halo_ref, *, tm, tf, blocks_per_seq):
    j = pl.program_id(0)
    i = pl.program_id(1)
    h = SUBLANES
    first = (i % blocks_per_seq) == 0

    @pl.when(first)
    def _():
        buf_ref[0:h, :] = jnp.zeros((h, 2 * tf), F32)

    @pl.when(jnp.logical_not(first))
    def _():
        buf_ref[0:h, :] = halo_ref[j]

    cw = cw_ref[...]

    def project(s):
        cs = slice(s * LANES, (s + 1) * LANES)
        w = jnp.concatenate([wg_ref[:, cs], wv_ref[:, cs]], axis=1)
        return jnp.dot(x_ref[...], w, preferred_element_type=F32)

    def conv(u, cols):
        buf_ref[h:h + tm, cols] = u
        halo_ref[j, :, cols] = u[tm - h:tm, :]
        acc = u * cw[FFN_CONV - 1:FFN_CONV, cols]
        for t in range(FFN_CONV - 1):
            off = h - (FFN_CONV - 1) + t
            acc = acc + buf_ref[off:off + tm, cols] * cw[t:t + 1, cols]
        return acc

    def gate(s, u):
        cg = conv(u[:, 0:LANES], slice(s * LANES, (s + 1) * LANES))
        cv = conv(u[:, LANES:], slice(tf + s * LANES, tf + (s + 1) * LANES))
        o_ref[:, s * LANES:(s + 1) * LANES] = ((cg * _sigmoid(cg)) * cv).astype(o_ref.dtype)

    nsub = tf // LANES
    pending = project(0)
    for s in range(nsub):
        nxt = project(s + 1) if s + 1 < nsub else None
        gate(s, pending)
        pending = nxt


def ffn_up_act(h2, w_up, layer, conv_w, *, seq, tm=1024, tf=FFN_TF):
    m, d = h2.shape
    tm = min(tm, seq)
    nf = D_FF // tf
    return pl.pallas_call(
        functools.partial(_ffn_up_act_kernel, tm=tm, tf=tf, blocks_per_seq=seq // tm),
        grid=(nf, m // tm),
        in_specs=[
            pl.BlockSpec((tm, d), lambda j, i: (i, 0)),
            pl.BlockSpec((d, tf), lambda j, i: (layer, j)),
            pl.BlockSpec((d, tf), lambda j, i: (layer, nf + j)),
            pl.BlockSpec((None, FFN_CONV, 2 * tf), lambda j, i: (j, 0, 0)),
        ],
        out_specs=pl.BlockSpec((tm, tf), lambda j, i: (i, j)),
        out_shape=jax.ShapeDtypeStruct((m, D_FF), BF16),
        scratch_shapes=[
            pltpu.VMEM((tm + SUBLANES, 2 * tf), F32),
            pltpu.VMEM((nf, SUBLANES, 2 * tf), F32),
        ],
        compiler_params=_cparams(("arbitrary", "arbitrary")),
        name="ffn_up_act",
    )(h2, w_up, w_up, conv_w)


WT_ZA, WT_ZB, WT_ZC, WT_ZG = 0, ZA_W, ZA_W + B_IN, ZA_W + B_IN + ZC_W
WT_TILES_PER_STEP = 4
WT_ROWS = -(-(WT_ZG + 3 * D_MODEL) // (WT_TILES_PER_STEP * LANES)) * (WT_TILES_PER_STEP * LANES)


def _w_in_tile_table():
    aw = 3 * A_WIDTH
    lora = A_DECAY_LORA + A_ICLR_LORA
    c0 = A_IN + B_IN
    qkv = 2 * C_KW + C_VW
    runs = [(0, aw), (aw + lora, A_GATE_LORA), (None, ZA_WA - ZA_GD - A_GATE_LORA), (aw, lora),
            (None, ZA_W - ZA_WA - lora),
            (A_IN, B_IN),
            (c0, qkv), (c0 + qkv + 2 * C_HEADS, C_VW), (c0 + qkv, 2 * C_HEADS), (None, ZC_W - ZC_BA - 2 * C_HEADS),
            (c0 + C_IN, 3 * D_MODEL)]
    src, valid = [], []
    pending = 0
    for start, n in runs:
        if start is None:
            assert pending + n == LANES or pending == 0 and n % LANES == 0
            if pending == 0:
                src += [0] * (n // LANES)
                valid += [0] * (n // LANES)
            pending = 0
            continue
        assert pending == 0
        for off in range(0, n, LANES):
            src.append(start + off)
            valid.append(min(LANES, n - off))
        pending = n % LANES
    assert pending == 0 and len(src) == (WT_ZG + 3 * D_MODEL) // LANES
    tail = WT_ROWS // LANES - len(src)
    return src + [0] * tail, valid + [0] * tail


def _prep_w_in_kernel(src_ref, valid_ref, *refs):
    *w_refs, lora_ref, o_ref = refs
    g_n = len(w_refs)
    t = pl.program_id(0)
    rows = _iota2(w_refs[0].shape, 0)
    lv_tile = (WT_ZA + ZA_LV) // LANES
    for g, w_ref in enumerate(w_refs):
        tile = jnp.where(rows < valid_ref[t * g_n + g], w_ref[...], 0.0)
        if g == lv_tile % g_n:
            tile = jnp.where(t == lv_tile // g_n, lora_ref[...], tile)
        o_ref[g * LANES:(g + 1) * LANES, :] = tile.astype(BF16)


def prep_w_in(w_in, l, lora_t):
    depth, d, n_in = w_in.shape
    wt = jnp.swapaxes(w_in, 1, 2).reshape(depth * n_in, d)
    src, valid = _w_in_tile_table()
    src = jnp.asarray(src, jnp.int32) + l * n_in
    valid = jnp.asarray(valid, jnp.int32)
    g = WT_TILES_PER_STEP
    tile_spec = lambda k: pl.BlockSpec(
        (pl.Element(LANES), pl.Element(d)),
        lambda t, src, valid: (pl.multiple_of(src[t * g + k], 2 * SUBLANES), 0))
    return pl.pallas_call(
        _prep_w_in_kernel,
        grid_spec=pltpu.PrefetchScalarGridSpec(
            num_scalar_prefetch=2, grid=(WT_ROWS // (g * LANES),),
            in_specs=[tile_spec(k) for k in range(g)] + [pl.BlockSpec((LANES, d), lambda t, src, valid: (0, 0))],
            out_specs=pl.BlockSpec((g * LANES, d), lambda t, src, valid: (t, 0))),
        out_shape=jax.ShapeDtypeStruct((WT_ROWS, d), BF16),
        compiler_params=_cparams(("parallel",)),
        name="prep_w_in",
    )(src, valid, *([wt] * g), lora_t)


def matmul_nt(x, wt, row0, n, tn, tm=2048, name="matmul_nt"):
    m, k = x.shape
    return pl.pallas_call(
        _matmul_nt_kernel,
        grid=(m // tm, n // tn),
        in_specs=[pl.BlockSpec((tm, k), lambda i, j: (i, 0)),
                  pl.BlockSpec((pl.Element(tn), pl.Element(k)),
                               lambda i, j: (pl.multiple_of(row0 + j * tn, 2 * SUBLANES), 0))],
        out_specs=pl.BlockSpec((tm, tn), lambda i, j: (i, j)),
        out_shape=jax.ShapeDtypeStruct((m, n), F32),
        compiler_params=_cparams(("parallel", "parallel")),
        name=name,
    )(x, wt)


def _pad_cols(w, width):
    return jnp.pad(w, ((0, 0), (0, width - w.shape[1])))


def _pad_rows(w, rows, at=0):
    return jnp.pad(w, ((at, rows - at - w.shape[0]), (0, 0)))


def _layer_params(l, p):
    aw = 3 * A_WIDTH
    mu = p["rwkv_mu"][l]
    rows = [p["rwkv_w0"][l], p["rwkv_a0"][l], p["rwkv_k_k"][l], p["rwkv_k_a"][l], p["rwkv_r_k"][l].reshape(-1),
            p["rwkv_ln_w"][l], p["rwkv_ln_b"][l],
            (p["rwkv_v0"][l - 1] if l > 0 else jnp.zeros((A_WIDTH,), F32)),
            mu[:A_WIDTH], mu[A_WIDTH:2 * A_WIDTH], mu[2 * A_WIDTH:aw]]
    rwkv_prm = jnp.pad(jnp.stack(rows), ((0, 16 - len(rows)), (0, 0)))
    mu2 = jnp.concatenate([_pad_cols(mu[None, aw + 128:], 256), mu[None, aw:aw + 128]], axis=1)
    v2 = (p["rwkv_v2"][l - 1] if l > 0 else jnp.zeros((A_VRES_LORA, A_WIDTH), F32))
    bcast = lambda t: jnp.repeat(t, C_HEAD_K)
    gdn_prm = jnp.pad(jnp.stack([bcast(p["gdn_A_log"][l]), bcast(p["gdn_dt_bias"][l]),
                                 jnp.tile(p["gdn_norm"][l], C_HEADS)]), ((0, SUBLANES - 3), (0, 0)))
    fc = p["ffn_conv"][l]
    return dict(
        lora_t=(_pad_rows(p["rwkv_v1"][l - 1].T, LANES) if l > 0 else jnp.zeros((LANES, D_MODEL), F32)),
        rwkv_prm=rwkv_prm, mu2=mu2,
        w2p=_pad_rows(p["rwkv_w2"][l], 128, 0).astype(BF16),
        a2p=_pad_rows(p["rwkv_a2"][l], 128, A_DECAY_LORA).astype(BF16),
        g2p=_pad_rows(p["rwkv_g2"][l], 256, 0).astype(BF16),
        v2p=_pad_rows(v2, 128, 0).astype(BF16),
        gdn_conv=p["gdn_conv"][l].reshape(C_CONV, 3, C_KW).transpose(1, 0, 2),
        gdn_prm=gdn_prm,
        pa=p["proj_a"][l].astype(BF16), pb=p["proj_b"][l].astype(BF16), pc=p["proj_c"][l].astype(BF16),
        ffn_conv=fc.reshape(FFN_CONV, 2, D_FF // FFN_TF, FFN_TF).transpose(2, 0, 1, 3).reshape(
            D_FF // FFN_TF, FFN_CONV, 2 * FFN_TF),
    )


def _forward(x, p):
    batch, seq, d = x.shape
    m = batch * seq
    xf = x.reshape(m, d)
    tab = rope_tables(seq)
    v_first = None
    stack_rows = lambda w: w.astype(BF16).reshape(w.shape[0] * w.shape[1], w.shape[2])
    w_out_all, ffn_up_all, ffn_down_all = stack_rows(p["w_out"]), stack_rows(p["ffn_up"]), stack_rows(p["ffn_down"])
    h = rmsnorm(xf, p["attn_norm"][0], BF16)
    for l in range(DEPTH):
        lp = _layer_params(l, p)
        wt = prep_w_in(p["w_in"], l, lp["lora_t"])
        zab = matmul_nt(h, wt, WT_ZA, ZA_W + B_IN, tn=1024, name="in_proj_ab")
        zc = matmul_nt(h, wt, WT_ZC, ZC_W, tn=1408, tm=1024, name="in_proj_c")
        zg = matmul_nt(h, wt, WT_ZG, 3 * D_MODEL, tn=1024, name="in_proj_g")
        ya, v_l = rwkv_mix(zab, v_first, lp["rwkv_prm"], lp["mu2"], lp["w2p"], lp["a2p"], lp["g2p"], lp["v2p"],
                           batch=batch, seq=seq)
        if l == 0:
            v_first = v_l
        attn = [dilated_attention_group(zab, tab, gi, batch=batch, seq=seq, col0=ZA_W // B_HEAD)
                for gi in range(B_GROUPS)]
        yc = gated_deltanet(zc, lp["gdn_conv"], lp["gdn_prm"], batch=batch, seq=seq)
        xf, h2 = merge_out_proj(ya, attn, yc, zg, lp["pa"], lp["pb"], lp["pc"], w_out_all, l, xf, p["ffn_norm"][l])
        act = ffn_up_act(h2, ffn_up_all, l, lp["ffn_conv"], seq=seq)
        if l + 1 < DEPTH:
            xf, h = matmul_res_norm(act, ffn_down_all, l, xf, p["attn_norm"][l + 1], tm=256, norm_dtype=BF16,
                                    emit_sum=True, name="ffn_down_norm")
        else:
            out = matmul_res_norm(act, ffn_down_all, l, xf, p["final_norm"], tm=256, norm_dtype=F32,
                                  emit_sum=False, name="ffn_down_norm")
    return out.reshape(batch, seq, d)


def kernel(x, attn_norm, w_in, rwkv_mu, rwkv_w0, rwkv_w2, rwkv_a0, rwkv_a2, rwkv_g2, rwkv_k_k, rwkv_k_a, rwkv_r_k, rwkv_ln_w, rwkv_ln_b, rwkv_v0, rwkv_v1, rwkv_v2, gdn_conv, gdn_A_log, gdn_dt_bias, gdn_norm, proj_a, proj_b, proj_c, w_out, ffn_norm, ffn_up, ffn_conv, ffn_down, final_norm):
    params = dict(
        attn_norm=attn_norm, w_in=w_in, rwkv_mu=rwkv_mu, rwkv_w0=rwkv_w0, rwkv_w2=rwkv_w2, rwkv_a0=rwkv_a0,
        rwkv_a2=rwkv_a2, rwkv_g2=rwkv_g2, rwkv_k_k=rwkv_k_k, rwkv_k_a=rwkv_k_a, rwkv_r_k=rwkv_r_k,
        rwkv_ln_w=rwkv_ln_w, rwkv_ln_b=rwkv_ln_b, rwkv_v0=rwkv_v0, rwkv_v1=rwkv_v1, rwkv_v2=rwkv_v2,
        gdn_conv=gdn_conv, gdn_A_log=gdn_A_log, gdn_dt_bias=gdn_dt_bias, gdn_norm=gdn_norm, proj_a=proj_a,
        proj_b=proj_b, proj_c=proj_c, w_out=w_out, ffn_norm=ffn_norm, ffn_up=ffn_up, ffn_conv=ffn_conv,
        ffn_down=ffn_down, final_norm=final_norm)
    return _forward(x, params)
```
